```python
import jax, jax.numpy as jnp
from jax import lax
import numpy as np


D_MODEL = 2048
BATCH = 16
SEQ = 2048
DEPTH = 2
DEC_BATCH = 16
DEC_SEQ = 32
PAST_LEN = 4096

CHUNK = 64
N_A_LAYERS = (DEPTH + 1) // 2
N_C_LAYERS = DEPTH // 2
MLSTM_HEADS = 8
MLSTM_DV = D_MODEL // MLSTM_HEADS
MLSTM_DK = MLSTM_DV // 2
MLSTM_WIDTH = MLSTM_HEADS * MLSTM_DV
MLSTM_QK_WIDTH = MLSTM_HEADS * MLSTM_DK
LRU_WIDTH = D_MODEL
LRU_BLOCKS = 16
LRU_BLOCK = LRU_WIDTH // LRU_BLOCKS
CONV_W = 4
LRU_C = 8.0
IN_A_SPLITS = (MLSTM_QK_WIDTH, MLSTM_QK_WIDTH, MLSTM_WIDTH, MLSTM_WIDTH, MLSTM_HEADS, MLSTM_HEADS, LRU_WIDTH, LRU_WIDTH)
IN_A_WIDTH = 2 * MLSTM_QK_WIDTH + 2 * MLSTM_WIDTH + 2 * MLSTM_HEADS + 2 * LRU_WIDTH
OUT_A_WIDTH = MLSTM_WIDTH + LRU_WIDTH
RWKV_HEAD = 64
RWKV_HEADS = D_MODEL // RWKV_HEAD
DECAY_LORA = 96
AAA_LORA = 96
GATE_LORA = 256
RWKV_GN_EPS = 64e-5
D_FF = 4 * D_MODEL
DN_ALPHA = (2 * DEPTH) ** 0.25
DN_BETA = (8 * DEPTH) ** -0.25
LN_EPS = 1e-5

kernel_name = 'hybrid_mlstm_rglru_rwkv7_stream_step'

F32 = jnp.float32


def _split_points(sizes):
    pts, acc = [], 0
    for s in sizes[:-1]:
        acc += s
        pts.append(acc)
    return pts


def layer_norm(x, g, b, eps=LN_EPS):
    xf = x.astype(F32)
    mu = jnp.mean(xf, -1, keepdims=True)
    var = jnp.mean(jnp.square(xf - mu), -1, keepdims=True)
    return (xf - mu) * lax.rsqrt(var + eps) * g + b


def head_norm(y, eps):
    mu = jnp.mean(y, -1, keepdims=True)
    var = jnp.mean(jnp.square(y - mu), -1, keepdims=True)
    return (y - mu) * lax.rsqrt(var + eps)


def squared_relu_mlp(x, w1, w2):
    h = jax.nn.relu(x @ w1)
    return (h * h) @ w2


def mlstm_chunk(C, n, m, q, k, v, ig, lf):
    L = q.shape[2]
    b = jnp.cumsum(lf, axis=-1)
    causal = jnp.tril(jnp.ones((L, L), bool))
    D = jnp.where(causal, b[..., :, None] - b[..., None, :] + ig[..., None, :], -jnp.inf)
    inter = b + m[..., None]
    m_t = jnp.maximum(inter, jnp.max(D, -1))
    w_intra = jnp.exp(D - m_t[..., None])
    w_inter = jnp.exp(inter - m_t)
    qk = jnp.einsum('bhtk,bhsk->bhts', q, k) * w_intra
    num = jnp.einsum('bhts,bhsv->bhtv', qk, v) + w_inter[..., None] * jnp.einsum('bhtk,bhkv->bhtv', q, C)
    den = jnp.sum(qk, -1) + w_inter * jnp.einsum('bhtk,bhk->bht', q, n)
    h = num / jnp.maximum(jnp.abs(den), jnp.exp(-m_t))[..., None]
    b_L = b[..., -1]
    lw = b_L[..., None] - b + ig
    m_new = jnp.maximum(b_L + m, jnp.max(lw, -1))
    ws = jnp.exp(lw - m_new[..., None])
    wc = jnp.exp(b_L + m - m_new)
    C_new = wc[..., None, None] * C + jnp.einsum('bhs,bhsk,bhsv->bhkv', ws, k, v)
    n_new = wc[..., None] * n + jnp.einsum('bhs,bhsk->bhk', ws, k)
    return h, C_new, n_new, m_new


def mlstm_sequence(C, n, m, q, k, v, ig, lf):
    T = q.shape[2]
    if T <= CHUNK:
        return mlstm_chunk(C, n, m, q, k, v, ig, lf)
    nc = T // CHUNK

    def to_chunks(a):
        return jnp.moveaxis(a.reshape(a.shape[:2] + (nc, CHUNK) + a.shape[3:]), 2, 0)

    def step(carry, xs):
        h, C1, n1, m1 = mlstm_chunk(*carry, *xs)
        return (C1, n1, m1), h

    (C, n, m), h = lax.scan(step, (C, n, m), (to_chunks(q), to_chunks(k), to_chunks(v), to_chunks(ig), to_chunks(lf)))
    h = jnp.moveaxis(h, 0, 2).reshape(q.shape[:3] + (v.shape[-1],))
    return h, C, n, m


def causal_conv(x, buf, w, bias):
    T = x.shape[1]
    xp = jnp.concatenate([buf.astype(x.dtype), x], axis=1)
    y = sum(xp[:, j:j + T] * w[j] for j in range(CONV_W)) + bias
    return y, xp[:, -(CONV_W - 1):]


def _lin_combine(c1, c2):
    a1, b1 = c1
    a2, b2 = c2
    return a1 * a2, a2 * b1 + b2


def rglru(x, h0, w_a, b_a, w_x, b_x, lam, reset_first):
    B, T, W = x.shape
    xf = x.astype(F32)
    xb = xf.reshape(B, T, LRU_BLOCKS, LRU_BLOCK)
    gate_r = jax.nn.sigmoid(jnp.einsum('btgi,gij->btgj', xb, w_a).reshape(B, T, W) + b_a)
    gate_i = jax.nn.sigmoid(jnp.einsum('btgi,gij->btgj', xb, w_x).reshape(B, T, W) + b_x)
    log_a = -LRU_C * gate_r * jax.nn.softplus(-lam.astype(F32))
    a = jnp.exp(log_a)
    mult = jnp.sqrt(-jnp.expm1(2.0 * log_a))
    if reset_first:
        mult = jnp.where((jnp.arange(T) == 0)[None, :, None], 1.0, mult)
    b = mult * gate_i * xf
    b = b.at[:, 0].add(a[:, 0] * h0.astype(F32))
    _, h = lax.associative_scan(_lin_combine, (a, b), axis=1)
    return h, h[:, -1]


def mlstm_rglru_mixer(x, st, p, li, reset_first):
    C0, n0, m0, conv0, h0 = st
    B, T, _ = x.shape
    u = x @ p['a_w_in'][li]
    q, k, v, o, ig, fg, xr, yg = jnp.split(u, _split_points(IN_A_SPLITS), axis=-1)

    def to_heads(a):
        return a.reshape(B, T, MLSTM_HEADS, -1).transpose(0, 2, 1, 3).astype(F32)

    q = to_heads(q) * MLSTM_DK ** -0.5
    k = to_heads(k)
    v = to_heads(v)
    ig = (ig + p['a_b_ig'][li]).astype(F32).transpose(0, 2, 1)
    lf = jax.nn.log_sigmoid((fg + p['a_b_fg'][li]).astype(F32)).transpose(0, 2, 1)
    hm, C1, n1, m1 = mlstm_sequence(C0.astype(F32), n0.astype(F32), m0.astype(F32), q, k, v, ig, lf)
    hm = head_norm(hm, LN_EPS).transpose(0, 2, 1, 3).reshape(B, T, MLSTM_WIDTH)
    hm = hm * p['a_mlstm_norm'][li] * jax.nn.sigmoid(o)
    xc, conv1 = causal_conv(xr, conv0, p['a_conv_w'][li], p['a_conv_b'][li])
    hl, h1 = rglru(xc, h0, p['a_lru_wa'][li], p['a_lru_ba'][li], p['a_lru_wx'][li], p['a_lru_bx'][li],
                   p['a_lru_lambda'][li], reset_first)
    yb = hl * jax.nn.gelu(yg)
    out = jnp.concatenate([hm, yb], axis=-1) @ p['a_w_out'][li]
    return out, (C1, n1, m1, conv1, h1)


def rwkv7_mixer(x, shift, S0, p, li):
    B, T, D = x.shape
    H, N = RWKV_HEADS, RWKV_HEAD
    xx = jnp.concatenate([shift[:, None, :].astype(x.dtype), x[:, :-1]], axis=1) - x
    mu = p['c_mu'][li]
    xr, xw, xk, xv, xa, xg = (x + xx * mu[j] for j in range(6))
    r = xr @ p['c_w_r'][li]
    k = xk @ p['c_w_k'][li]
    v = xv @ p['c_w_v'][li]
    w_log = -jax.nn.softplus(-(p['c_w0'][li] + jnp.tanh(xw @ p['c_w1'][li]) @ p['c_w2'][li]).astype(F32)) - 0.5
    decay = jnp.exp(-jnp.exp(w_log))
    a = jax.nn.sigmoid((p['c_a0'][li] + (xa @ p['c_a1'][li]) @ p['c_a2'][li]).astype(F32))
    g = jax.nn.sigmoid(xg @ p['c_g1'][li]) @ p['c_g2'][li]

    def heads(t):
        return t.astype(F32).reshape(B, T, H, N)

    kk = heads(k * p['c_k_k'][li])
    kk = kk * lax.rsqrt(jnp.maximum(jnp.sum(kk * kk, -1, keepdims=True), 1e-24))
    k = k.astype(F32) * (1.0 + (a - 1.0) * p['c_k_a'][li])
    r, k, v, a, decay = heads(r), heads(k), heads(v), heads(a), heads(decay)

    def step(S, inp):
        r_t, k_t, v_t, kk_t, a_t, w_t = inp
        sa = jnp.einsum('bhvk,bhk->bhv', S, -kk_t)
        S = S * w_t[:, :, None, :] + sa[..., None] * (kk_t * a_t)[:, :, None, :] + v_t[..., None] * k_t[:, :, None, :]
        return S, jnp.einsum('bhvk,bhk->bhv', S, r_t)

    def tmaj(t):
        return jnp.moveaxis(t, 1, 0)

    S1, y = lax.scan(step, S0.astype(F32), (tmaj(r), tmaj(k), tmaj(v), tmaj(kk), tmaj(a), tmaj(decay)))
    y = head_norm(jnp.moveaxis(y, 0, 1), RWKV_GN_EPS).reshape(B, T, D) * p['c_gn_g'][li] + p['c_gn_b'][li]
    bonus = (jnp.sum(r * k * p['c_r_k'][li], -1, keepdims=True) * v).reshape(B, T, D)
    out = ((y + bonus) * g) @ p['c_w_o'][li]
    return out, (x[:, -1], S1)


def trunk(x, states, p, reset_first):
    mC, mn, mm, cv, hl, sh, S = states
    new_a, new_c = [], []
    for layer in range(DEPTH):
        li = layer // 2
        if layer % 2 == 0:
            mix, st = mlstm_rglru_mixer(x, (mC[li], mn[li], mm[li], cv[li], hl[li]), p, li, reset_first)
            new_a.append(st)
        else:
            mix, st = rwkv7_mixer(x, sh[li], S[li], p, li)
            new_c.append(st)
        x = layer_norm(DN_ALPHA * x + mix, p['ln1_g'][layer], p['ln1_b'][layer])
        x = layer_norm(DN_ALPHA * x + squared_relu_mlp(x, p['mlp_w1'][layer], p['mlp_w2'][layer]),
                       p['ln2_g'][layer], p['ln2_b'][layer])
    sa = [jnp.stack([s[j] for s in new_a]) for j in range(5)]
    sc = [jnp.stack([s[j] for s in new_c]) for j in range(2)]
    return x, sa + sc


def setup_inputs(seed: int = 0) -> dict:
    key = jax.random.key(seed)
    ks = iter(jax.random.split(key, 64))

    def nrm(shape, scale):
        return jax.random.normal(next(ks), shape, F32) * scale

    def unif(shape, lo, hi):
        return jax.random.uniform(next(ks), shape, F32, lo, hi)

    NA, NC, D = N_A_LAYERS, N_C_LAYERS, D_MODEL
    inp = {}
    inp['x_prompt'] = nrm((BATCH, SEQ, D), 1.0)
    inp['x_sample'] = nrm((DEC_BATCH, DEC_SEQ, D), 1.0)
    inp['state_mlstm_C'] = nrm((NA, DEC_BATCH, MLSTM_HEADS, MLSTM_DK, MLSTM_DV), 0.1)
    inp['state_mlstm_n'] = nrm((NA, DEC_BATCH, MLSTM_HEADS, MLSTM_DK), 0.1)
    inp['state_mlstm_m'] = nrm((NA, DEC_BATCH, MLSTM_HEADS), 0.5)
    inp['state_lru_conv'] = nrm((NA, DEC_BATCH, CONV_W - 1, LRU_WIDTH), 1.0)
    inp['state_lru_h'] = nrm((NA, DEC_BATCH, LRU_WIDTH), 0.5)
    inp['state_rwkv_shift'] = nrm((NC, DEC_BATCH, D), 1.0)
    inp['state_rwkv_S'] = nrm((NC, DEC_BATCH, RWKV_HEADS, RWKV_HEAD, RWKV_HEAD), 0.1)
    inp['a_w_in'] = nrm((NA, D, IN_A_WIDTH), D ** -0.5)
    inp['a_b_ig'] = nrm((NA, MLSTM_HEADS), 0.1)
    inp['a_b_fg'] = jnp.broadcast_to(jnp.linspace(3.0, 6.0, MLSTM_HEADS, dtype=F32), (NA, MLSTM_HEADS)) + nrm((NA, MLSTM_HEADS), 0.1)
    inp['a_mlstm_norm'] = 1.0 + nrm((NA, MLSTM_WIDTH), 0.02)
    inp['a_conv_w'] = nrm((NA, CONV_W, LRU_WIDTH), CONV_W ** -0.5)
    inp['a_conv_b'] = nrm((NA, LRU_WIDTH), 0.02)
    inp['a_lru_wa'] = nrm((NA, LRU_BLOCKS, LRU_BLOCK, LRU_BLOCK), LRU_BLOCK ** -0.5)
    inp['a_lru_ba'] = nrm((NA, LRU_WIDTH), 0.02)
    inp['a_lru_wx'] = nrm((NA, LRU_BLOCKS, LRU_BLOCK, LRU_BLOCK), LRU_BLOCK ** -0.5)
    inp['a_lru_bx'] = nrm((NA, LRU_WIDTH), 0.02)
    s = unif((NA, LRU_WIDTH), 0.9, 0.999) ** (1.0 / LRU_C)
    inp['a_lru_lambda'] = jnp.log(s) - jnp.log1p(-s)
    inp['a_w_out'] = nrm((NA, OUT_A_WIDTH, D), DN_BETA * OUT_A_WIDTH ** -0.5)
    inp['c_mu'] = unif((NC, 6, D), 0.0, 1.0)
    inp['c_w_r'] = nrm((NC, D, D), D ** -0.5)
    inp['c_w_k'] = nrm((NC, D, D), D ** -0.5)
    inp['c_w_v'] = nrm((NC, D, D), DN_BETA * D ** -0.5)
    inp['c_w0'] = unif((NC, D), -6.0, -1.0)
    inp['c_w1'] = nrm((NC, D, DECAY_LORA), D ** -0.5)
    inp['c_w2'] = nrm((NC, DECAY_LORA, D), 0.1 * DECAY_LORA ** -0.5)
    inp['c_a0'] = nrm((NC, D), 0.1)
    inp['c_a1'] = nrm((NC, D, AAA_LORA), D ** -0.5)
    inp['c_a2'] = nrm((NC, AAA_LORA, D), 0.1 * AAA_LORA ** -0.5)
    inp['c_g1'] = nrm((NC, D, GATE_LORA), D ** -0.5)
    inp['c_g2'] = nrm((NC, GATE_LORA, D), GATE_LORA ** -0.5)
    inp['c_k_k'] = 0.85 + nrm((NC, D), 0.02)
    inp['c_k_a'] = 1.0 + nrm((NC, D), 0.02)
    inp['c_r_k'] = nrm((NC, RWKV_HEADS, RWKV_HEAD), 0.1)
    inp['c_gn_g'] = 1.0 + nrm((NC, D), 0.02)
    inp['c_gn_b'] = nrm((NC, D), 0.02)
    inp['c_w_o'] = nrm((NC, D, D), DN_BETA * D ** -0.5)
    inp['ln1_g'] = 1.0 + nrm((DEPTH, D), 0.02)
    inp['ln1_b'] = nrm((DEPTH, D), 0.02)
    inp['ln2_g'] = 1.0 + nrm((DEPTH, D), 0.02)
    inp['ln2_b'] = nrm((DEPTH, D), 0.02)
    inp['mlp_w1'] = nrm((DEPTH, D, D_FF), DN_BETA * D ** -0.5)
    inp['mlp_w2'] = nrm((DEPTH, D_FF, D), DN_BETA * D_FF ** -0.5)
    return inp


def reference(x_prompt, x_sample, state_mlstm_C, state_mlstm_n, state_mlstm_m, state_lru_conv, state_lru_h,
              state_rwkv_shift, state_rwkv_S, a_w_in, a_b_ig, a_b_fg, a_mlstm_norm, a_conv_w, a_conv_b,
              a_lru_wa, a_lru_ba, a_lru_wx, a_lru_bx, a_lru_lambda, a_w_out, c_mu, c_w_r, c_w_k, c_w_v,
              c_w0, c_w1, c_w2, c_a0, c_a1, c_a2, c_g1, c_g2, c_k_k, c_k_a, c_r_k, c_gn_g, c_gn_b, c_w_o,
              ln1_g, ln1_b, ln2_g, ln2_b, mlp_w1, mlp_w2):
    p = dict(a_w_in=a_w_in, a_b_ig=a_b_ig, a_b_fg=a_b_fg, a_mlstm_norm=a_mlstm_norm, a_conv_w=a_conv_w,
             a_conv_b=a_conv_b, a_lru_wa=a_lru_wa, a_lru_ba=a_lru_ba, a_lru_wx=a_lru_wx, a_lru_bx=a_lru_bx,
             a_lru_lambda=a_lru_lambda, a_w_out=a_w_out, c_mu=c_mu, c_w_r=c_w_r, c_w_k=c_w_k, c_w_v=c_w_v,
             c_w0=c_w0, c_w1=c_w1, c_w2=c_w2, c_a0=c_a0, c_a1=c_a1, c_a2=c_a2, c_g1=c_g1, c_g2=c_g2,
             c_k_k=c_k_k, c_k_a=c_k_a, c_r_k=c_r_k, c_gn_g=c_gn_g, c_gn_b=c_gn_b, c_w_o=c_w_o,
             ln1_g=ln1_g, ln1_b=ln1_b, ln2_g=ln2_g, ln2_b=ln2_b, mlp_w1=mlp_w1, mlp_w2=mlp_w2)
    Bp = x_prompt.shape[0]
    init = (jnp.zeros((N_A_LAYERS, Bp, MLSTM_HEADS, MLSTM_DK, MLSTM_DV), F32),
            jnp.zeros((N_A_LAYERS, Bp, MLSTM_HEADS, MLSTM_DK), F32),
            jnp.zeros((N_A_LAYERS, Bp, MLSTM_HEADS), F32),
            jnp.zeros((N_A_LAYERS, Bp, CONV_W - 1, LRU_WIDTH), x_prompt.dtype),
            jnp.zeros((N_A_LAYERS, Bp, LRU_WIDTH), F32),
            jnp.zeros((N_C_LAYERS, Bp, D_MODEL), x_prompt.dtype),
            jnp.zeros((N_C_LAYERS, Bp, RWKV_HEADS, RWKV_HEAD, RWKV_HEAD), F32))
    y_prompt, ps = trunk(x_prompt, init, p, True)
    y_sample, ss = trunk(x_sample, (state_mlstm_C, state_mlstm_n, state_mlstm_m, state_lru_conv, state_lru_h,
                                    state_rwkv_shift, state_rwkv_S), p, False)
    return (y_prompt, y_sample, ps[0], ps[1], ps[2], ps[3], ps[4], ps[5], ps[6],
            ss[0], ss[1], ss[2], ss[3], ss[4], ss[5], ss[6])
```

```python
import functools

import jax
import jax.numpy as jnp
from jax import lax
from jax.experimental import pallas as pl
from jax.experimental.pallas import tpu as pltpu

F32 = jnp.float32
BF16 = jnp.bfloat16

LANES = 128
SUBLANES = 8
VMEM_LIMIT_BYTES = 56 * 1024 * 1024

CHUNK = 64
LRU_C = 8.0
LN_EPS = 1e-5
RWKV_GN_EPS = 64e-5
HIGHEST = lax.Precision.HIGHEST


def _params(*sem):
    return pltpu.CompilerParams(dimension_semantics=sem, vmem_limit_bytes=VMEM_LIMIT_BYTES)


def _tile(n, pref):
    t = min(n, pref)
    while n % t:
        t -= 1
    return t


def _sigmoid(x):
    return 1.0 / (1.0 + jnp.exp(-x))


def _softplus(x):
    return jnp.maximum(x, 0.0) + jnp.log1p(jnp.exp(-jnp.abs(x)))


def _layer_norm(y, g, b):
    mu = jnp.mean(y, -1, keepdims=True)
    d = y - mu
    var = jnp.mean(d * d, -1, keepdims=True)
    return d * lax.rsqrt(var + LN_EPS) * g + b


def _mm_kernel(a_ref, w_ref, o_ref, ab_ref):
    @pl.when(pl.program_id(1) == 0)
    def _():
        ab_ref[...] = a_ref[...].astype(BF16)

    o_ref[...] = jnp.dot(ab_ref[...], w_ref[...], preferred_element_type=F32)


def matmul(a, w, tm=1024, tn=1024):
    M, K = a.shape
    N = w.shape[1]
    tm, tn = _tile(M, tm), _tile(N, tn)
    return pl.pallas_call(
        _mm_kernel,
        grid=(M // tm, N // tn),
        in_specs=[pl.BlockSpec((tm, K), lambda i, j: (i, 0)),
                  pl.BlockSpec((K, tn), lambda i, j: (0, j))],
        out_specs=pl.BlockSpec((tm, tn), lambda i, j: (i, j)),
        out_shape=jax.ShapeDtypeStruct((M, N), F32),
        scratch_shapes=[pltpu.VMEM((tm, K), BF16)],
        compiler_params=_params("parallel", "arbitrary"),
        name="matmul",
    )(a, w)


def _proj_ln_kernel(*refs, nk1, nk, alpha, gated):
    if gated:
        a1_ref, a2_ref, gate_ref, w_ref, x_ref, g_ref, b_ref, o_ref, acc_ref = refs
    else:
        a1_ref, a2_ref, w_ref, x_ref, g_ref, b_ref, o_ref, acc_ref = refs
        gate_ref = None
    k = pl.program_id(1)

    @pl.when(k == 0)
    def _():
        acc_ref[...] = jnp.zeros_like(acc_ref)

    def contrib(a_ref):
        a = a_ref[...]
        if gated:
            a = a * gate_ref[...]
        acc_ref[...] += jnp.dot(a.astype(BF16), w_ref[...], preferred_element_type=F32)

    @pl.when(k < nk1)
    def _():
        contrib(a1_ref)

    @pl.when(k >= nk1)
    def _():
        contrib(a2_ref)

    @pl.when(k == nk - 1)
    def _():
        y = alpha * x_ref[...] + acc_ref[...]
        o_ref[...] = _layer_norm(y, g_ref[...], b_ref[...])


def proj_ln(a1, a2, gate, w, x, g, b, alpha, tm=512, tk=1024):
    M, K1 = a1.shape
    D = w.shape[1]
    tm, tk = _tile(M, tm), _tile(K1, tk)
    nk1 = K1 // tk
    if a2 is None:
        a2, nk = a1, nk1
    else:
        assert a2.shape == a1.shape
        nk = 2 * nk1
    gated = gate is not None
    a1_spec = pl.BlockSpec((tm, tk), lambda i, k: (i, jnp.minimum(k, nk1 - 1)))
    a2_spec = pl.BlockSpec((tm, tk), lambda i, k: (i, jnp.maximum(k - nk1, 0)))
    row_spec = pl.BlockSpec((tm, D), lambda i, k: (i, 0))
    vec_spec = pl.BlockSpec((1, D), lambda i, k: (0, 0))
    in_specs = [a1_spec, a2_spec]
    args = [a1, a2]
    if gated:
        in_specs.append(a1_spec)
        args.append(gate)
    in_specs += [pl.BlockSpec((tk, D), lambda i, k: (k, 0)), row_spec, vec_spec, vec_spec]
    args += [w, x, g.reshape(1, D), b.reshape(1, D)]
    return pl.pallas_call(
        functools.partial(_proj_ln_kernel, nk1=nk1, nk=nk, alpha=alpha, gated=gated),
        grid=(M // tm, nk),
        in_specs=in_specs,
        out_specs=row_spec,
        out_shape=jax.ShapeDtypeStruct((M, D), F32),
        scratch_shapes=[pltpu.VMEM((tm, D), F32)],
        compiler_params=_params("parallel", "arbitrary"),
        name="proj_ln",
    )(*args)


def _mlp_ln_kernel(x_ref, w1_ref, w2_ref, g_ref, b_ref, o_ref, acc_ref, xb_ref, *, nf, alpha):
    f = pl.program_id(1)

    @pl.when(f == 0)
    def _():
        acc_ref[...] = jnp.zeros_like(acc_ref)
        xb_ref[...] = x_ref[...].astype(BF16)

    h = jnp.maximum(jnp.dot(xb_ref[...], w1_ref[...], preferred_element_type=F32), 0.0)
    acc_ref[...] += jnp.dot((h * h).astype(BF16), w2_ref[...], preferred_element_type=F32)

    @pl.when(f == nf - 1)
    def _():
        y = alpha * x_ref[...] + acc_ref[...]
        o_ref[...] = _layer_norm(y, g_ref[...], b_ref[...])


def mlp_ln(x, w1, w2, g, b, alpha, tm=512, tf=512):
    M, D = x.shape
    FF = w1.shape[1]
    tm, tf = _tile(M, tm), _tile(FF, tf)
    nf = FF // tf
    row_spec = pl.BlockSpec((tm, D), lambda i, f: (i, 0))
    vec_spec = pl.BlockSpec((1, D), lambda i, f: (0, 0))
    return pl.pallas_call(
        functools.partial(_mlp_ln_kernel, nf=nf, alpha=alpha),
        grid=(M // tm, nf),
        in_specs=[row_spec,
                  pl.BlockSpec((D, tf), lambda i, f: (0, f)),
                  pl.BlockSpec((tf, D), lambda i, f: (f, 0)),
                  vec_spec, vec_spec],
        out_specs=row_spec,
        out_shape=jax.ShapeDtypeStruct((M, D), F32),
        scratch_shapes=[pltpu.VMEM((tm, D), F32), pltpu.VMEM((tm, D), BF16)],
        compiler_params=_params("parallel", "arbitrary"),
        name="mlp_ln",
    )(x, w1, w2, g.reshape(1, D), b.reshape(1, D))


def _mlstm_kernel(q_ref, k_ref, v_ref, o_ref, gate_ref, gbias_ref, nw_ref, c0_ref, n0_ref, m0_ref,
                  h_ref, c1_ref, n1_ref, m1_ref, *, heads, dk, dv, L, n_chunks, nt):
    t = pl.program_id(1)

    @pl.when(t == 0)
    def _():
        c1_ref[...] = c0_ref[...]
        n1_ref[...] = n0_ref[...]
        m1_ref[...] = m0_ref[...]

    row = lax.broadcasted_iota(jnp.int32, (L, L), 0)
    col = lax.broadcasted_iota(jnp.int32, (L, L), 1)
    causal = row >= col
    tril = causal.astype(F32)
    sel_r = lax.broadcasted_iota(jnp.int32, (2 * heads, LANES), 0)
    sel_c = lax.broadcasted_iota(jnp.int32, (2 * heads, LANES), 1)
    pick = (sel_r == sel_c).astype(F32)
    lane = lax.broadcasted_iota(jnp.int32, (L, LANES), 1)
    scale = dk ** -0.5

    for c in range(n_chunks):
        r0 = c * L
        gt = gate_ref[0, r0:r0 + L, :] + gbias_ref[...]
        logf = jnp.minimum(gt, 0.0) - jnp.log1p(jnp.exp(-jnp.abs(gt)))
        gl = jnp.where(lane < heads, gt, logf)
        cum = jnp.dot(tril, gl, preferred_element_type=F32, precision=HIGHEST)
        nt_dims = (((1,), (1,)), ((), ()))
        gl_t = lax.dot_general(pick, gl, nt_dims, preferred_element_type=F32, precision=HIGHEST)
        cum_t = lax.dot_general(pick, cum, nt_dims, preferred_element_type=F32, precision=HIGHEST)
        for h in range(heads):
            ig_col = gl[:, h:h + 1]
            b_col = cum[:, heads + h:heads + h + 1]
            ig_row = gl_t[h:h + 1, :]
            b_row = cum_t[heads + h:heads + h + 1, :]
            b_last = b_row[:, L - 1:L]
            m_prev = m1_ref[0, :, h:h + 1]
            qh = (q_ref[0, r0:r0 + L, h * dk:(h + 1) * dk] * scale).astype(BF16)
            kf = k_ref[0, r0:r0 + L, h * dk:(h + 1) * dk]
            kh = kf.astype(BF16)
            vh = v_ref[0, r0:r0 + L, h * dv:(h + 1) * dv].astype(BF16)
            c_prev = c1_ref[0, h]
            n_prev = n1_ref[0, h:h + 1, :]

            dmat = jnp.where(causal, b_col - b_row + ig_row, -jnp.inf)
            inter = b_col + m_prev
            m_t = jnp.maximum(inter, jnp.max(dmat, -1, keepdims=True))
            w_intra = jnp.exp(dmat - m_t)
            w_inter = jnp.exp(inter - m_t)
            s = lax.dot_general(qh, kh, nt_dims, preferred_element_type=F32)
            qk = s * w_intra
            num = jnp.dot(qk.astype(BF16), vh, preferred_element_type=F32)
            num = num + w_inter * jnp.dot(qh, c_prev.astype(BF16), preferred_element_type=F32)
            qn = jnp.sum(qh.astype(F32) * n_prev, -1, keepdims=True)
            den = jnp.sum(qk, -1, keepdims=True) + w_inter * qn
            hh = num / jnp.maximum(jnp.abs(den), jnp.exp(-m_t))

            mu = jnp.mean(hh, -1, keepdims=True)
            d = hh - mu
            var = jnp.mean(d * d, -1, keepdims=True)
            hn = d * lax.rsqrt(var + LN_EPS)
            og = o_ref[0, r0:r0 + L, h * dv:(h + 1) * dv]
            h_ref[0, r0:r0 + L, h * dv:(h + 1) * dv] = hn * nw_ref[:, h * dv:(h + 1) * dv] * _sigmoid(og)

            lw_col = b_last - b_col + ig_col
            lw_row = b_last - b_row + ig_row
            m_new = jnp.maximum(b_last + m_prev, jnp.max(lw_row, -1, keepdims=True))
            ws_col = jnp.exp(lw_col - m_new)
            wc = jnp.exp(b_last + m_prev - m_new)
            kw = kf * ws_col
            tn_dims = (((0,), (0,)), ((), ()))
            c1_ref[0, h] = wc * c_prev + lax.dot_general(kw.astype(BF16), vh, tn_dims,
                                                         preferred_element_type=F32)
            n1_ref[0, h:h + 1, :] = wc * n_prev + jnp.sum(kw, 0, keepdims=True)
            m1_ref[0, :, h:h + 1] = m_new


def mlstm(u_qkvo, gates, gbias, norm_w, c0, n0, m0, heads, dk, dv):
    B, T, _ = u_qkvo.shape
    L = min(T, CHUNK)
    tb = _tile(T, 4 * L)
    n_chunks = tb // L
    nt = T // tb
    qw, vw = heads * dk, heads * dv
    assert qw % LANES == 0 and vw == 2 * qw
    st = lambda i, t: (i, 0, 0)
    kern = functools.partial(_mlstm_kernel, heads=heads, dk=dk, dv=dv, L=L, n_chunks=n_chunks, nt=nt)
    return pl.pallas_call(
        kern,
        grid=(B, nt),
        in_specs=[pl.BlockSpec((1, tb, qw), lambda i, t: (i, t, 0)),
                  pl.BlockSpec((1, tb, qw), lambda i, t: (i, t, 1)),
                  pl.BlockSpec((1, tb, vw), lambda i, t: (i, t, 1)),
                  pl.BlockSpec((1, tb, vw), lambda i, t: (i, t, 2)),
                  pl.BlockSpec((1, tb, LANES), lambda i, t: (i, t, 0)),
                  pl.BlockSpec((1, LANES), lambda i, t: (0, 0)),
                  pl.BlockSpec((1, vw), lambda i, t: (0, 0)),
                  pl.BlockSpec((1, heads, dk, dv), lambda i, t: (i, 0, 0, 0)),
                  pl.BlockSpec((1, heads, dk), st),
                  pl.BlockSpec((1, 1, heads), st)],
        out_specs=[pl.BlockSpec((1, tb, vw), lambda i, t: (i, t, 0)),
                   pl.BlockSpec((1, heads, dk, dv), lambda i, t: (i, 0, 0, 0)),
                   pl.BlockSpec((1, heads, dk), st),
                   pl.BlockSpec((1, 1, heads), st)],
        out_shape=[jax.ShapeDtypeStruct((B, T, vw), F32),
                   jax.ShapeDtypeStruct((B, heads, dk, dv), F32),
                   jax.ShapeDtypeStruct((B, heads, dk), F32),
                   jax.ShapeDtypeStruct((B, 1, heads), F32)],
        compiler_params=_params("parallel", "arbitrary"),
        name="mlstm",
    )(u_qkvo, u_qkvo, u_qkvo, u_qkvo, gates, gbias, norm_w, c0, n0, m0)


def _gelu_tanh(x):
    return 0.5 * x * (1.0 + jnp.tanh(0.7978845608028654 * (x + 0.044715 * x * x * x)))


def _lru_kernel(xr_ref, yg_ref, conv0_ref, h0_ref, cw_ref, cb_ref, wa_ref, ba_ref, wx_ref, bx_ref, lam_ref,
                y_ref, conv1_ref, h1_ref, xp_ref, a_ref, b_ref, *, tb, blocks, cw, reset_first):
    t = pl.program_id(1)
    halo = cw - 1
    base = SUBLANES

    @pl.when(t == 0)
    def _():
        xp_ref[base - halo:base, :] = conv0_ref[0]
        h1_ref[0] = h0_ref[0]

    xp_ref[base:base + tb, :] = xr_ref[0]
    xc = cb_ref[...] + xp_ref[base:base + tb, :] * cw_ref[cw - 1:cw, :]
    for j in range(cw - 1):
        xc = xc + xp_ref[base - halo + j:base - halo + j + tb, :] * cw_ref[j:j + 1, :]

    sp = _softplus(-lam_ref[...])
    bw = xc.shape[1] // blocks
    for g in range(blocks):
        sl = slice(g * bw, (g + 1) * bw)
        xg = xc[:, sl]
        xgb = xg.astype(BF16)
        gr = _sigmoid(jnp.dot(xgb, wa_ref[g], preferred_element_type=F32) + ba_ref[:, sl])
        gi = _sigmoid(jnp.dot(xgb, wx_ref[g], preferred_element_type=F32) + bx_ref[:, sl])
        log_a = -LRU_C * gr * sp[:, sl]
        th = jnp.tanh(log_a)
        mult = jnp.sqrt(-2.0 * th / (1.0 - th))
        if reset_first:
            first = (lax.broadcasted_iota(jnp.int32, mult.shape, 0) == 0) & (t == 0)
            mult = jnp.where(first, 1.0, mult)
        a_ref[:, sl] = jnp.exp(log_a)
        b_ref[:, sl] = mult * gi * xg

    def step(i, h):
        h = a_ref[pl.ds(i, 1), :] * h + b_ref[pl.ds(i, 1), :]
        b_ref[pl.ds(i, 1), :] = h
        return h

    h_last = lax.fori_loop(0, tb, step, h1_ref[0], unroll=8)
    h1_ref[0] = h_last
    y_ref[0] = b_ref[...] * _gelu_tanh(yg_ref[0])
    tail = xp_ref[base + tb - halo:base + tb, :]
    xp_ref[base - halo:base, :] = tail
    conv1_ref[0] = tail


def lru(u_xy, conv0, h0, conv_w, conv_b, wa, ba, wx, bx, lam, reset_first):
    B, T, W2 = u_xy.shape
    W = W2 // 2
    cw = conv_w.shape[0]
    blocks = wa.shape[0]
    tb = _tile(T, 256)
    assert tb >= cw - 1 and cw - 1 <= SUBLANES
    st = lambda i, t: (i, 0, 0)
    vec = pl.BlockSpec((1, W), lambda i, t: (0, 0))
    wsp = pl.BlockSpec(wa.shape, lambda i, t: (0, 0, 0))
    kern = functools.partial(_lru_kernel, tb=tb, blocks=blocks, cw=cw, reset_first=reset_first)
    return pl.pallas_call(
        kern,
        grid=(B, T // tb),
        in_specs=[pl.BlockSpec((1, tb, W), lambda i, t: (i, t, 0)),
                  pl.BlockSpec((1, tb, W), lambda i, t: (i, t, 1)),
                  pl.BlockSpec((1, cw - 1, W), st),
                  pl.BlockSpec((1, 1, W), st),
                  pl.BlockSpec((cw, W), lambda i, t: (0, 0)),
                  vec, wsp, vec, wsp, vec, vec],
        out_specs=[pl.BlockSpec((1, tb, W), lambda i, t: (i, t, 0)),
                   pl.BlockSpec((1, cw - 1, W), st),
                   pl.BlockSpec((1, 1, W), st)],
        out_shape=[jax.ShapeDtypeStruct((B, T, W), F32),
                   jax.ShapeDtypeStruct((B, cw - 1, W), F32),
                   jax.ShapeDtypeStruct((B, 1, W), F32)],
        scratch_shapes=[pltpu.VMEM((tb + SUBLANES, W), F32),
                        pltpu.VMEM((tb, W), F32),
                        pltpu.VMEM((tb, W), F32)],
        compiler_params=_params("parallel", "arbitrary"),
        name="lru",
    )(u_xy, u_xy, conv0, h0, conv_w, conv_b, wa, ba, wx, bx, lam)


def _mix_kernel(x_ref, shift_ref, mu_ref, o_ref, xp_ref, *, tb, n_mix):
    t = pl.program_id(1)
    base = SUBLANES

    @pl.when(t == 0)
    def _():
        xp_ref[base - 1:base, :] = shift_ref[0]

    x = x_ref[0]
    xp_ref[base:base + tb, :] = x
    xx = xp_ref[base - 1:base - 1 + tb, :] - x
    for j in range(n_mix):
        o_ref[j, 0] = (x + xx * mu_ref[j:j + 1, :]).astype(BF16)
    xp_ref[base - 1:base, :] = xp_ref[base + tb - 1:base + tb, :]


def rwkv_mix(x, shift, mu):
    B, T, D = x.shape
    n_mix = mu.shape[0]
    tb = _tile(T, 256)
    return pl.pallas_call(
        functools.partial(_mix_kernel, tb=tb, n_mix=n_mix),
        grid=(B, T // tb),
        in_specs=[pl.BlockSpec((1, tb, D), lambda i, t: (i, t, 0)),
                  pl.BlockSpec((1, 1, D), lambda i, t: (i, 0, 0)),
                  pl.BlockSpec((n_mix, D), lambda i, t: (0, 0))],
        out_specs=pl.BlockSpec((n_mix, 1, tb, D), lambda i, t: (0, i, t, 0)),
        out_shape=jax.ShapeDtypeStruct((n_mix, B, T, D), BF16),
        scratch_shapes=[pltpu.VMEM((tb + SUBLANES, D), F32)],
        compiler_params=_params("parallel", "arbitrary"),
        name="rwkv_mix",
    )(x, shift, mu)


def _lora_kernel(x_ref, w1_ref, w2_ref, bias_ref, o_ref, *, mid, post):
    z = jnp.dot(x_ref[...], w1_ref[...], preferred_element_type=F32)
    if mid == "tanh":
        z = jnp.tanh(z)
    elif mid == "sigmoid":
        z = _sigmoid(z)
    y = jnp.dot(z.astype(BF16), w2_ref[...], preferred_element_type=F32)
    if post == "decay":
        w_log = -_softplus(-(bias_ref[...] + y)) - 0.5
        y = jnp.exp(-jnp.exp(w_log))
    elif post == "sigmoid":
        y = _sigmoid(bias_ref[...] + y)
    o_ref[...] = y


def lora(x, w1, w2, bias, mid, post, tm=512):
    M, D = x.shape
    R = w1.shape[1]
    N = w2.shape[1]
    tm = _tile(M, tm)
    return pl.pallas_call(
        functools.partial(_lora_kernel, mid=mid, post=post),
        grid=(M // tm,),
        in_specs=[pl.BlockSpec((tm, D), lambda i: (i, 0)),
                  pl.BlockSpec((D, R), lambda i: (0, 0)),
                  pl.BlockSpec((R, N), lambda i: (0, 0)),
                  pl.BlockSpec((1, N), lambda i: (0, 0))],
        out_specs=pl.BlockSpec((tm, N), lambda i: (i, 0)),
        out_shape=jax.ShapeDtypeStruct((M, N), F32),
        compiler_params=_params("parallel"),
        name="lora",
    )(x, w1, w2, bias)


def _rwkv_scan_kernel(r_ref, k_ref, v_ref, w_ref, a_ref, kk_p, ka_p, rk_p, gg_p, gb_p, s0_ref,
                      y_ref, s1_ref, nkk_s, kka_s, km_s, *, tb, n):
    t = pl.program_id(1)

    @pl.when(t == 0)
    def _():
        s1_ref[...] = s0_ref[...]

    def time_step(i, carry):
        r = r_ref[0, i]
        k = k_ref[0, i]
        a = a_ref[0, i]
        w = w_ref[0, i]
        kk = k * kk_p[...]
        kk = kk * lax.rsqrt(jnp.maximum(jnp.sum(kk * kk, 0, keepdims=True), 1e-24))
        km = k * (1.0 + (a - 1.0) * ka_p[...])
        nkk_s[...] = -kk
        kka_s[...] = kk * a
        km_s[...] = km

        def v_step(j, c):
            s = s1_ref[0, j]
            sa = jnp.sum(s * nkk_s[...], 0, keepdims=True)
            vj = v_ref[0, i, pl.ds(j, 1), :]
            s = s * w + sa * kka_s[...] + vj * km_s[...]
            s1_ref[0, j] = s
            y_ref[0, i, pl.ds(j, 1), :] = jnp.sum(s * r, 0, keepdims=True)
            return c

        lax.fori_loop(0, n, v_step, 0, unroll=4)

        y = y_ref[0, i]
        mu = jnp.mean(y, 0, keepdims=True)
        d = y - mu
        var = jnp.mean(d * d, 0, keepdims=True)
        yn = d * lax.rsqrt(var + RWKV_GN_EPS) * gg_p[...] + gb_p[...]
        bonus = jnp.sum(r * km * rk_p[...], 0, keepdims=True) * v_ref[0, i]
        y_ref[0, i] = yn + bonus
        return carry

    lax.fori_loop(0, tb, time_step, 0)


def rwkv_scan(r, k, v, w, a, kk_p, ka_p, rk_p, gg_p, gb_p, s0):
    G, T, n, _ = r.shape
    tb = _tile(T, 32)
    seq = pl.BlockSpec((1, tb, n, LANES), lambda g, t: (g, t, 0, 0))
    par = pl.BlockSpec((n, LANES), lambda g, t: (0, 0))
    st = pl.BlockSpec((1, n, n, LANES), lambda g, t: (g, 0, 0, 0))
    return pl.pallas_call(
        functools.partial(_rwkv_scan_kernel, tb=tb, n=n),
        grid=(G, T // tb),
        in_specs=[seq] * 5 + [par] * 5 + [st],
        out_specs=[seq, st],
        out_shape=[jax.ShapeDtypeStruct((G, T, n, LANES), F32),
                   jax.ShapeDtypeStruct((G, n, n, LANES), F32)],
        scratch_shapes=[pltpu.VMEM((n, LANES), F32)] * 3,
        compiler_params=_params("parallel", "arbitrary"),
        name="rwkv_scan",
    )(r, k, v, w, a, kk_p, ka_p, rk_p, gg_p, gb_p, s0)


def _pad_cols(w, n):
    return jnp.pad(w, ((0, 0), (0, n - w.shape[1])))


def _pad_rows(w, n):
    return jnp.pad(w, ((0, n - w.shape[0]), (0, 0)))


def _layer_a(x, st, p, li, reset_first, alpha):
    c0, n0, m0, conv0, h0 = st
    B, T, D = x.shape
    heads = p['a_b_ig'].shape[1]
    dv = p['a_mlstm_norm'].shape[1] // heads
    dk = dv // 2
    qw, vw = heads * dk, heads * dv
    W = p['a_conv_w'].shape[2]
    w_in = p['a_w_in'][li]
    n_qkvo = 2 * qw + 2 * vw
    w_qkvo = w_in[:, :n_qkvo].astype(BF16)
    w_gate = _pad_cols(w_in[:, n_qkvo:n_qkvo + 2 * heads], LANES).astype(BF16)
    w_xy = w_in[:, n_qkvo + 2 * heads:].astype(BF16)
    x2 = x.reshape(B * T, D)
    u_qkvo = matmul(x2, w_qkvo).reshape(B, T, n_qkvo)
    gates = matmul(x2, w_gate).reshape(B, T, LANES)
    u_xy = matmul(x2, w_xy).reshape(B, T, 2 * W)
    gbias = _pad_cols(jnp.concatenate([p['a_b_ig'][li], p['a_b_fg'][li]])[None, :], LANES)
    hm, c1, n1, m1 = mlstm(u_qkvo, gates, gbias, p['a_mlstm_norm'][li][None, :],
                           c0, n0, m0.reshape(B, 1, heads), heads, dk, dv)
    yb, conv1, h1 = lru(u_xy, conv0, h0.reshape(B, 1, W), p['a_conv_w'][li], p['a_conv_b'][li][None, :],
                        p['a_lru_wa'][li].astype(BF16), p['a_lru_ba'][li][None, :],
                        p['a_lru_wx'][li].astype(BF16), p['a_lru_bx'][li][None, :],
                        p['a_lru_lambda'][li][None, :], reset_first)
    y = proj_ln(hm.reshape(B * T, vw), yb.reshape(B * T, W), None, p['a_w_out'][li].astype(BF16), x2,
                p['ln1_g'][2 * li], p['ln1_b'][2 * li], alpha)
    return y.reshape(B, T, D), (c1, n1, m1.reshape(B, heads), conv1, h1.reshape(B, W))


def _layer_c(x, shift, s0, p, li, layer, alpha):
    B, T, D = x.shape
    H, N = p['c_r_k'].shape[1:]
    M = B * T
    xm = rwkv_mix(x, shift.reshape(B, 1, D), p['c_mu'][li]).reshape(6, M, D)
    r = matmul(xm[0], p['c_w_r'][li].astype(BF16))
    k = matmul(xm[2], p['c_w_k'][li].astype(BF16))
    v = matmul(xm[3], p['c_w_v'][li].astype(BF16))
    rd = -(-p['c_w1'].shape[2] // LANES) * LANES
    ra = -(-p['c_a1'].shape[2] // LANES) * LANES
    decay = lora(xm[1], _pad_cols(p['c_w1'][li], rd).astype(BF16), _pad_rows(p['c_w2'][li], rd).astype(BF16),
                 p['c_w0'][li][None, :], "tanh", "decay")
    a = lora(xm[4], _pad_cols(p['c_a1'][li], ra).astype(BF16), _pad_rows(p['c_a2'][li], ra).astype(BF16),
             p['c_a0'][li][None, :], "none", "sigmoid")
    g = lora(xm[5], p['c_g1'][li].astype(BF16), p['c_g2'][li].astype(BF16),
             jnp.zeros((1, D), F32), "sigmoid", "none")

    bl = LANES // H
    G = B // bl

    def to_scan(z):
        return z.reshape(G, bl, T, H, N).transpose(0, 2, 4, 1, 3).reshape(G, T, N, LANES)

    def par(z):
        return jnp.tile(z.reshape(H, N).T, (1, bl))

    s0t = s0.reshape(G, bl, H, N, N).transpose(0, 3, 4, 1, 2).reshape(G, N, N, LANES)
    yt, s1t = rwkv_scan(to_scan(r), to_scan(k), to_scan(v), to_scan(decay), to_scan(a),
                        par(p['c_k_k'][li]), par(p['c_k_a'][li]), par(p['c_r_k'][li].reshape(D)),
                        par(p['c_gn_g'][li]), par(p['c_gn_b'][li]), s0t)
    y = yt.reshape(G, T, N, bl, H).transpose(0, 3, 1, 4, 2).reshape(M, D)
    s1 = s1t.reshape(G, N, N, bl, H).transpose(0, 3, 4, 1, 2).reshape(B, H, N, N)
    out = proj_ln(y, None, g, p['c_w_o'][li].astype(BF16), x.reshape(M, D),
                  p['ln1_g'][layer], p['ln1_b'][layer], alpha)
    return out.reshape(B, T, D), (x[:, -1], s1)


def _trunk(x, states, p, reset_first):
    mC, mn, mm, cv, hl, sh, S = states
    depth = p['ln1_g'].shape[0]
    alpha = (2 * depth) ** 0.25
    B, T, D = x.shape
    new_a, new_c = [], []
    for layer in range(depth):
        li = layer // 2
        if layer % 2 == 0:
            x, st = _layer_a(x, (mC[li], mn[li], mm[li], cv[li], hl[li]), p, li, reset_first, alpha)
            new_a.append(st)
        else:
            x, st = _layer_c(x, sh[li], S[li], p, li, layer, alpha)
            new_c.append(st)
        x = mlp_ln(x.reshape(B * T, D), p['mlp_w1'][layer].astype(BF16), p['mlp_w2'][layer].astype(BF16),
                   p['ln2_g'][layer], p['ln2_b'][layer], alpha).reshape(B, T, D)
    sa = [jnp.stack([s[j] for s in new_a]) for j in range(5)]
    sc = [jnp.stack([s[j] for s in new_c]) for j in range(2)]
    return x, sa + sc


def kernel(x_prompt, x_sample, state_mlstm_C, state_mlstm_n, state_mlstm_m, state_lru_conv, state_lru_h,
           state_rwkv_shift, state_rwkv_S, a_w_in, a_b_ig, a_b_fg, a_mlstm_norm, a_conv_w, a_conv_b,
           a_lru_wa, a_lru_ba, a_lru_wx, a_lru_bx, a_lru_lambda, a_w_out, c_mu, c_w_r, c_w_k, c_w_v,
           c_w0, c_w1, c_w2, c_a0, c_a1, c_a2, c_g1, c_g2, c_k_k, c_k_a, c_r_k, c_gn_g, c_gn_b, c_w_o,
           ln1_g, ln1_b, ln2_g, ln2_b, mlp_w1, mlp_w2):
    p = dict(a_w_in=a_w_in, a_b_ig=a_b_ig, a_b_fg=a_b_fg, a_mlstm_norm=a_mlstm_norm, a_conv_w=a_conv_w,
             a_conv_b=a_conv_b, a_lru_wa=a_lru_wa, a_lru_ba=a_lru_ba, a_lru_wx=a_lru_wx, a_lru_bx=a_lru_bx,
             a_lru_lambda=a_lru_lambda, a_w_out=a_w_out, c_mu=c_mu, c_w_r=c_w_r, c_w_k=c_w_k, c_w_v=c_w_v,
             c_w0=c_w0, c_w1=c_w1, c_w2=c_w2, c_a0=c_a0, c_a1=c_a1, c_a2=c_a2, c_g1=c_g1, c_g2=c_g2,
             c_k_k=c_k_k, c_k_a=c_k_a, c_r_k=c_r_k, c_gn_g=c_gn_g, c_gn_b=c_gn_b, c_w_o=c_w_o,
             ln1_g=ln1_g, ln1_b=ln1_b, ln2_g=ln2_g, ln2_b=ln2_b, mlp_w1=mlp_w1, mlp_w2=mlp_w2)
    Bp = x_prompt.shape[0]
    init = tuple(jnp.zeros((s.shape[0], Bp) + s.shape[2:], s.dtype)
                 for s in (state_mlstm_C, state_mlstm_n, state_mlstm_m, state_lru_conv, state_lru_h,
                           state_rwkv_shift, state_rwkv_S))
    y_prompt, ps = _trunk(x_prompt, init, p, True)
    y_sample, ss = _trunk(x_sample, (state_mlstm_C, state_mlstm_n, state_mlstm_m, state_lru_conv,
                                     state_lru_h, state_rwkv_shift, state_rwkv_S), p, False)
    return (y_prompt, y_sample, *ps, *ss)
```

```python
import functools

import jax
import jax.numpy as jnp
from jax import lax
from jax.experimental import pallas as pl
from jax.experimental.pallas import tpu as pltpu

F32 = jnp.float32
BF16 = jnp.bfloat16

LANES = 128
SUBLANES = 8
VMEM_LIMIT_BYTES = 56 * 1024 * 1024

CHUNK = 64
LRU_C = 8.0
LN_EPS = 1e-5
RWKV_GN_EPS = 64e-5
HIGHEST = lax.Precision.HIGHEST


def _params(*sem):
    return pltpu.CompilerParams(dimension_semantics=sem, vmem_limit_bytes=VMEM_LIMIT_BYTES)


def _tile(n, pref):
    t = min(n, pref)
    while n % t:
        t -= 1
    return t


def _sigmoid(x):
    return 1.0 / (1.0 + jnp.exp(-x))


def _softplus(x):
    return jnp.maximum(x, 0.0) + jnp.log1p(jnp.exp(-jnp.abs(x)))


def _layer_norm(y, g, b):
    mu = jnp.mean(y, -1, keepdims=True)
    d = y - mu
    var = jnp.mean(d * d, -1, keepdims=True)
    return d * lax.rsqrt(var + LN_EPS) * g + b


def _mm_kernel(a_ref, w_ref, o_ref, ab_ref):
    @pl.when(pl.program_id(1) == 0)
    def _():
        ab_ref[...] = a_ref[...].astype(BF16)

    o_ref[...] = jnp.dot(ab_ref[...], w_ref[...], preferred_element_type=F32)


def matmul(a, w, tm=1024, tn=1024):
    M, K = a.shape
    N = w.shape[1]
    tm, tn = _tile(M, tm), _tile(N, tn)
    return pl.pallas_call(
        _mm_kernel,
        grid=(M // tm, N // tn),
        in_specs=[pl.BlockSpec((tm, K), lambda i, j: (i, 0)),
                  pl.BlockSpec((K, tn), lambda i, j: (0, j))],
        out_specs=pl.BlockSpec((tm, tn), lambda i, j: (i, j)),
        out_shape=jax.ShapeDtypeStruct((M, N), F32),
        scratch_shapes=[pltpu.VMEM((tm, K), BF16)],
        compiler_params=_params("parallel", "arbitrary"),
        name="matmul",
    )(a, w)


def _proj_ln_kernel(*refs, nk1, nk, alpha, gated):
    if gated:
        a1_ref, a2_ref, gate_ref, w_ref, x_ref, g_ref, b_ref, o_ref, acc_ref = refs
    else:
        a1_ref, a2_ref, w_ref, x_ref, g_ref, b_ref, o_ref, acc_ref = refs
        gate_ref = None
    k = pl.program_id(1)

    @pl.when(k == 0)
    def _():
        acc_ref[...] = jnp.zeros_like(acc_ref)

    def contrib(a_ref):
        a = a_ref[...]
        if gated:
            a = a * gate_ref[...]
        acc_ref[...] += jnp.dot(a.astype(BF16), w_ref[...], preferred_element_type=F32)

    @pl.when(k < nk1)
    def _():
        contrib(a1_ref)

    @pl.when(k >= nk1)
    def _():
        contrib(a2_ref)

    @pl.when(k == nk - 1)
    def _():
        y = alpha * x_ref[...] + acc_ref[...]
        o_ref[...] = _layer_norm(y, g_ref[...], b_ref[...])


def proj_ln(a1, a2, gate, w, x, g, b, alpha, tm=512, tk=1024):
    M, K1 = a1.shape
    D = w.shape[1]
    tm, tk = _tile(M, tm), _tile(K1, tk)
    nk1 = K1 // tk
    if a2 is None:
        a2, nk = a1, nk1
    else:
        assert a2.shape == a1.shape
        nk = 2 * nk1
    gated = gate is not None
    a1_spec = pl.BlockSpec((tm, tk), lambda i, k: (i, jnp.minimum(k, nk1 - 1)))
    a2_spec = pl.BlockSpec((tm, tk), lambda i, k: (i, jnp.maximum(k - nk1, 0)))
    row_spec = pl.BlockSpec((tm, D), lambda i, k: (i, 0))
    vec_spec = pl.BlockSpec((1, D), lambda i, k: (0, 0))
    in_specs = [a1_spec, a2_spec]
    args = [a1, a2]
    if gated:
        in_specs.append(a1_spec)
        args.append(gate)
    in_specs += [pl.BlockSpec((tk, D), lambda i, k: (k, 0)), row_spec, vec_spec, vec_spec]
    args += [w, x, g.reshape(1, D), b.reshape(1, D)]
    return pl.pallas_call(
        functools.partial(_proj_ln_kernel, nk1=nk1, nk=nk, alpha=alpha, gated=gated),
        grid=(M // tm, nk),
        in_specs=in_specs,
        out_specs=row_spec,
        out_shape=jax.ShapeDtypeStruct((M, D), F32),
        scratch_shapes=[pltpu.VMEM((tm, D), F32)],
        compiler_params=_params("parallel", "arbitrary"),
        name="proj_ln",
    )(*args)


def _mlp_ln_kernel(x_ref, w1_ref, w2_ref, g_ref, b_ref, o_ref, acc_ref, xb_ref, *, nf, alpha):
    f = pl.program_id(1)

    @pl.when(f == 0)
    def _():
        acc_ref[...] = jnp.zeros_like(acc_ref)
        xb_ref[...] = x_ref[...].astype(BF16)

    h = jnp.maximum(jnp.dot(xb_ref[...], w1_ref[...], preferred_element_type=F32), 0.0)
    acc_ref[...] += jnp.dot((h * h).astype(BF16), w2_ref[...], preferred_element_type=F32)

    @pl.when(f == nf - 1)
    def _():
        y = alpha * x_ref[...] + acc_ref[...]
        o_ref[...] = _layer_norm(y, g_ref[...], b_ref[...])


def mlp_ln(x, w1, w2, g, b, alpha, tm=512, tf=512):
    M, D = x.shape
    FF = w1.shape[1]
    tm, tf = _tile(M, tm), _tile(FF, tf)
    nf = FF // tf
    row_spec = pl.BlockSpec((tm, D), lambda i, f: (i, 0))
    vec_spec = pl.BlockSpec((1, D), lambda i, f: (0, 0))
    return pl.pallas_call(
        functools.partial(_mlp_ln_kernel, nf=nf, alpha=alpha),
        grid=(M // tm, nf),
        in_specs=[row_spec,
                  pl.BlockSpec((D, tf), lambda i, f: (0, f)),
                  pl.BlockSpec((tf, D), lambda i, f: (f, 0)),
                  vec_spec, vec_spec],
        out_specs=row_spec,
        out_shape=jax.ShapeDtypeStruct((M, D), F32),
        scratch_shapes=[pltpu.VMEM((tm, D), F32), pltpu.VMEM((tm, D), BF16)],
        compiler_params=_params("parallel", "arbitrary"),
        name="mlp_ln",
    )(x, w1, w2, g.reshape(1, D), b.reshape(1, D))


def _mlstm_kernel(q_ref, k_ref, v_ref, o_ref, gate_ref, gbias_ref, nw_ref, c0_ref, n0_ref, m0_ref,
                  h_ref, c1_ref, n1_ref, m1_ref, *, heads, dk, dv, L, n_chunks, nt):
    t = pl.program_id(1)

    @pl.when(t == 0)
    def _():
        c1_ref[...] = c0_ref[...]
        n1_ref[...] = n0_ref[...]
        m1_ref[...] = m0_ref[...]

    row = lax.broadcasted_iota(jnp.int32, (L, L), 0)
    col = lax.broadcasted_iota(jnp.int32, (L, L), 1)
    causal = row >= col
    tril = causal.astype(F32)
    sel_r = lax.broadcasted_iota(jnp.int32, (2 * heads, LANES), 0)
    sel_c = lax.broadcasted_iota(jnp.int32, (2 * heads, LANES), 1)
    pick = (sel_r == sel_c).astype(F32)
    lane = lax.broadcasted_iota(jnp.int32, (L, LANES), 1)
    scale = dk ** -0.5

    for c in range(n_chunks):
        r0 = c * L
        gt = gate_ref[0, r0:r0 + L, :] + gbias_ref[...]
        logf = jnp.minimum(gt, 0.0) - jnp.log1p(jnp.exp(-jnp.abs(gt)))
        gl = jnp.where(lane < heads, gt, logf)
        cum = jnp.dot(tril, gl, preferred_element_type=F32, precision=HIGHEST)
        nt_dims = (((1,), (1,)), ((), ()))
        gl_t = lax.dot_general(pick, gl, nt_dims, preferred_element_type=F32, precision=HIGHEST)
        cum_t = lax.dot_general(pick, cum, nt_dims, preferred_element_type=F32, precision=HIGHEST)
        for h in range(heads):
            ig_col = gl[:, h:h + 1]
            b_col = cum[:, heads + h:heads + h + 1]
            ig_row = gl_t[h:h + 1, :]
            b_row = cum_t[heads + h:heads + h + 1, :]
            b_last = b_row[:, L - 1:L]
            m_prev = m1_ref[0, :, h:h + 1]
            qh = (q_ref[0, r0:r0 + L, h * dk:(h + 1) * dk] * scale).astype(BF16)
            kf = k_ref[0, r0:r0 + L, h * dk:(h + 1) * dk]
            kh = kf.astype(BF16)
            vh = v_ref[0, r0:r0 + L, h * dv:(h + 1) * dv].astype(BF16)
            c_prev = c1_ref[0, h]
            n_prev = n1_ref[0, h:h + 1, :]

            dmat = jnp.where(causal, b_col - b_row + ig_row, -jnp.inf)
            inter = b_col + m_prev
            m_t = jnp.maximum(inter, jnp.max(dmat, -1, keepdims=True))
            w_intra = jnp.exp(dmat - m_t)
            w_inter = jnp.exp(inter - m_t)
            s = lax.dot_general(qh, kh, nt_dims, preferred_element_type=F32)
            qk = s * w_intra
            num = jnp.dot(qk.astype(BF16), vh, preferred_element_type=F32)
            num = num + w_inter * jnp.dot(qh, c_prev.astype(BF16), preferred_element_type=F32)
            qn = jnp.sum(qh.astype(F32) * n_prev, -1, keepdims=True)
            den = jnp.sum(qk, -1, keepdims=True) + w_inter * qn
            hh = num / jnp.maximum(jnp.abs(den), jnp.exp(-m_t))

            mu = jnp.mean(hh, -1, keepdims=True)
            d = hh - mu
            var = jnp.mean(d * d, -1, keepdims=True)
            hn = d * lax.rsqrt(var + LN_EPS)
            og = o_ref[0, r0:r0 + L, h * dv:(h + 1) * dv]
            h_ref[0, r0:r0 + L, h * dv:(h + 1) * dv] = hn * nw_ref[:, h * dv:(h + 1) * dv] * _sigmoid(og)

            lw_col = b_last - b_col + ig_col
            lw_row = b_last - b_row + ig_row
            m_new = jnp.maximum(b_last + m_prev, jnp.max(lw_row, -1, keepdims=True))
            ws_col = jnp.exp(lw_col - m_new)
            wc = jnp.exp(b_last + m_prev - m_new)
            kw = kf * ws_col
            tn_dims = (((0,), (0,)), ((), ()))
            c1_ref[0, h] = wc * c_prev + lax.dot_general(kw.astype(BF16), vh, tn_dims,
                                                         preferred_element_type=F32)
            n1_ref[0, h:h + 1, :] = wc * n_prev + jnp.sum(kw, 0, keepdims=True)
            m1_ref[0, :, h:h + 1] = m_new


def mlstm(u_qkvo, gates, gbias, norm_w, c0, n0, m0, heads, dk, dv):
    B, T, _ = u_qkvo.shape
    L = min(T, CHUNK)
    tb = _tile(T, 4 * L)
    n_chunks = tb // L
    nt = T // tb
    qw, vw = heads * dk, heads * dv
    assert qw % LANES == 0 and vw == 2 * qw
    st = lambda i, t: (i, 0, 0)
    kern = functools.partial(_mlstm_kernel, heads=heads, dk=dk, dv=dv, L=L, n_chunks=n_chunks, nt=nt)
    return pl.pallas_call(
        kern,
        grid=(B, nt),
        in_specs=[pl.BlockSpec((1, tb, qw), lambda i, t: (i, t, 0)),
                  pl.BlockSpec((1, tb, qw), lambda i, t: (i, t, 1)),
                  pl.BlockSpec((1, tb, vw), lambda i, t: (i, t, 1)),
                  pl.BlockSpec((1, tb, vw), lambda i, t: (i, t, 2)),
                  pl.BlockSpec((1, tb, LANES), lambda i, t: (i, t, 0)),
                  pl.BlockSpec((1, LANES), lambda i, t: (0, 0)),
                  pl.BlockSpec((1, vw), lambda i, t: (0, 0)),
                  pl.BlockSpec((1, heads, dk, dv), lambda i, t: (i, 0, 0, 0)),
                  pl.BlockSpec((1, heads, dk), st),
                  pl.BlockSpec((1, 1, heads), st)],
        out_specs=[pl.BlockSpec((1, tb, vw), lambda i, t: (i, t, 0)),
                   pl.BlockSpec((1, heads, dk, dv), lambda i, t: (i, 0, 0, 0)),
                   pl.BlockSpec((1, heads, dk), st),
                   pl.BlockSpec((1, 1, heads), st)],
        out_shape=[jax.ShapeDtypeStruct((B, T, vw), F32),
                   jax.ShapeDtypeStruct((B, heads, dk, dv), F32),
                   jax.ShapeDtypeStruct((B, heads, dk), F32),
                   jax.ShapeDtypeStruct((B, 1, heads), F32)],
        compiler_params=_params("parallel", "arbitrary"),
        name="mlstm",
    )(u_qkvo, u_qkvo, u_qkvo, u_qkvo, gates, gbias, norm_w, c0, n0, m0)


def _gelu_tanh(x):
    return 0.5 * x * (1.0 + jnp.tanh(0.7978845608028654 * (x + 0.044715 * x * x * x)))


def _lru_kernel(xr_ref, yg_ref, conv0_ref, h0_ref, cw_ref, cb_ref, wa_ref, ba_ref, wx_ref, bx_ref, lam_ref,
                y_ref, conv1_ref, h1_ref, xp_ref, a_ref, b_ref, *, tb, blocks, cw, reset_first):
    t = pl.program_id(1)
    halo = cw - 1
    base = SUBLANES

    @pl.when(t == 0)
    def _():
        xp_ref[base - halo:base, :] = conv0_ref[0]
        h1_ref[0] = h0_ref[0]

    xp_ref[base:base + tb, :] = xr_ref[0]
    xc = cb_ref[...] + xp_ref[base:base + tb, :] * cw_ref[cw - 1:cw, :]
    for j in range(cw - 1):
        xc = xc + xp_ref[base - halo + j:base - halo + j + tb, :] * cw_ref[j:j + 1, :]

    sp = _softplus(-lam_ref[...])
    bw = xc.shape[1] // blocks
    for g in range(blocks):
        sl = slice(g * bw, (g + 1) * bw)
        xg = xc[:, sl]
        xgb = xg.astype(BF16)
        gr = _sigmoid(jnp.dot(xgb, wa_ref[g], preferred_element_type=F32) + ba_ref[:, sl])
        gi = _sigmoid(jnp.dot(xgb, wx_ref[g], preferred_element_type=F32) + bx_ref[:, sl])
        log_a = -LRU_C * gr * sp[:, sl]
        th = jnp.tanh(log_a)
        mult = jnp.sqrt(-2.0 * th / (1.0 - th))
        if reset_first:
            first = (lax.broadcasted_iota(jnp.int32, mult.shape, 0) == 0) & (t == 0)
            mult = jnp.where(first, 1.0, mult)
        a_ref[:, sl] = jnp.exp(log_a)
        b_ref[:, sl] = mult * gi * xg

    def step(i, h):
        h = a_ref[pl.ds(i, 1), :] * h + b_ref[pl.ds(i, 1), :]
        b_ref[pl.ds(i, 1), :] = h
        return h

    h_last = lax.fori_loop(0, tb, step, h1_ref[0], unroll=8)
    h1_ref[0] = h_last
    y_ref[0] = b_ref[...] * _gelu_tanh(yg_ref[0])
    tail = xp_ref[base + tb - halo:base + tb, :]
    xp_ref[base - halo:base, :] = tail
    conv1_ref[0] = tail


def lru(u_xy, conv0, h0, conv_w, conv_b, wa, ba, wx, bx, lam, reset_first):
    B, T, W2 = u_xy.shape
    W = W2 // 2
    cw = conv_w.shape[0]
    blocks = wa.shape[0]
    tb = _tile(T, 256)
    assert tb >= cw - 1 and cw - 1 <= SUBLANES
    st = lambda i, t: (i, 0, 0)
    vec = pl.BlockSpec((1, W), lambda i, t: (0, 0))
    wsp = pl.BlockSpec(wa.shape, lambda i, t: (0, 0, 0))
    kern = functools.partial(_lru_kernel, tb=tb, blocks=blocks, cw=cw, reset_first=reset_first)
    return pl.pallas_call(
        kern,
        grid=(B, T // tb),
        in_specs=[pl.BlockSpec((1, tb, W), lambda i, t: (i, t, 0)),
                  pl.BlockSpec((1, tb, W), lambda i, t: (i, t, 1)),
                  pl.BlockSpec((1, cw - 1, W), st),
                  pl.BlockSpec((1, 1, W), st),
                  pl.BlockSpec((cw, W), lambda i, t: (0, 0)),
                  vec, wsp, vec, wsp, vec, vec],
        out_specs=[pl.BlockSpec((1, tb, W), lambda i, t: (i, t, 0)),
                   pl.BlockSpec((1, cw - 1, W), st),
                   pl.BlockSpec((1, 1, W), st)],
        out_shape=[jax.ShapeDtypeStruct((B, T, W), F32),
                   jax.ShapeDtypeStruct((B, cw - 1, W), F32),
                   jax.ShapeDtypeStruct((B, 1, W), F32)],
        scratch_shapes=[pltpu.VMEM((tb + SUBLANES, W), F32),
                        pltpu.VMEM((tb, W), F32),
                        pltpu.VMEM((tb, W), F32)],
        compiler_params=_params("parallel", "arbitrary"),
        name="lru",
    )(u_xy, u_xy, conv0, h0, conv_w, conv_b, wa, ba, wx, bx, lam)


def _mix_kernel(x_ref, shift_ref, mu_ref, o_ref, xp_ref, *, tb, n_mix):
    t = pl.program_id(1)
    base = SUBLANES

    @pl.when(t == 0)
    def _():
        xp_ref[base - 1:base, :] = shift_ref[0]

    x = x_ref[0]
    xp_ref[base:base + tb, :] = x
    xx = xp_ref[base - 1:base - 1 + tb, :] - x
    for j in range(n_mix):
        o_ref[j, 0] = (x + xx * mu_ref[j:j + 1, :]).astype(BF16)
    xp_ref[base - 1:base, :] = xp_ref[base + tb - 1:base + tb, :]


def rwkv_mix(x, shift, mu):
    B, T, D = x.shape
    n_mix = mu.shape[0]
    tb = _tile(T, 256)
    return pl.pallas_call(
        functools.partial(_mix_kernel, tb=tb, n_mix=n_mix),
        grid=(B, T // tb),
        in_specs=[pl.BlockSpec((1, tb, D), lambda i, t: (i, t, 0)),
                  pl.BlockSpec((1, 1, D), lambda i, t: (i, 0, 0)),
                  pl.BlockSpec((n_mix, D), lambda i, t: (0, 0))],
        out_specs=pl.BlockSpec((n_mix, 1, tb, D), lambda i, t: (0, i, t, 0)),
        out_shape=jax.ShapeDtypeStruct((n_mix, B, T, D), BF16),
        scratch_shapes=[pltpu.VMEM((tb + SUBLANES, D), F32)],
        compiler_params=_params("parallel", "arbitrary"),
        name="rwkv_mix",
    )(x, shift, mu)


def _lora_math(x, w1_ref, w2_ref, bias_ref, mid, post):
    z = jnp.dot(x, w1_ref[...], preferred_element_type=F32)
    if mid == "tanh":
        z = jnp.tanh(z)
    elif mid == "sigmoid":
        z = _sigmoid(z)
    y = jnp.dot(z.astype(BF16), w2_ref[...], preferred_element_type=F32)
    if post == "decay":
        w_log = -_softplus(-(bias_ref[...] + y)) - 0.5
        y = jnp.exp(-jnp.exp(w_log))
    elif post == "sigmoid":
        y = _sigmoid(bias_ref[...] + y)
    return y


def _lora_kernel(x_ref, w1_ref, w2_ref, bias_ref, o_ref, *, mid, post):
    o_ref[...] = _lora_math(x_ref[0], w1_ref, w2_ref, bias_ref, mid, post)


def lora(xs, j, w1, w2, bias, mid, post, tm=512):
    _, M, D = xs.shape
    R = w1.shape[1]
    N = w2.shape[1]
    tm = _tile(M, tm)
    return pl.pallas_call(
        functools.partial(_lora_kernel, mid=mid, post=post),
        grid=(M // tm,),
        in_specs=[pl.BlockSpec((1, tm, D), lambda i: (j, i, 0)),
                  pl.BlockSpec((D, R), lambda i: (0, 0)),
                  pl.BlockSpec((R, N), lambda i: (0, 0)),
                  pl.BlockSpec((1, N), lambda i: (0, 0))],
        out_specs=pl.BlockSpec((tm, N), lambda i: (i, 0)),
        out_shape=jax.ShapeDtypeStruct((M, N), F32),
        compiler_params=_params("parallel"),
        name="lora",
    )(xs, w1, w2, bias)


def _rwkv_scan_kernel(r_ref, k_ref, v_ref, w_ref, a_ref, kk_p, ka_p, rk_p, gg_p, gb_p, s0_ref,
                      y_ref, s1_ref, nkk_s, kka_s, km_s, *, tt, n):
    t = pl.program_id(1)
    nb = n // SUBLANES

    @pl.when(t == 0)
    def _():
        s1_ref[...] = s0_ref[...]

    def bc(p_ref):
        return p_ref[...][:, None, :]

    k = k_ref[0]
    a = a_ref[0]
    kk = k * bc(kk_p)
    kk = kk * lax.rsqrt(jnp.maximum(jnp.sum(kk * kk, 0, keepdims=True), 1e-24))
    nkk_s[...] = -kk
    kka_s[...] = kk * a
    km_s[...] = k * (1.0 + (a - 1.0) * bc(ka_p))

    zeros = tuple(jnp.zeros((SUBLANES, LANES), F32) for _ in range(nb))

    def time_step(i, carry):
        row = pl.ds(i, 1)
        vt = [jnp.concatenate([v_ref[0, jb * SUBLANES + u, row, :] for u in range(SUBLANES)], 0)
              for jb in range(nb)]

        def reduce_keys(kx, acc):
            nk = nkk_s[kx, row, :]
            return tuple(acc[jb] + s1_ref[0, jb, kx] * nk for jb in range(nb))

        sa = lax.fori_loop(0, n, reduce_keys, zeros, unroll=8)

        def update_keys(kx, acc):
            wr = w_ref[0, kx, row, :]
            ar = kka_s[kx, row, :]
            mr = km_s[kx, row, :]
            rr = r_ref[0, kx, row, :]
            out = []
            for jb in range(nb):
                s = s1_ref[0, jb, kx] * wr + sa[jb] * ar + vt[jb] * mr
                s1_ref[0, jb, kx] = s
                out.append(acc[jb] + s * rr)
            return tuple(out)

        y = lax.fori_loop(0, n, update_keys, zeros, unroll=8)
        for jb in range(nb):
            for u in range(SUBLANES):
                y_ref[0, jb * SUBLANES + u, row, :] = y[jb][u:u + 1, :]
        return carry

    lax.fori_loop(0, tt, time_step, 0)

    y = y_ref[0]
    mu = jnp.mean(y, 0, keepdims=True)
    d = y - mu
    var = jnp.mean(d * d, 0, keepdims=True)
    yn = d * lax.rsqrt(var + RWKV_GN_EPS) * bc(gg_p) + bc(gb_p)
    bonus = jnp.sum(r_ref[0] * km_s[...] * bc(rk_p), 0, keepdims=True) * v_ref[0]
    y_ref[0] = yn + bonus


def rwkv_scan(r, k, v, w, a, kk_p, ka_p, rk_p, gg_p, gb_p, s0):
    G, n, T, _ = r.shape
    tt = _tile(T, 32)
    seq = pl.BlockSpec((1, n, tt, LANES), lambda g, t: (g, 0, t, 0))
    par = pl.BlockSpec((n, LANES), lambda g, t: (0, 0))
    st = pl.BlockSpec((1, n // SUBLANES, n, SUBLANES, LANES), lambda g, t: (g, 0, 0, 0, 0))
    return pl.pallas_call(
        functools.partial(_rwkv_scan_kernel, tt=tt, n=n),
        grid=(G, T // tt),
        in_specs=[seq] * 5 + [par] * 5 + [st],
        out_specs=[seq, st],
        out_shape=[jax.ShapeDtypeStruct((G, n, T, LANES), F32),
                   jax.ShapeDtypeStruct(s0.shape, F32)],
        scratch_shapes=[pltpu.VMEM((n, tt, LANES), F32)] * 3,
        compiler_params=_params("parallel", "arbitrary"),
        name="rwkv_scan",
    )(r, k, v, w, a, kk_p, ka_p, rk_p, gg_p, gb_p, s0)


def _store_scan_layout(res, z_ref, heads):
    bl = LANES // heads
    for c in range(res.shape[1] // LANES):
        sub = [res[s * LANES:(s + 1) * LANES, c * LANES:(c + 1) * LANES].T for s in range(bl)]
        for ni in range(bl):
            tile = jnp.concatenate([sub[s][ni * heads:(ni + 1) * heads, :] for s in range(bl)], 0)
            z_ref[0, c * bl + ni] = tile.T


def _mm_scan_kernel(a_ref, w_ref, z_ref, *, heads):
    bl, tq, K = a_ref.shape[1:]
    a = a_ref[0].reshape(bl * tq, K)
    _store_scan_layout(jnp.dot(a, w_ref[...], preferred_element_type=F32), z_ref, heads)


def matmul_scan(xm, j, w, heads, tn=1024):
    _, B, T, K = xm.shape
    N = w.shape[1]
    bl = LANES // heads
    tn = _tile(N, tn)
    return pl.pallas_call(
        functools.partial(_mm_scan_kernel, heads=heads),
        grid=(B // bl, T // LANES, N // tn),
        in_specs=[pl.BlockSpec((1, bl, LANES, K), lambda g, t, c: (j, g, t, 0)),
                  pl.BlockSpec((K, tn), lambda g, t, c: (0, c))],
        out_specs=pl.BlockSpec((1, tn // heads, LANES, LANES), lambda g, t, c: (g, c, t, 0)),
        out_shape=jax.ShapeDtypeStruct((B // bl, N // heads, T, LANES), F32),
        compiler_params=_params("parallel", "parallel", "arbitrary"),
        name="matmul_scan",
    )(xm, w)


def _lora_scan_kernel(a_ref, w1_ref, w2_ref, bias_ref, z_ref, *, heads, mid, post):
    bl, tq, K = a_ref.shape[1:]
    a = a_ref[0].reshape(bl * tq, K)
    _store_scan_layout(_lora_math(a, w1_ref, w2_ref, bias_ref, mid, post), z_ref, heads)


def lora_scan(xm, j, w1, w2, bias, mid, post, heads):
    _, B, T, K = xm.shape
    R = w1.shape[1]
    N = w2.shape[1]
    bl = LANES // heads
    return pl.pallas_call(
        functools.partial(_lora_scan_kernel, heads=heads, mid=mid, post=post),
        grid=(B // bl, T // LANES),
        in_specs=[pl.BlockSpec((1, bl, LANES, K), lambda g, t: (j, g, t, 0)),
                  pl.BlockSpec((K, R), lambda g, t: (0, 0)),
                  pl.BlockSpec((R, N), lambda g, t: (0, 0)),
                  pl.BlockSpec((1, N), lambda g, t: (0, 0))],
        out_specs=pl.BlockSpec((1, N // heads, LANES, LANES), lambda g, t: (g, 0, t, 0)),
        out_shape=jax.ShapeDtypeStruct((B // bl, N // heads, T, LANES), F32),
        compiler_params=_params("parallel", "parallel"),
        name="lora_scan",
    )(xm, w1, w2, bias)


def _from_scan_kernel(y_ref, g_ref, o_ref, *, heads):
    bl = LANES // heads
    for c in range(o_ref.shape[2] // LANES):
        sub = [y_ref[0, c * bl + ni].T for ni in range(bl)]
        for s in range(bl):
            tile = jnp.concatenate([sub[ni][s * heads:(s + 1) * heads, :] for ni in range(bl)], 0)
            cols = slice(c * LANES, (c + 1) * LANES)
            o_ref[s, :, cols] = (tile.T * g_ref[s, :, cols]).astype(BF16)


def from_scan_gated(y, gate, heads):
    G, n, T, _ = y.shape
    B, _, D = gate.shape
    bl = LANES // heads
    tok = pl.BlockSpec((bl, LANES, D), lambda g, t: (g, t, 0))
    return pl.pallas_call(
        functools.partial(_from_scan_kernel, heads=heads),
        grid=(G, T // LANES),
        in_specs=[pl.BlockSpec((1, n, LANES, LANES), lambda g, t: (g, 0, t, 0)), tok],
        out_specs=tok,
        out_shape=jax.ShapeDtypeStruct((B, T, D), BF16),
        compiler_params=_params("parallel", "parallel"),
        name="from_scan_gated",
    )(y, gate)


def _pad_cols(w, n):
    return jnp.pad(w, ((0, 0), (0, n - w.shape[1])))


def _pad_rows(w, n):
    return jnp.pad(w, ((0, n - w.shape[0]), (0, 0)))


def _layer_a(x, st, p, li, reset_first, alpha):
    c0, n0, m0, conv0, h0 = st
    B, T, D = x.shape
    heads = p['a_b_ig'].shape[1]
    dv = p['a_mlstm_norm'].shape[1] // heads
    dk = dv // 2
    qw, vw = heads * dk, heads * dv
    W = p['a_conv_w'].shape[2]
    w_in = p['a_w_in'][li]
    n_qkvo = 2 * qw + 2 * vw
    w_qkvo = w_in[:, :n_qkvo].astype(BF16)
    w_gate = _pad_cols(w_in[:, n_qkvo:n_qkvo + 2 * heads], LANES).astype(BF16)
    w_xy = w_in[:, n_qkvo + 2 * heads:].astype(BF16)
    x2 = x.reshape(B * T, D)
    u_qkvo = matmul(x2, w_qkvo).reshape(B, T, n_qkvo)
    gates = matmul(x2, w_gate).reshape(B, T, LANES)
    u_xy = matmul(x2, w_xy).reshape(B, T, 2 * W)
    gbias = _pad_cols(jnp.concatenate([p['a_b_ig'][li], p['a_b_fg'][li]])[None, :], LANES)
    hm, c1, n1, m1 = mlstm(u_qkvo, gates, gbias, p['a_mlstm_norm'][li][None, :],
                           c0, n0, m0.reshape(B, 1, heads), heads, dk, dv)
    yb, conv1, h1 = lru(u_xy, conv0, h0.reshape(B, 1, W), p['a_conv_w'][li], p['a_conv_b'][li][None, :],
                        p['a_lru_wa'][li].astype(BF16), p['a_lru_ba'][li][None, :],
                        p['a_lru_wx'][li].astype(BF16), p['a_lru_bx'][li][None, :],
                        p['a_lru_lambda'][li][None, :], reset_first)
    y = proj_ln(hm.reshape(B * T, vw), yb.reshape(B * T, W), None, p['a_w_out'][li].astype(BF16), x2,
                p['ln1_g'][2 * li], p['ln1_b'][2 * li], alpha)
    return y.reshape(B, T, D), (c1, n1, m1.reshape(B, heads), conv1, h1.reshape(B, W))


def _layer_c(x, shift, s0, p, li, layer, alpha):
    B, T, D = x.shape
    H, N = p['c_r_k'].shape[1:]
    M = B * T
    bl = LANES // H
    G = B // bl
    nb = N // SUBLANES
    perm = jnp.arange(D).reshape(H, N).T.reshape(D)
    xm = rwkv_mix(x, shift.reshape(B, 1, D), p['c_mu'][li])
    w_r = p['c_w_r'][li][:, perm].astype(BF16)
    w_k = p['c_w_k'][li][:, perm].astype(BF16)
    w_v = p['c_w_v'][li][:, perm].astype(BF16)
    rd = -(-p['c_w1'].shape[2] // LANES) * LANES
    ra = -(-p['c_a1'].shape[2] // LANES) * LANES
    d_args = (_pad_cols(p['c_w1'][li], rd).astype(BF16), _pad_rows(p['c_w2'][li][:, perm], rd).astype(BF16),
              p['c_w0'][li][perm][None, :], "tanh", "decay")
    a_args = (_pad_cols(p['c_a1'][li], ra).astype(BF16), _pad_rows(p['c_a2'][li][:, perm], ra).astype(BF16),
              p['c_a0'][li][perm][None, :], "none", "sigmoid")
    xm2 = xm.reshape(6, M, D)
    g = lora(xm2, 5, p['c_g1'][li].astype(BF16), p['c_g2'][li][:, perm].astype(BF16),
             jnp.zeros((1, D), F32), "sigmoid", "none")
    fused = T % LANES == 0
    if fused:
        r = matmul_scan(xm, 0, w_r, H)
        k = matmul_scan(xm, 2, w_k, H)
        v = matmul_scan(xm, 3, w_v, H)
        decay = lora_scan(xm, 1, *d_args, H)
        a = lora_scan(xm, 4, *a_args, H)
    else:
        def to_scan(z):
            return z.reshape(G, bl, T, N, H).transpose(0, 3, 2, 1, 4).reshape(G, N, T, LANES)

        r = to_scan(matmul(xm2[0], w_r))
        k = to_scan(matmul(xm2[2], w_k))
        v = to_scan(matmul(xm2[3], w_v))
        decay = to_scan(lora(xm2, 1, *d_args))
        a = to_scan(lora(xm2, 4, *a_args))

    def par(z):
        return jnp.tile(z.reshape(H, N).T, (1, bl))

    s0t = s0.reshape(G, bl, H, nb, SUBLANES, N).transpose(0, 3, 5, 4, 1, 2).reshape(G, nb, N, SUBLANES, LANES)
    yt, s1t = rwkv_scan(r, k, v, decay, a,
                        par(p['c_k_k'][li]), par(p['c_k_a'][li]), par(p['c_r_k'][li].reshape(D)),
                        par(p['c_gn_g'][li]), par(p['c_gn_b'][li]), s0t)
    s1 = s1t.reshape(G, nb, N, SUBLANES, bl, H).transpose(0, 4, 5, 1, 3, 2).reshape(B, H, N, N)
    w_o = p['c_w_o'][li][perm, :].astype(BF16)
    if fused:
        y = from_scan_gated(yt, g.reshape(B, T, D), H).reshape(M, D)
        gate = None
    else:
        y = yt.reshape(G, N, T, bl, H).transpose(0, 3, 2, 1, 4).reshape(M, D)
        gate = g
    out = proj_ln(y, None, gate, w_o, x.reshape(M, D), p['ln1_g'][layer], p['ln1_b'][layer], alpha)
    return out.reshape(B, T, D), (x[:, -1], s1)


def _trunk(x, states, p, reset_first):
    mC, mn, mm, cv, hl, sh, S = states
    depth = p['ln1_g'].shape[0]
    alpha = (2 * depth) ** 0.25
    B, T, D = x.shape
    new_a, new_c = [], []
    for layer in range(depth):
        li = layer // 2
        if layer % 2 == 0:
            x, st = _layer_a(x, (mC[li], mn[li], mm[li], cv[li], hl[li]), p, li, reset_first, alpha)
            new_a.append(st)
        else:
            x, st = _layer_c(x, sh[li], S[li], p, li, layer, alpha)
            new_c.append(st)
        x = mlp_ln(x.reshape(B * T, D), p['mlp_w1'][layer].astype(BF16), p['mlp_w2'][layer].astype(BF16),
                   p['ln2_g'][layer], p['ln2_b'][layer], alpha).reshape(B, T, D)
    sa = [jnp.stack([s[j] for s in new_a]) for j in range(5)]
    sc = [jnp.stack([s[j] for s in new_c]) for j in range(2)]
    return x, sa + sc


def kernel(x_prompt, x_sample, state_mlstm_C, state_mlstm_n, state_mlstm_m, state_lru_conv, state_lru_h,
           state_rwkv_shift, state_rwkv_S, a_w_in, a_b_ig, a_b_fg, a_mlstm_norm, a_conv_w, a_conv_b,
           a_lru_wa, a_lru_ba, a_lru_wx, a_lru_bx, a_lru_lambda, a_w_out, c_mu, c_w_r, c_w_k, c_w_v,
           c_w0, c_w1, c_w2, c_a0, c_a1, c_a2, c_g1, c_g2, c_k_k, c_k_a, c_r_k, c_gn_g, c_gn_b, c_w_o,
           ln1_g, ln1_b, ln2_g, ln2_b, mlp_w1, mlp_w2):
    p = dict(a_w_in=a_w_in, a_b_ig=a_b_ig, a_b_fg=a_b_fg, a_mlstm_norm=a_mlstm_norm, a_conv_w=a_conv_w,
             a_conv_b=a_conv_b, a_lru_wa=a_lru_wa, a_lru_ba=a_lru_ba, a_lru_wx=a_lru_wx, a_lru_bx=a_lru_bx,
             a_lru_lambda=a_lru_lambda, a_w_out=a_w_out, c_mu=c_mu, c_w_r=c_w_r, c_w_k=c_w_k, c_w_v=c_w_v,
             c_w0=c_w0, c_w1=c_w1, c_w2=c_w2, c_a0=c_a0, c_a1=c_a1, c_a2=c_a2, c_g1=c_g1, c_g2=c_g2,
             c_k_k=c_k_k, c_k_a=c_k_a, c_r_k=c_r_k, c_gn_g=c_gn_g, c_gn_b=c_gn_b, c_w_o=c_w_o,
             ln1_g=ln1_g, ln1_b=ln1_b, ln2_g=ln2_g, ln2_b=ln2_b, mlp_w1=mlp_w1, mlp_w2=mlp_w2)
    Bp = x_prompt.shape[0]
    init = tuple(jnp.zeros((s.shape[0], Bp) + s.shape[2:], s.dtype)
                 for s in (state_mlstm_C, state_mlstm_n, state_mlstm_m, state_lru_conv, state_lru_h,
                           state_rwkv_shift, state_rwkv_S))
    y_prompt, ps = _trunk(x_prompt, init, p, True)
    y_sample, ss = _trunk(x_sample, (state_mlstm_C, state_mlstm_n, state_mlstm_m, state_lru_conv,
                                     state_lru_h, state_rwkv_shift, state_rwkv_S), p, False)
    return (y_prompt, y_sample, *ps, *ss)
```

```python
import functools

import jax
import jax.numpy as jnp
from jax import lax
from jax.experimental import pallas as pl
from jax.experimental.pallas import tpu as pltpu

F32 = jnp.float32
BF16 = jnp.bfloat16

LANES = 128
SUBLANES = 8
VMEM_LIMIT_BYTES = 56 * 1024 * 1024

CHUNK = 256
LRU_C = 8.0
LN_EPS = 1e-5
RWKV_GN_EPS = 64e-5
HIGHEST = lax.Precision.HIGHEST
EXP_NEG_HALF = 0.6065306597126334


def _params(*sem):
    return pltpu.CompilerParams(dimension_semantics=sem, vmem_limit_bytes=VMEM_LIMIT_BYTES)


def _tile(n, pref):
    t = min(n, pref)
    while n % t:
        t -= 1
    return t


def _sigmoid(x):
    return 1.0 / (1.0 + jnp.exp(-x))


def _softplus(x):
    return jnp.maximum(x, 0.0) + jnp.log1p(jnp.exp(-jnp.abs(x)))


def _layer_norm(y, g, b):
    mu = jnp.mean(y, -1, keepdims=True)
    d = y - mu
    var = jnp.mean(d * d, -1, keepdims=True)
    return d * lax.rsqrt(var + LN_EPS) * g + b


def _mm_kernel(a_ref, w_ref, o_ref, ab_ref):
    @pl.when(pl.program_id(1) == 0)
    def _():
        ab_ref[...] = a_ref[...].astype(BF16)

    o_ref[...] = jnp.dot(ab_ref[...], w_ref[...], preferred_element_type=F32)


def matmul(a, w, tm=1024, tn=1024):
    M, K = a.shape
    N = w.shape[1]
    tm, tn = _tile(M, tm), _tile(N, tn)
    return pl.pallas_call(
        _mm_kernel,
        grid=(M // tm, N // tn),
        in_specs=[pl.BlockSpec((tm, K), lambda i, j: (i, 0)),
                  pl.BlockSpec((K, tn), lambda i, j: (0, j))],
        out_specs=pl.BlockSpec((tm, tn), lambda i, j: (i, j)),
        out_shape=jax.ShapeDtypeStruct((M, N), F32),
        scratch_shapes=[pltpu.VMEM((tm, K), BF16)],
        compiler_params=_params("parallel", "arbitrary"),
        name="matmul",
    )(a, w)


def _proj_ln_kernel(*refs, n_a, alpha, gated):
    a_refs, rest = refs[:n_a], list(refs[n_a:])
    gate_ref = rest.pop(0) if gated else None
    w_ref, x_ref, g_ref, b_ref, o_ref = rest[:5]

    def term(a_ref):
        a = a_ref[...]
        if gated:
            a = a * gate_ref[...]
        return jnp.dot(a.astype(BF16), w_ref[...], preferred_element_type=F32)

    def finish(acc):
        o_ref[...] = _layer_norm(alpha * x_ref[...] + acc, g_ref[...], b_ref[...])

    if n_a == 1:
        finish(term(a_refs[0]))
    else:
        acc_ref = rest[5]
        k = pl.program_id(1)

        @pl.when(k == 0)
        def _():
            acc_ref[...] = term(a_refs[0])

        @pl.when(k == 1)
        def _():
            finish(acc_ref[...] + term(a_refs[1]))


PROJ_VMEM_BUDGET = 46 * 1024 * 1024


def proj_ln(a_list, gate, w, x, g, b, alpha):
    n_a = len(a_list)
    M, K = a_list[0].shape
    D = w.shape[1]
    assert n_a in (1, 2) and w.shape[0] == n_a * K
    gated = gate is not None

    def vmem_bytes(tm):
        blocks = sum(tm * K * a.dtype.itemsize for a in a_list) + gated * tm * K * 4
        blocks += K * D * w.dtype.itemsize + 2 * tm * D * 4
        return 2 * blocks + (n_a > 1) * tm * D * 4

    tm = next(t for t in (512, 256, 128, 64, 32, 16, 8) if M % t == 0 and vmem_bytes(t) <= PROJ_VMEM_BUDGET)
    a_spec = pl.BlockSpec((tm, K), lambda i, k: (i, 0))
    row_spec = pl.BlockSpec((tm, D), lambda i, k: (i, 0))
    vec_spec = pl.BlockSpec((1, D), lambda i, k: (0, 0))
    in_specs = [a_spec] * n_a + [a_spec] * gated
    in_specs += [pl.BlockSpec((K, D), lambda i, k: (k, 0)), row_spec, vec_spec, vec_spec]
    args = list(a_list) + [gate] * gated + [w, x, g.reshape(1, D), b.reshape(1, D)]
    return pl.pallas_call(
        functools.partial(_proj_ln_kernel, n_a=n_a, alpha=alpha, gated=gated),
        grid=(M // tm, n_a),
        in_specs=in_specs,
        out_specs=row_spec,
        out_shape=jax.ShapeDtypeStruct((M, D), F32),
        scratch_shapes=[pltpu.VMEM((tm, D), F32)] * (n_a > 1),
        compiler_params=_params("parallel", "arbitrary"),
        name="proj_ln",
    )(*args)


def _mlp_ln_kernel(x_ref, w1_ref, w2_ref, g_ref, b_ref, o_ref, xb_ref, *, nf, alpha):
    f = pl.program_id(1)

    @pl.when(f == 0)
    def _():
        o_ref[...] = jnp.zeros_like(o_ref)
        xb_ref[...] = x_ref[...].astype(BF16)

    h = jnp.maximum(jnp.dot(xb_ref[...], w1_ref[...], preferred_element_type=F32), 0.0)
    o_ref[...] += jnp.dot((h * h).astype(BF16), w2_ref[...], preferred_element_type=F32)

    @pl.when(f == nf - 1)
    def _():
        y = alpha * x_ref[...] + o_ref[...]
        o_ref[...] = _layer_norm(y, g_ref[...], b_ref[...])


def mlp_ln(x, w1, w2, g, b, alpha, tm=1024, tf=512):
    M, D = x.shape
    FF = w1.shape[1]
    tm, tf = _tile(M, tm), _tile(FF, tf)
    nf = FF // tf
    row_spec = pl.BlockSpec((tm, D), lambda i, f: (i, 0))
    vec_spec = pl.BlockSpec((1, D), lambda i, f: (0, 0))
    return pl.pallas_call(
        functools.partial(_mlp_ln_kernel, nf=nf, alpha=alpha),
        grid=(M // tm, nf),
        in_specs=[row_spec,
                  pl.BlockSpec((D, tf), lambda i, f: (0, f)),
                  pl.BlockSpec((tf, D), lambda i, f: (f, 0)),
                  vec_spec, vec_spec],
        out_specs=row_spec,
        out_shape=jax.ShapeDtypeStruct((M, D), F32),
        scratch_shapes=[pltpu.VMEM((tm, D), BF16)],
        compiler_params=_params("parallel", "arbitrary"),
        name="mlp_ln",
    )(x, w1, w2, g.reshape(1, D), b.reshape(1, D))


def _mlstm_kernel(q_ref, k_ref, v_ref, o_ref, gate_ref, gbias_ref, nw_ref, c0_ref, n0_ref, m0_ref,
                  h_ref, c1_ref, n1_ref, m1_ref, *, heads, dk, dv, L, n_chunks, nt):
    t = pl.program_id(1)

    @pl.when(t == 0)
    def _():
        c1_ref[...] = c0_ref[...]
        n1_ref[...] = n0_ref[...]
        m1_ref[...] = m0_ref[...]

    row = lax.broadcasted_iota(jnp.int32, (L, L), 0)
    col = lax.broadcasted_iota(jnp.int32, (L, L), 1)
    causal = row >= col
    tril = causal.astype(F32)
    sel_r = lax.broadcasted_iota(jnp.int32, (2 * heads, LANES), 0)
    sel_c = lax.broadcasted_iota(jnp.int32, (2 * heads, LANES), 1)
    pick = (sel_r == sel_c).astype(F32)
    lane = lax.broadcasted_iota(jnp.int32, (L, LANES), 1)
    scale = dk ** -0.5

    for c in range(n_chunks):
        r0 = c * L
        gt = gate_ref[0, r0:r0 + L, :] + gbias_ref[...]
        logf = jnp.minimum(gt, 0.0) - jnp.log1p(jnp.exp(-jnp.abs(gt)))
        gl = jnp.where(lane < heads, gt, logf)
        cum = jnp.dot(tril, gl, preferred_element_type=F32, precision=HIGHEST)
        nt_dims = (((1,), (1,)), ((), ()))
        gl_t = lax.dot_general(pick, gl, nt_dims, preferred_element_type=F32, precision=HIGHEST)
        cum_t = lax.dot_general(pick, cum, nt_dims, preferred_element_type=F32, precision=HIGHEST)
        for h in range(heads):
            ig_col = gl[:, h:h + 1]
            b_col = cum[:, heads + h:heads + h + 1]
            ig_row = gl_t[h:h + 1, :]
            b_row = cum_t[heads + h:heads + h + 1, :]
            b_last = b_row[:, L - 1:L]
            m_prev = m1_ref[0, :, h:h + 1]
            qh = (q_ref[0, r0:r0 + L, h * dk:(h + 1) * dk] * scale).astype(BF16)
            kf = k_ref[0, r0:r0 + L, h * dk:(h + 1) * dk]
            kh = kf.astype(BF16)
            vh = v_ref[0, r0:r0 + L, h * dv:(h + 1) * dv].astype(BF16)
            c_prev = c1_ref[0, h]
            n_prev = n1_ref[0, h:h + 1, :]

            dmat = jnp.where(causal, b_col - b_row + ig_row, -jnp.inf)
            inter = b_col + m_prev
            m_t = jnp.maximum(inter, jnp.max(dmat, -1, keepdims=True))
            w_intra = jnp.exp(dmat - m_t)
            w_inter = jnp.exp(inter - m_t)
            s = lax.dot_general(qh, kh, nt_dims, preferred_element_type=F32)
            qk = s * w_intra
            num = jnp.dot(qk.astype(BF16), vh, preferred_element_type=F32)
            num = num + w_inter * jnp.dot(qh, c_prev.astype(BF16), preferred_element_type=F32)
            qn = jnp.sum(qh.astype(F32) * n_prev, -1, keepdims=True)
            den = jnp.sum(qk, -1, keepdims=True) + w_inter * qn
            hh = num / jnp.maximum(jnp.abs(den), jnp.exp(-m_t))

            mu = jnp.mean(hh, -1, keepdims=True)
            d = hh - mu
            var = jnp.mean(d * d, -1, keepdims=True)
            hn = d * lax.rsqrt(var + LN_EPS)
            og = o_ref[0, r0:r0 + L, h * dv:(h + 1) * dv]
            hg = hn * nw_ref[:, h * dv:(h + 1) * dv] * _sigmoid(og)
            h_ref[0, r0:r0 + L, h * dv:(h + 1) * dv] = hg.astype(h_ref.dtype)

            lw_col = b_last - b_col + ig_col
            lw_row = b_last - b_row + ig_row
            m_new = jnp.maximum(b_last + m_prev, jnp.max(lw_row, -1, keepdims=True))
            ws_col = jnp.exp(lw_col - m_new)
            wc = jnp.exp(b_last + m_prev - m_new)
            kw = kf * ws_col
            tn_dims = (((0,), (0,)), ((), ()))
            c1_ref[0, h] = wc * c_prev + lax.dot_general(kw.astype(BF16), vh, tn_dims,
                                                         preferred_element_type=F32)
            n1_ref[0, h:h + 1, :] = wc * n_prev + jnp.sum(kw, 0, keepdims=True)
            m1_ref[0, :, h:h + 1] = m_new


def mlstm(u_qkvo, gates, gbias, norm_w, c0, n0, m0, heads, dk, dv):
    B, T, _ = u_qkvo.shape
    L = min(T, CHUNK)
    assert T % L == 0
    tb = L
    n_chunks = tb // L
    nt = T // tb
    qw, vw = heads * dk, heads * dv
    assert qw % LANES == 0 and vw == 2 * qw
    st = lambda i, t: (i, 0, 0)
    kern = functools.partial(_mlstm_kernel, heads=heads, dk=dk, dv=dv, L=L, n_chunks=n_chunks, nt=nt)
    return pl.pallas_call(
        kern,
        grid=(B, nt),
        in_specs=[pl.BlockSpec((1, tb, qw), lambda i, t: (i, t, 0)),
                  pl.BlockSpec((1, tb, qw), lambda i, t: (i, t, 1)),
                  pl.BlockSpec((1, tb, vw), lambda i, t: (i, t, 1)),
                  pl.BlockSpec((1, tb, vw), lambda i, t: (i, t, 2)),
                  pl.BlockSpec((1, tb, LANES), lambda i, t: (i, t, 0)),
                  pl.BlockSpec((1, LANES), lambda i, t: (0, 0)),
                  pl.BlockSpec((1, vw), lambda i, t: (0, 0)),
                  pl.BlockSpec((1, heads, dk, dv), lambda i, t: (i, 0, 0, 0)),
                  pl.BlockSpec((1, heads, dk), st),
                  pl.BlockSpec((1, 1, heads), st)],
        out_specs=[pl.BlockSpec((1, tb, vw), lambda i, t: (i, t, 0)),
                   pl.BlockSpec((1, heads, dk, dv), lambda i, t: (i, 0, 0, 0)),
                   pl.BlockSpec((1, heads, dk), st),
                   pl.BlockSpec((1, 1, heads), st)],
        out_shape=[jax.ShapeDtypeStruct((B, T, vw), BF16),
                   jax.ShapeDtypeStruct((B, heads, dk, dv), F32),
                   jax.ShapeDtypeStruct((B, heads, dk), F32),
                   jax.ShapeDtypeStruct((B, 1, heads), F32)],
        compiler_params=_params("parallel", "arbitrary"),
        name="mlstm",
    )(u_qkvo, u_qkvo, u_qkvo, u_qkvo, gates, gbias, norm_w, c0, n0, m0)


def _gelu_tanh(x):
    return 0.5 * x * (1.0 + jnp.tanh(0.7978845608028654 * (x + 0.044715 * x * x * x)))


def _lru_kernel(xr_ref, yg_ref, conv0_ref, h0_ref, cw_ref, cb_ref, wa_ref, ba_ref, wx_ref, bx_ref, lam_ref,
                y_ref, conv1_ref, h1_ref, xp_ref, a_ref, b_ref, *, tb, blocks, cw, reset_first):
    t = pl.program_id(1)
    halo = cw - 1
    base = SUBLANES

    @pl.when(t == 0)
    def _():
        xp_ref[base - halo:base, :] = conv0_ref[0]
        h1_ref[0] = h0_ref[0]

    xp_ref[base:base + tb, :] = xr_ref[0]
    xc = cb_ref[...] + xp_ref[base:base + tb, :] * cw_ref[cw - 1:cw, :]
    for j in range(cw - 1):
        xc = xc + xp_ref[base - halo + j:base - halo + j + tb, :] * cw_ref[j:j + 1, :]

    sp = _softplus(-lam_ref[...])
    bw = xc.shape[1] // blocks
    for g in range(blocks):
        sl = slice(g * bw, (g + 1) * bw)
        xg = xc[:, sl]
        xgb = xg.astype(BF16)
        gr = _sigmoid(jnp.dot(xgb, wa_ref[g], preferred_element_type=F32) + ba_ref[:, sl])
        gi = _sigmoid(jnp.dot(xgb, wx_ref[g], preferred_element_type=F32) + bx_ref[:, sl])
        log_a = -LRU_C * gr * sp[:, sl]
        th = jnp.tanh(log_a)
        mult = jnp.sqrt(-2.0 * th / (1.0 - th))
        if reset_first:
            first = (lax.broadcasted_iota(jnp.int32, mult.shape, 0) == 0) & (t == 0)
            mult = jnp.where(first, 1.0, mult)
        a_ref[:, sl] = jnp.exp(log_a)
        b_ref[:, sl] = mult * gi * xg

    def step(i, h):
        h = a_ref[pl.ds(i, 1), :] * h + b_ref[pl.ds(i, 1), :]
        b_ref[pl.ds(i, 1), :] = h
        return h

    h_last = lax.fori_loop(0, tb, step, h1_ref[0], unroll=8)
    h1_ref[0] = h_last
    y_ref[0] = (b_ref[...] * _gelu_tanh(yg_ref[0])).astype(y_ref.dtype)
    tail = xp_ref[base + tb - halo:base + tb, :]
    xp_ref[base - halo:base, :] = tail
    conv1_ref[0] = tail


def lru(u_xy, conv0, h0, conv_w, conv_b, wa, ba, wx, bx, lam, reset_first):
    B, T, W2 = u_xy.shape
    W = W2 // 2
    cw = conv_w.shape[0]
    blocks = wa.shape[0]
    tb = _tile(T, 256)
    assert tb >= cw - 1 and cw - 1 <= SUBLANES
    st = lambda i, t: (i, 0, 0)
    vec = pl.BlockSpec((1, W), lambda i, t: (0, 0))
    wsp = pl.BlockSpec(wa.shape, lambda i, t: (0, 0, 0))
    kern = functools.partial(_lru_kernel, tb=tb, blocks=blocks, cw=cw, reset_first=reset_first)
    return pl.pallas_call(
        kern,
        grid=(B, T // tb),
        in_specs=[pl.BlockSpec((1, tb, W), lambda i, t: (i, t, 0)),
                  pl.BlockSpec((1, tb, W), lambda i, t: (i, t, 1)),
                  pl.BlockSpec((1, cw - 1, W), st),
                  pl.BlockSpec((1, 1, W), st),
                  pl.BlockSpec((cw, W), lambda i, t: (0, 0)),
                  vec, wsp, vec, wsp, vec, vec],
        out_specs=[pl.BlockSpec((1, tb, W), lambda i, t: (i, t, 0)),
                   pl.BlockSpec((1, cw - 1, W), st),
                   pl.BlockSpec((1, 1, W), st)],
        out_shape=[jax.ShapeDtypeStruct((B, T, W), BF16),
                   jax.ShapeDtypeStruct((B, cw - 1, W), F32),
                   jax.ShapeDtypeStruct((B, 1, W), F32)],
        scratch_shapes=[pltpu.VMEM((tb + SUBLANES, W), F32),
                        pltpu.VMEM((tb, W), F32),
                        pltpu.VMEM((tb, W), F32)],
        compiler_params=_params("parallel", "arbitrary"),
        name="lru",
    )(u_xy, u_xy, conv0, h0, conv_w, conv_b, wa, ba, wx, bx, lam)


def _mix_kernel(x_ref, shift_ref, mu_ref, o_ref, xp_ref, *, tb, n_mix):
    t = pl.program_id(1)
    base = SUBLANES

    @pl.when(t == 0)
    def _():
        xp_ref[base - 1:base, :] = shift_ref[0]

    x = x_ref[0]
    xp_ref[base:base + tb, :] = x
    xx = xp_ref[base - 1:base - 1 + tb, :] - x
    for j in range(n_mix):
        o_ref[j, 0] = (x + xx * mu_ref[j:j + 1, :]).astype(BF16)
    xp_ref[base - 1:base, :] = xp_ref[base + tb - 1:base + tb, :]


def rwkv_mix(x, shift, mu):
    B, T, D = x.shape
    n_mix = mu.shape[0]
    tb = _tile(T, 256)
    return pl.pallas_call(
        functools.partial(_mix_kernel, tb=tb, n_mix=n_mix),
        grid=(B, T // tb),
        in_specs=[pl.BlockSpec((1, tb, D), lambda i, t: (i, t, 0)),
                  pl.BlockSpec((1, 1, D), lambda i, t: (i, 0, 0)),
                  pl.BlockSpec((n_mix, D), lambda i, t: (0, 0))],
        out_specs=pl.BlockSpec((n_mix, 1, tb, D), lambda i, t: (0, i, t, 0)),
        out_shape=jax.ShapeDtypeStruct((n_mix, B, T, D), BF16),
        scratch_shapes=[pltpu.VMEM((tb + SUBLANES, D), F32)],
        compiler_params=_params("parallel", "arbitrary"),
        name="rwkv_mix",
    )(x, shift, mu)


def _lora_math(x, w1_ref, w2_ref, bias_ref, mid, post):
    z = jnp.dot(x, w1_ref[...], preferred_element_type=F32)
    if mid == "tanh":
        z = jnp.tanh(z)
    elif mid == "sigmoid":
        z = _sigmoid(z)
    y = jnp.dot(z.astype(BF16), w2_ref[...], preferred_element_type=F32)
    if post == "decay":
        y = jnp.exp(-EXP_NEG_HALF * _sigmoid(bias_ref[...] + y))
    elif post == "sigmoid":
        y = _sigmoid(bias_ref[...] + y)
    return y


def _lora_kernel(x_ref, w1_ref, w2_ref, bias_ref, o_ref, *, mid, post):
    o_ref[...] = _lora_math(x_ref[0], w1_ref, w2_ref, bias_ref, mid, post)


def lora(xs, j, w1, w2, bias, mid, post, tm=512):
    _, M, D = xs.shape
    R = w1.shape[1]
    N = w2.shape[1]
    tm = _tile(M, tm)
    return pl.pallas_call(
        functools.partial(_lora_kernel, mid=mid, post=post),
        grid=(M // tm,),
        in_specs=[pl.BlockSpec((1, tm, D), lambda i: (j, i, 0)),
                  pl.BlockSpec((D, R), lambda i: (0, 0)),
                  pl.BlockSpec((R, N), lambda i: (0, 0)),
                  pl.BlockSpec((1, N), lambda i: (0, 0))],
        out_specs=pl.BlockSpec((tm, N), lambda i: (i, 0)),
        out_shape=jax.ShapeDtypeStruct((M, N), F32),
        compiler_params=_params("parallel"),
        name="lora",
    )(xs, w1, w2, bias)


def _rwkv_scan_kernel(r_ref, k_ref, v_ref, w_ref, a_ref, kk_p, ka_p, rk_p, gg_p, gb_p, s0_ref,
                      y_ref, s1_ref, nkk_s, kka_s, km_s, *, tt, n):
    t = pl.program_id(1)
    nb = n // SUBLANES

    @pl.when(t == 0)
    def _():
        s1_ref[...] = s0_ref[...]

    def prow(p_ref, f):
        return p_ref[f:f + 1, :]

    nrm = jnp.zeros((tt, LANES), F32)
    for f in range(n):
        kk = k_ref[0, f] * prow(kk_p, f)
        nrm = nrm + kk * kk
    inv = lax.rsqrt(jnp.maximum(nrm, 1e-24))
    for f in range(n):
        kf, af = k_ref[0, f], a_ref[0, f]
        kk = kf * prow(kk_p, f) * inv
        nkk_s[f] = -kk
        kka_s[f] = kk * af
        km_s[f] = kf * (1.0 + (af - 1.0) * prow(ka_p, f))

    zeros = tuple(jnp.zeros((SUBLANES, LANES), F32) for _ in range(nb))

    def reduce_keys(kx, acc):
        nk = nkk_s[kx, pl.ds(0, 1), :]
        return tuple(acc[jb] + s1_ref[0, jb, kx] * nk for jb in range(nb))

    sa0 = lax.fori_loop(0, n, reduce_keys, zeros, unroll=8)

    def time_step(i, sa):
        row = pl.ds(i, 1)
        nxt = pl.ds(jnp.minimum(i + 1, tt - 1), 1)
        vt = [jnp.concatenate([v_ref[0, jb * SUBLANES + u, row, :] for u in range(SUBLANES)], 0)
              for jb in range(nb)]

        def update_keys(kx, acc):
            wr = w_ref[0, kx, row, :]
            ar = kka_s[kx, row, :]
            mr = km_s[kx, row, :]
            rr = r_ref[0, kx, row, :]
            nk = nkk_s[kx, nxt, :]
            ys, sn = [], []
            for jb in range(nb):
                s = s1_ref[0, jb, kx] * wr + sa[jb] * ar + vt[jb] * mr
                s1_ref[0, jb, kx] = s
                ys.append(acc[jb] + s * rr)
                sn.append(acc[nb + jb] + s * nk)
            return tuple(ys + sn)

        acc = lax.fori_loop(0, n, update_keys, zeros + zeros, unroll=8)
        for jb in range(nb):
            for u in range(SUBLANES):
                y_ref[0, jb * SUBLANES + u, row, :] = acc[jb][u:u + 1, :]
        return tuple(acc[nb:])

    lax.fori_loop(0, tt, time_step, sa0)

    mu = jnp.zeros((tt, LANES), F32)
    cb = jnp.zeros((tt, LANES), F32)
    for f in range(n):
        mu = mu + y_ref[0, f]
        cb = cb + r_ref[0, f] * km_s[f] * prow(rk_p, f)
    mu = mu * (1.0 / n)
    var = jnp.zeros((tt, LANES), F32)
    for f in range(n):
        d = y_ref[0, f] - mu
        var = var + d * d
    rs = lax.rsqrt(var * (1.0 / n) + RWKV_GN_EPS)
    for f in range(n):
        y_ref[0, f] = (y_ref[0, f] - mu) * rs * prow(gg_p, f) + prow(gb_p, f) + cb * v_ref[0, f]


def rwkv_scan(r, k, v, w, a, kk_p, ka_p, rk_p, gg_p, gb_p, s0):
    G, n, T, _ = r.shape
    tt = _tile(T, 32)
    seq = pl.BlockSpec((1, n, tt, LANES), lambda g, t: (g, 0, t, 0))
    par = pl.BlockSpec((n, LANES), lambda g, t: (0, 0))
    st = pl.BlockSpec((1, n // SUBLANES, n, SUBLANES, LANES), lambda g, t: (g, 0, 0, 0, 0))
    return pl.pallas_call(
        functools.partial(_rwkv_scan_kernel, tt=tt, n=n),
        grid=(G, T // tt),
        in_specs=[seq] * 5 + [par] * 5 + [st],
        out_specs=[seq, st],
        out_shape=[jax.ShapeDtypeStruct((G, n, T, LANES), F32),
                   jax.ShapeDtypeStruct(s0.shape, F32)],
        scratch_shapes=[pltpu.VMEM((n, tt, LANES), F32)] * 3,
        compiler_params=_params("parallel", "arbitrary"),
        name="rwkv_scan",
    )(r, k, v, w, a, kk_p, ka_p, rk_p, gg_p, gb_p, s0)


def _store_scan_layout(res, z_ref, heads, c0=0):
    bl = LANES // heads
    for c in range(res.shape[1] // LANES):
        sub = [res[s * LANES:(s + 1) * LANES, c * LANES:(c + 1) * LANES].T for s in range(bl)]
        for ni in range(bl):
            tile = jnp.concatenate([sub[s][ni * heads:(ni + 1) * heads, :] for s in range(bl)], 0)
            z_ref[0, (c0 + c) * bl + ni] = tile.T


MXU_WIDTH = 256


def _mm_scan_kernel(a_ref, w_ref, z_ref, *, heads):
    bl, tq, K = a_ref.shape[1:]
    a = a_ref[0].reshape(bl * tq, K)
    for c in range(w_ref.shape[1] // MXU_WIDTH):
        cols = slice(c * MXU_WIDTH, (c + 1) * MXU_WIDTH)
        res = jnp.dot(a, w_ref[:, cols], preferred_element_type=F32)
        _store_scan_layout(res, z_ref, heads, c * (MXU_WIDTH // LANES))


def matmul_scan(xm, j, w, heads, tn=2048):
    _, B, T, K = xm.shape
    N = w.shape[1]
    bl = LANES // heads
    tn = _tile(N, tn)
    return pl.pallas_call(
        functools.partial(_mm_scan_kernel, heads=heads),
        grid=(B // bl, T // LANES, N // tn),
        in_specs=[pl.BlockSpec((1, bl, LANES, K), lambda g, t, c: (j, g, t, 0)),
                  pl.BlockSpec((K, tn), lambda g, t, c: (0, c))],
        out_specs=pl.BlockSpec((1, tn // heads, LANES, LANES), lambda g, t, c: (g, c, t, 0)),
        out_shape=jax.ShapeDtypeStruct((B // bl, N // heads, T, LANES), F32),
        compiler_params=_params("parallel", "parallel", "arbitrary"),
        name="matmul_scan",
    )(xm, w)


def _lora_scan_kernel(a_ref, w1_ref, w2_ref, bias_ref, z_ref, *, heads, mid, post):
    bl, tq, K = a_ref.shape[1:]
    a = a_ref[0].reshape(bl * tq, K)
    _store_scan_layout(_lora_math(a, w1_ref, w2_ref, bias_ref, mid, post), z_ref, heads)


def lora_scan(xm, j, w1, w2, bias, mid, post, heads):
    _, B, T, K = xm.shape
    R = w1.shape[1]
    N = w2.shape[1]
    bl = LANES // heads
    return pl.pallas_call(
        functools.partial(_lora_scan_kernel, heads=heads, mid=mid, post=post),
        grid=(B // bl, T // LANES),
        in_specs=[pl.BlockSpec((1, bl, LANES, K), lambda g, t: (j, g, t, 0)),
                  pl.BlockSpec((K, R), lambda g, t: (0, 0)),
                  pl.BlockSpec((R, N), lambda g, t: (0, 0)),
                  pl.BlockSpec((1, N), lambda g, t: (0, 0))],
        out_specs=pl.BlockSpec((1, N // heads, LANES, LANES), lambda g, t: (g, 0, t, 0)),
        out_shape=jax.ShapeDtypeStruct((B // bl, N // heads, T, LANES), F32),
        compiler_params=_params("parallel", "parallel"),
        name="lora_scan",
    )(xm, w1, w2, bias)


def _from_scan_kernel(y_ref, g_ref, o_ref, *, heads):
    bl = LANES // heads
    for c in range(o_ref.shape[2] // LANES):
        sub = [y_ref[0, c * bl + ni].T for ni in range(bl)]
        for s in range(bl):
            tile = jnp.concatenate([sub[ni][s * heads:(s + 1) * heads, :] for ni in range(bl)], 0)
            cols = slice(c * LANES, (c + 1) * LANES)
            o_ref[s, :, cols] = (tile.T * g_ref[s, :, cols]).astype(BF16)


def from_scan_gated(y, gate, heads):
    G, n, T, _ = y.shape
    B, _, D = gate.shape
    bl = LANES // heads
    tok = pl.BlockSpec((bl, LANES, D), lambda g, t: (g, t, 0))
    return pl.pallas_call(
        functools.partial(_from_scan_kernel, heads=heads),
        grid=(G, T // LANES),
        in_specs=[pl.BlockSpec((1, n, LANES, LANES), lambda g, t: (g, 0, t, 0)), tok],
        out_specs=tok,
        out_shape=jax.ShapeDtypeStruct((B, T, D), BF16),
        compiler_params=_params("parallel", "parallel"),
        name="from_scan_gated",
    )(y, gate)


def _pad_cols(w, n):
    return jnp.pad(w, ((0, 0), (0, n - w.shape[1])))


def _pad_rows(w, n):
    return jnp.pad(w, ((0, n - w.shape[0]), (0, 0)))


def _layer_a(x, st, p, li, reset_first, alpha):
    c0, n0, m0, conv0, h0 = st
    B, T, D = x.shape
    heads = p['a_b_ig'].shape[1]
    dv = p['a_mlstm_norm'].shape[1] // heads
    dk = dv // 2
    qw, vw = heads * dk, heads * dv
    W = p['a_conv_w'].shape[2]
    w_in = p['a_w_in'][li]
    n_qkvo = 2 * qw + 2 * vw
    w_qkvo = w_in[:, :n_qkvo].astype(BF16)
    w_gate = _pad_cols(w_in[:, n_qkvo:n_qkvo + 2 * heads], LANES).astype(BF16)
    w_xy = w_in[:, n_qkvo + 2 * heads:].astype(BF16)
    x2 = x.reshape(B * T, D)
    u_qkvo = matmul(x2, w_qkvo).reshape(B, T, n_qkvo)
    gates = matmul(x2, w_gate).reshape(B, T, LANES)
    u_xy = matmul(x2, w_xy).reshape(B, T, 2 * W)
    gbias = _pad_cols(jnp.concatenate([p['a_b_ig'][li], p['a_b_fg'][li]])[None, :], LANES)
    hm, c1, n1, m1 = mlstm(u_qkvo, gates, gbias, p['a_mlstm_norm'][li][None, :],
                           c0, n0, m0.reshape(B, 1, heads), heads, dk, dv)
    yb, conv1, h1 = lru(u_xy, conv0, h0.reshape(B, 1, W), p['a_conv_w'][li], p['a_conv_b'][li][None, :],
                        p['a_lru_wa'][li].astype(BF16), p['a_lru_ba'][li][None, :],
                        p['a_lru_wx'][li].astype(BF16), p['a_lru_bx'][li][None, :],
                        p['a_lru_lambda'][li][None, :], reset_first)
    assert vw == W
    y = proj_ln([hm.reshape(B * T, vw), yb.reshape(B * T, W)], None, p['a_w_out'][li].astype(BF16), x2,
                p['ln1_g'][2 * li], p['ln1_b'][2 * li], alpha)
    return y.reshape(B, T, D), (c1, n1, m1.reshape(B, heads), conv1, h1.reshape(B, W))


def _layer_c(x, shift, s0, p, li, layer, alpha):
    B, T, D = x.shape
    H, N = p['c_r_k'].shape[1:]
    M = B * T
    bl = LANES // H
    G = B // bl
    nb = N // SUBLANES
    perm = jnp.arange(D).reshape(H, N).T.reshape(D)
    xm = rwkv_mix(x, shift.reshape(B, 1, D), p['c_mu'][li])
    w_r = p['c_w_r'][li][:, perm].astype(BF16)
    w_k = p['c_w_k'][li][:, perm].astype(BF16)
    w_v = p['c_w_v'][li][:, perm].astype(BF16)
    rd = -(-p['c_w1'].shape[2] // LANES) * LANES
    ra = -(-p['c_a1'].shape[2] // LANES) * LANES
    d_args = (_pad_cols(p['c_w1'][li], rd).astype(BF16), _pad_rows(p['c_w2'][li][:, perm], rd).astype(BF16),
              p['c_w0'][li][perm][None, :], "tanh", "decay")
    a_args = (_pad_cols(p['c_a1'][li], ra).astype(BF16), _pad_rows(p['c_a2'][li][:, perm], ra).astype(BF16),
              p['c_a0'][li][perm][None, :], "none", "sigmoid")
    xm2 = xm.reshape(6, M, D)
    g = lora(xm2, 5, p['c_g1'][li].astype(BF16), p['c_g2'][li][:, perm].astype(BF16),
             jnp.zeros((1, D), F32), "sigmoid", "none")
    fused = T % LANES == 0
    if fused:
        r = matmul_scan(xm, 0, w_r, H)
        k = matmul_scan(xm, 2, w_k, H)
        v = matmul_scan(xm, 3, w_v, H)
        decay = lora_scan(xm, 1, *d_args, H)
        a = lora_scan(xm, 4, *a_args, H)
    else:
        def to_scan(z):
            return z.reshape(G, bl, T, N, H).transpose(0, 3, 2, 1, 4).reshape(G, N, T, LANES)

        r = to_scan(matmul(xm2[0], w_r))
        k = to_scan(matmul(xm2[2], w_k))
        v = to_scan(matmul(xm2[3], w_v))
        decay = to_scan(lora(xm2, 1, *d_args))
        a = to_scan(lora(xm2, 4, *a_args))

    def par(z):
        return jnp.tile(z.reshape(H, N).T, (1, bl))

    s0t = s0.reshape(G, bl, H, nb, SUBLANES, N).transpose(0, 3, 5, 4, 1, 2).reshape(G, nb, N, SUBLANES, LANES)
    yt, s1t = rwkv_scan(r, k, v, decay, a,
                        par(p['c_k_k'][li]), par(p['c_k_a'][li]), par(p['c_r_k'][li].reshape(D)),
                        par(p['c_gn_g'][li]), par(p['c_gn_b'][li]), s0t)
    s1 = s1t.reshape(G, nb, N, SUBLANES, bl, H).transpose(0, 4, 5, 1, 3, 2).reshape(B, H, N, N)
    w_o = p['c_w_o'][li][perm, :].astype(BF16)
    if fused:
        y = from_scan_gated(yt, g.reshape(B, T, D), H).reshape(M, D)
        gate = None
    else:
        y = yt.reshape(G, N, T, bl, H).transpose(0, 3, 2, 1, 4).reshape(M, D)
        gate = g
    out = proj_ln([y], gate, w_o, x.reshape(M, D), p['ln1_g'][layer], p['ln1_b'][layer], alpha)
    return out.reshape(B, T, D), (x[:, -1], s1)


def _trunk(x, states, p, reset_first):
    mC, mn, mm, cv, hl, sh, S = states
    depth = p['ln1_g'].shape[0]
    alpha = (2 * depth) ** 0.25
    B, T, D = x.shape
    new_a, new_c = [], []
    for layer in range(depth):
        li = layer // 2
        if layer % 2 == 0:
            x, st = _layer_a(x, (mC[li], mn[li], mm[li], cv[li], hl[li]), p, li, reset_first, alpha)
            new_a.append(st)
        else:
            x, st = _layer_c(x, sh[li], S[li], p, li, layer, alpha)
            new_c.append(st)
        x = mlp_ln(x.reshape(B * T, D), p['mlp_w1'][layer].astype(BF16), p['mlp_w2'][layer].astype(BF16),
                   p['ln2_g'][layer], p['ln2_b'][layer], alpha).reshape(B, T, D)
    sa = [jnp.stack([s[j] for s in new_a]) for j in range(5)]
    sc = [jnp.stack([s[j] for s in new_c]) for j in range(2)]
    return x, sa + sc


def kernel(x_prompt, x_sample, state_mlstm_C, state_mlstm_n, state_mlstm_m, state_lru_conv, state_lru_h,
           state_rwkv_shift, state_rwkv_S, a_w_in, a_b_ig, a_b_fg, a_mlstm_norm, a_conv_w, a_conv_b,
           a_lru_wa, a_lru_ba, a_lru_wx, a_lru_bx, a_lru_lambda, a_w_out, c_mu, c_w_r, c_w_k, c_w_v,
           c_w0, c_w1, c_w2, c_a0, c_a1, c_a2, c_g1, c_g2, c_k_k, c_k_a, c_r_k, c_gn_g, c_gn_b, c_w_o,
           ln1_g, ln1_b, ln2_g, ln2_b, mlp_w1, mlp_w2):
    p = dict(a_w_in=a_w_in, a_b_ig=a_b_ig, a_b_fg=a_b_fg, a_mlstm_norm=a_mlstm_norm, a_conv_w=a_conv_w,
             a_conv_b=a_conv_b, a_lru_wa=a_lru_wa, a_lru_ba=a_lru_ba, a_lru_wx=a_lru_wx, a_lru_bx=a_lru_bx,
             a_lru_lambda=a_lru_lambda, a_w_out=a_w_out, c_mu=c_mu, c_w_r=c_w_r, c_w_k=c_w_k, c_w_v=c_w_v,
             c_w0=c_w0, c_w1=c_w1, c_w2=c_w2, c_a0=c_a0, c_a1=c_a1, c_a2=c_a2, c_g1=c_g1, c_g2=c_g2,
             c_k_k=c_k_k, c_k_a=c_k_a, c_r_k=c_r_k, c_gn_g=c_gn_g, c_gn_b=c_gn_b, c_w_o=c_w_o,
             ln1_g=ln1_g, ln1_b=ln1_b, ln2_g=ln2_g, ln2_b=ln2_b, mlp_w1=mlp_w1, mlp_w2=mlp_w2)
    Bp = x_prompt.shape[0]
    init = tuple(jnp.zeros((s.shape[0], Bp) + s.shape[2:], s.dtype)
                 for s in (state_mlstm_C, state_mlstm_n, state_mlstm_m, state_lru_conv, state_lru_h,
                           state_rwkv_shift, state_rwkv_S))
    y_prompt, ps = _trunk(x_prompt, init, p, True)
    y_sample, ss = _trunk(x_sample, (state_mlstm_C, state_mlstm_n, state_mlstm_m, state_lru_conv,
                                     state_lru_h, state_rwkv_shift, state_rwkv_S), p, False)
    return (y_prompt, y_sample, *ps, *ss)
```

```python
import functools

import jax
import jax.numpy as jnp
from jax import lax
from jax.experimental import pallas as pl
from jax.experimental.pallas import tpu as pltpu

F32 = jnp.float32
BF16 = jnp.bfloat16

LANES = 128
SUBLANES = 8
VMEM_LIMIT_BYTES = 56 * 1024 * 1024

CHUNK = 256
LRU_C = 8.0
LN_EPS = 1e-5
RWKV_GN_EPS = 64e-5
HIGHEST = lax.Precision.HIGHEST
EXP_NEG_HALF = 0.6065306597126334


def _params(*sem):
    return pltpu.CompilerParams(dimension_semantics=sem, vmem_limit_bytes=VMEM_LIMIT_BYTES)


def _tile(n, pref):
    t = min(n, pref)
    while n % t:
        t -= 1
    return t


def _sigmoid(x):
    return 1.0 / (1.0 + jnp.exp(-x))


def _softplus(x):
    return jnp.maximum(x, 0.0) + jnp.log1p(jnp.exp(-jnp.abs(x)))


def _layer_norm(y, g, b):
    mu = jnp.mean(y, -1, keepdims=True)
    d = y - mu
    var = jnp.mean(d * d, -1, keepdims=True)
    return d * lax.rsqrt(var + LN_EPS) * g + b


def _mm_kernel(a_ref, w_ref, o_ref, ab_ref):
    @pl.when(pl.program_id(1) == 0)
    def _():
        ab_ref[...] = a_ref[...].astype(BF16)

    o_ref[...] = jnp.dot(ab_ref[...], w_ref[...], preferred_element_type=F32)


def matmul(a, w, tm=1024, tn=1024):
    M, K = a.shape
    N = w.shape[1]
    tm, tn = _tile(M, tm), _tile(N, tn)
    return pl.pallas_call(
        _mm_kernel,
        grid=(M // tm, N // tn),
        in_specs=[pl.BlockSpec((tm, K), lambda i, j: (i, 0)),
                  pl.BlockSpec((K, tn), lambda i, j: (0, j))],
        out_specs=pl.BlockSpec((tm, tn), lambda i, j: (i, j)),
        out_shape=jax.ShapeDtypeStruct((M, N), F32),
        scratch_shapes=[pltpu.VMEM((tm, K), BF16)],
        compiler_params=_params("parallel", "arbitrary"),
        name="matmul",
    )(a, w)


def _proj_ln_kernel(*refs, n_a, alpha, gated):
    a_refs, rest = refs[:n_a], list(refs[n_a:])
    gate_ref = rest.pop(0) if gated else None
    w_ref, x_ref, g_ref, b_ref, o_ref = rest[:5]

    def term(a_ref):
        a = a_ref[...]
        if gated:
            a = a * gate_ref[...]
        return jnp.dot(a.astype(BF16), w_ref[...], preferred_element_type=F32)

    def finish(acc):
        o_ref[...] = _layer_norm(alpha * x_ref[...] + acc, g_ref[...], b_ref[...])

    if n_a == 1:
        finish(term(a_refs[0]))
    else:
        acc_ref = rest[5]
        k = pl.program_id(1)

        @pl.when(k == 0)
        def _():
            acc_ref[...] = term(a_refs[0])

        @pl.when(k == 1)
        def _():
            finish(acc_ref[...] + term(a_refs[1]))


PROJ_VMEM_BUDGET = 46 * 1024 * 1024


def proj_ln(a_list, gate, w, x, g, b, alpha):
    n_a = len(a_list)
    M, K = a_list[0].shape
    D = w.shape[1]
    assert n_a in (1, 2) and w.shape[0] == n_a * K
    gated = gate is not None

    def vmem_bytes(tm):
        blocks = sum(tm * K * a.dtype.itemsize for a in a_list) + gated * tm * K * 4
        blocks += K * D * w.dtype.itemsize + 2 * tm * D * 4
        return 2 * blocks + (n_a > 1) * tm * D * 4

    tm = next(t for t in (512, 256, 128, 64, 32, 16, 8) if M % t == 0 and vmem_bytes(t) <= PROJ_VMEM_BUDGET)
    a_spec = pl.BlockSpec((tm, K), lambda i, k: (i, 0))
    row_spec = pl.BlockSpec((tm, D), lambda i, k: (i, 0))
    vec_spec = pl.BlockSpec((1, D), lambda i, k: (0, 0))
    in_specs = [a_spec] * n_a + [a_spec] * gated
    in_specs += [pl.BlockSpec((K, D), lambda i, k: (k, 0)), row_spec, vec_spec, vec_spec]
    args = list(a_list) + [gate] * gated + [w, x, g.reshape(1, D), b.reshape(1, D)]
    return pl.pallas_call(
        functools.partial(_proj_ln_kernel, n_a=n_a, alpha=alpha, gated=gated),
        grid=(M // tm, n_a),
        in_specs=in_specs,
        out_specs=row_spec,
        out_shape=jax.ShapeDtypeStruct((M, D), F32),
        scratch_shapes=[pltpu.VMEM((tm, D), F32)] * (n_a > 1),
        compiler_params=_params("parallel", "arbitrary"),
        name="proj_ln",
    )(*args)


def _mlp_ln_kernel(x_ref, w1_ref, w2_ref, g_ref, b_ref, o_ref, xb_ref, *, nf, alpha):
    f = pl.program_id(1)

    @pl.when(f == 0)
    def _():
        o_ref[...] = jnp.zeros_like(o_ref)
        xb_ref[...] = x_ref[...].astype(BF16)

    h = jnp.maximum(jnp.dot(xb_ref[...], w1_ref[0], preferred_element_type=F32), 0.0)
    o_ref[...] += jnp.dot((h * h).astype(BF16), w2_ref[0], preferred_element_type=F32)

    @pl.when(f == nf - 1)
    def _():
        y = alpha * x_ref[...] + o_ref[...]
        o_ref[...] = _layer_norm(y, g_ref[...], b_ref[...])


def mlp_ln(x, w1, w2, layer, g, b, alpha, tm=1024, tf=512):
    M, D = x.shape
    FF = w1.shape[2]
    tm, tf = _tile(M, tm), _tile(FF, tf)
    nf = FF // tf
    row_spec = pl.BlockSpec((tm, D), lambda i, f: (i, 0))
    vec_spec = pl.BlockSpec((1, D), lambda i, f: (0, 0))
    return pl.pallas_call(
        functools.partial(_mlp_ln_kernel, nf=nf, alpha=alpha),
        grid=(M // tm, nf),
        in_specs=[row_spec,
                  pl.BlockSpec((1, D, tf), lambda i, f: (layer, 0, f)),
                  pl.BlockSpec((1, tf, D), lambda i, f: (layer, f, 0)),
                  vec_spec, vec_spec],
        out_specs=row_spec,
        out_shape=jax.ShapeDtypeStruct((M, D), F32),
        scratch_shapes=[pltpu.VMEM((tm, D), BF16)],
        compiler_params=_params("parallel", "arbitrary"),
        name="mlp_ln",
    )(x, w1, w2, g.reshape(1, D), b.reshape(1, D))


def _mlstm_kernel(q_ref, k_ref, v_ref, o_ref, gate_ref, gbias_ref, nw_ref, c0_ref, n0_ref, m0_ref,
                  h_ref, c1_ref, n1_ref, m1_ref, *, heads, dk, dv, L, n_chunks, nt):
    t = pl.program_id(1)

    @pl.when(t == 0)
    def _():
        c1_ref[...] = c0_ref[...]
        n1_ref[...] = n0_ref[...]
        m1_ref[...] = m0_ref[...]

    row = lax.broadcasted_iota(jnp.int32, (L, L), 0)
    col = lax.broadcasted_iota(jnp.int32, (L, L), 1)
    causal = row >= col
    tril = causal.astype(F32)
    sel_r = lax.broadcasted_iota(jnp.int32, (2 * heads, LANES), 0)
    sel_c = lax.broadcasted_iota(jnp.int32, (2 * heads, LANES), 1)
    pick = (sel_r == sel_c).astype(F32)
    lane = lax.broadcasted_iota(jnp.int32, (L, LANES), 1)
    scale = dk ** -0.5

    for c in range(n_chunks):
        r0 = c * L
        gt = gate_ref[0, r0:r0 + L, :] + gbias_ref[...]
        logf = jnp.minimum(gt, 0.0) - jnp.log1p(jnp.exp(-jnp.abs(gt)))
        gl = jnp.where(lane < heads, gt, logf)
        cum = jnp.dot(tril, gl, preferred_element_type=F32, precision=HIGHEST)
        nt_dims = (((1,), (1,)), ((), ()))
        gl_t = lax.dot_general(pick, gl, nt_dims, preferred_element_type=F32, precision=HIGHEST)
        cum_t = lax.dot_general(pick, cum, nt_dims, preferred_element_type=F32, precision=HIGHEST)
        for h in range(heads):
            ig_col = gl[:, h:h + 1]
            b_col = cum[:, heads + h:heads + h + 1]
            ig_row = gl_t[h:h + 1, :]
            b_row = cum_t[heads + h:heads + h + 1, :]
            b_last = b_row[:, L - 1:L]
            m_prev = m1_ref[0, :, h:h + 1]
            qh = (q_ref[0, r0:r0 + L, h * dk:(h + 1) * dk] * scale).astype(BF16)
            kf = k_ref[0, r0:r0 + L, h * dk:(h + 1) * dk]
            kh = kf.astype(BF16)
            vh = v_ref[0, r0:r0 + L, h * dv:(h + 1) * dv].astype(BF16)
            c_prev = c1_ref[0, h]
            n_prev = n1_ref[0, h:h + 1, :]

            dmat = jnp.where(causal, b_col - b_row + ig_row, -jnp.inf)
            inter = b_col + m_prev
            m_t = jnp.maximum(inter, jnp.max(dmat, -1, keepdims=True))
            w_intra = jnp.exp(dmat - m_t)
            w_inter = jnp.exp(inter - m_t)
            s = lax.dot_general(qh, kh, nt_dims, preferred_element_type=F32)
            qk = s * w_intra
            num = jnp.dot(qk.astype(BF16), vh, preferred_element_type=F32)
            num = num + w_inter * jnp.dot(qh, c_prev.astype(BF16), preferred_element_type=F32)
            qn = jnp.sum(qh.astype(F32) * n_prev, -1, keepdims=True)
            den = jnp.sum(qk, -1, keepdims=True) + w_inter * qn
            hh = num / jnp.maximum(jnp.abs(den), jnp.exp(-m_t))

            mu = jnp.mean(hh, -1, keepdims=True)
            d = hh - mu
            var = jnp.mean(d * d, -1, keepdims=True)
            hn = d * lax.rsqrt(var + LN_EPS)
            og = o_ref[0, r0:r0 + L, h * dv:(h + 1) * dv]
            hg = hn * nw_ref[:, h * dv:(h + 1) * dv] * _sigmoid(og)
            h_ref[0, r0:r0 + L, h * dv:(h + 1) * dv] = hg.astype(h_ref.dtype)

            lw_col = b_last - b_col + ig_col
            lw_row = b_last - b_row + ig_row
            m_new = jnp.maximum(b_last + m_prev, jnp.max(lw_row, -1, keepdims=True))
            ws_col = jnp.exp(lw_col - m_new)
            wc = jnp.exp(b_last + m_prev - m_new)
            kw = kf * ws_col
            tn_dims = (((0,), (0,)), ((), ()))
            c1_ref[0, h] = wc * c_prev + lax.dot_general(kw.astype(BF16), vh, tn_dims,
                                                         preferred_element_type=F32)
            n1_ref[0, h:h + 1, :] = wc * n_prev + jnp.sum(kw, 0, keepdims=True)
            m1_ref[0, :, h:h + 1] = m_new


def mlstm(u_qkvo, gates, gbias, norm_w, c0, n0, m0, heads, dk, dv):
    B, T, _ = u_qkvo.shape
    L = min(T, CHUNK)
    assert T % L == 0
    tb = L
    n_chunks = tb // L
    nt = T // tb
    qw, vw = heads * dk, heads * dv
    assert qw % LANES == 0 and vw == 2 * qw
    st = lambda i, t: (i, 0, 0)
    kern = functools.partial(_mlstm_kernel, heads=heads, dk=dk, dv=dv, L=L, n_chunks=n_chunks, nt=nt)
    return pl.pallas_call(
        kern,
        grid=(B, nt),
        in_specs=[pl.BlockSpec((1, tb, qw), lambda i, t: (i, t, 0)),
                  pl.BlockSpec((1, tb, qw), lambda i, t: (i, t, 1)),
                  pl.BlockSpec((1, tb, vw), lambda i, t: (i, t, 1)),
                  pl.BlockSpec((1, tb, vw), lambda i, t: (i, t, 2)),
                  pl.BlockSpec((1, tb, LANES), lambda i, t: (i, t, 0)),
                  pl.BlockSpec((1, LANES), lambda i, t: (0, 0)),
                  pl.BlockSpec((1, vw), lambda i, t: (0, 0)),
                  pl.BlockSpec((1, heads, dk, dv), lambda i, t: (i, 0, 0, 0)),
                  pl.BlockSpec((1, heads, dk), st),
                  pl.BlockSpec((1, 1, heads), st)],
        out_specs=[pl.BlockSpec((1, tb, vw), lambda i, t: (i, t, 0)),
                   pl.BlockSpec((1, heads, dk, dv), lambda i, t: (i, 0, 0, 0)),
                   pl.BlockSpec((1, heads, dk), st),
                   pl.BlockSpec((1, 1, heads), st)],
        out_shape=[jax.ShapeDtypeStruct((B, T, vw), BF16),
                   jax.ShapeDtypeStruct((B, heads, dk, dv), F32),
                   jax.ShapeDtypeStruct((B, heads, dk), F32),
                   jax.ShapeDtypeStruct((B, 1, heads), F32)],
        compiler_params=_params("parallel", "arbitrary"),
        name="mlstm",
    )(u_qkvo, u_qkvo, u_qkvo, u_qkvo, gates, gbias, norm_w, c0, n0, m0)


def _gelu_tanh(x):
    return 0.5 * x * (1.0 + jnp.tanh(0.7978845608028654 * (x + 0.044715 * x * x * x)))


def _lru_kernel(xr_ref, yg_ref, conv0_ref, h0_ref, cw_ref, cb_ref, wa_ref, ba_ref, wx_ref, bx_ref, lam_ref,
                y_ref, conv1_ref, h1_ref, xp_ref, a_ref, b_ref, *, tb, blocks, cw, reset_first):
    t = pl.program_id(1)
    halo = cw - 1

    @pl.when(t == 0)
    def _():
        xp_ref[...] = jnp.zeros_like(xp_ref)
        xp_ref[SUBLANES - halo:, :] = conv0_ref[0]
        h1_ref[0] = h0_ref[0]

    x = xr_ref[0]
    prev = xp_ref[...]
    head_rows = lax.broadcasted_iota(jnp.int32, prev.shape, 0)
    xc = cb_ref[...] + x * cw_ref[cw - 1:cw, :]
    for d in range(1, cw):
        xs = pltpu.roll(x, d, 0)
        first = jnp.where(head_rows < d, pltpu.roll(prev, d, 0), xs[:SUBLANES])
        xs = jnp.concatenate([first, xs[SUBLANES:]], 0)
        xc = xc + xs * cw_ref[cw - 1 - d:cw - d, :]

    sp = _softplus(-lam_ref[...])
    bw = xc.shape[1] // blocks
    for g in range(blocks):
        sl = slice(g * bw, (g + 1) * bw)
        xg = xc[:, sl]
        xgb = xg.astype(BF16)
        gr = _sigmoid(jnp.dot(xgb, wa_ref[g], preferred_element_type=F32) + ba_ref[:, sl])
        gi = _sigmoid(jnp.dot(xgb, wx_ref[g], preferred_element_type=F32) + bx_ref[:, sl])
        log_a = -LRU_C * gr * sp[:, sl]
        th = jnp.tanh(log_a)
        z = -2.0 * th / (1.0 - th)
        mult = jnp.where(z > 0.0, z * lax.rsqrt(z), 0.0)
        if reset_first:
            first = (lax.broadcasted_iota(jnp.int32, mult.shape, 0) == 0) & (t == 0)
            mult = jnp.where(first, 1.0, mult)
        a_ref[:, sl] = jnp.exp(log_a)
        b_ref[:, sl] = mult * gi * xg

    def step(i, h):
        h = a_ref[pl.ds(i, 1), :] * h + b_ref[pl.ds(i, 1), :]
        b_ref[pl.ds(i, 1), :] = h
        return h

    h_last = lax.fori_loop(0, tb, step, h1_ref[0], unroll=8)
    h1_ref[0] = h_last
    y_ref[0] = (b_ref[...] * _gelu_tanh(yg_ref[0])).astype(y_ref.dtype)
    last = xr_ref[0, tb - SUBLANES:, :]
    xp_ref[...] = last
    conv1_ref[0] = last[SUBLANES - halo:, :]


def lru(u_xy, conv0, h0, conv_w, conv_b, wa, ba, wx, bx, lam, reset_first):
    B, T, W2 = u_xy.shape
    W = W2 // 2
    cw = conv_w.shape[0]
    blocks = wa.shape[0]
    tb = _tile(T, 256)
    assert tb >= cw - 1 and cw - 1 <= SUBLANES
    st = lambda i, t: (i, 0, 0)
    vec = pl.BlockSpec((1, W), lambda i, t: (0, 0))
    wsp = pl.BlockSpec(wa.shape, lambda i, t: (0, 0, 0))
    kern = functools.partial(_lru_kernel, tb=tb, blocks=blocks, cw=cw, reset_first=reset_first)
    return pl.pallas_call(
        kern,
        grid=(B, T // tb),
        in_specs=[pl.BlockSpec((1, tb, W), lambda i, t: (i, t, 0)),
                  pl.BlockSpec((1, tb, W), lambda i, t: (i, t, 1)),
                  pl.BlockSpec((1, cw - 1, W), st),
                  pl.BlockSpec((1, 1, W), st),
                  pl.BlockSpec((cw, W), lambda i, t: (0, 0)),
                  vec, wsp, vec, wsp, vec, vec],
        out_specs=[pl.BlockSpec((1, tb, W), lambda i, t: (i, t, 0)),
                   pl.BlockSpec((1, cw - 1, W), st),
                   pl.BlockSpec((1, 1, W), st)],
        out_shape=[jax.ShapeDtypeStruct((B, T, W), BF16),
                   jax.ShapeDtypeStruct((B, cw - 1, W), F32),
                   jax.ShapeDtypeStruct((B, 1, W), F32)],
        scratch_shapes=[pltpu.VMEM((SUBLANES, W), F32),
                        pltpu.VMEM((tb, W), F32),
                        pltpu.VMEM((tb, W), F32)],
        compiler_params=_params("parallel", "arbitrary"),
        name="lru",
    )(u_xy, u_xy, conv0, h0, conv_w, conv_b, wa, ba, wx, bx, lam)


def _mix_kernel(x_ref, shift_ref, mu_ref, o_ref, xp_ref, *, tb, n_mix):
    t = pl.program_id(1)

    @pl.when(t == 0)
    def _():
        xp_ref[...] = shift_ref[0]

    x = x_ref[0]
    xs = pltpu.roll(x, 1, 0)
    head_rows = lax.broadcasted_iota(jnp.int32, (SUBLANES, x.shape[1]), 0)
    first = jnp.where(head_rows == 0, xp_ref[...], xs[:SUBLANES])
    xx = jnp.concatenate([first, xs[SUBLANES:]], 0) - x
    for j in range(n_mix):
        o_ref[j, 0] = (x + xx * mu_ref[j:j + 1, :]).astype(BF16)
    xp_ref[...] = x_ref[0, tb - 1:, :]


def rwkv_mix(x, shift, mu):
    B, T, D = x.shape
    n_mix = mu.shape[0]
    tb = _tile(T, 256)
    return pl.pallas_call(
        functools.partial(_mix_kernel, tb=tb, n_mix=n_mix),
        grid=(B, T // tb),
        in_specs=[pl.BlockSpec((1, tb, D), lambda i, t: (i, t, 0)),
                  pl.BlockSpec((1, 1, D), lambda i, t: (i, 0, 0)),
                  pl.BlockSpec((n_mix, D), lambda i, t: (0, 0))],
        out_specs=pl.BlockSpec((n_mix, 1, tb, D), lambda i, t: (0, i, t, 0)),
        out_shape=jax.ShapeDtypeStruct((n_mix, B, T, D), BF16),
        scratch_shapes=[pltpu.VMEM((1, D), F32)],
        compiler_params=_params("parallel", "arbitrary"),
        name="rwkv_mix",
    )(x, shift, mu)


def _lora_math(x, w1_ref, w2_ref, bias_ref, mid, post):
    z = jnp.dot(x, w1_ref[...], preferred_element_type=F32)
    if mid == "tanh":
        z = jnp.tanh(z)
    elif mid == "sigmoid":
        z = _sigmoid(z)
    y = jnp.dot(z.astype(BF16), w2_ref[...], preferred_element_type=F32)
    if post == "decay":
        y = jnp.exp(-EXP_NEG_HALF * _sigmoid(bias_ref[...] + y))
    elif post == "sigmoid":
        y = _sigmoid(bias_ref[...] + y)
    return y


def _lora_kernel(x_ref, w1_ref, w2_ref, bias_ref, o_ref, *, mid, post):
    o_ref[...] = _lora_math(x_ref[0], w1_ref, w2_ref, bias_ref, mid, post)


def lora(xs, j, w1, w2, bias, mid, post, tm=512):
    _, M, D = xs.shape
    R = w1.shape[1]
    N = w2.shape[1]
    tm = _tile(M, tm)
    return pl.pallas_call(
        functools.partial(_lora_kernel, mid=mid, post=post),
        grid=(M // tm,),
        in_specs=[pl.BlockSpec((1, tm, D), lambda i: (j, i, 0)),
                  pl.BlockSpec((D, R), lambda i: (0, 0)),
                  pl.BlockSpec((R, N), lambda i: (0, 0)),
                  pl.BlockSpec((1, N), lambda i: (0, 0))],
        out_specs=pl.BlockSpec((tm, N), lambda i: (i, 0)),
        out_shape=jax.ShapeDtypeStruct((M, N), F32),
        compiler_params=_params("parallel"),
        name="lora",
    )(xs, w1, w2, bias)


def _rwkv_scan_kernel(r_ref, k_ref, v_ref, w_ref, a_ref, kk_p, ka_p, rk_p, gg_p, gb_p, s0_ref,
                      y_ref, s1_ref, nkk_s, kka_s, km_s, vt_s, ys_s, *, tt, n):
    t = pl.program_id(1)
    nb = n // SUBLANES

    @pl.when(t == 0)
    def _():
        s1_ref[...] = s0_ref[...]

    def prow(p_ref, f):
        return p_ref[f:f + 1, :]

    nrm = jnp.zeros((tt, LANES), F32)
    for f in range(n):
        kk = k_ref[0, f] * prow(kk_p, f)
        nrm = nrm + kk * kk
    inv = lax.rsqrt(jnp.maximum(nrm, 1e-24))
    for f in range(n):
        kf, af = k_ref[0, f], a_ref[0, f]
        kk = kf * prow(kk_p, f) * inv
        nkk_s[f] = -kk
        kka_s[f] = kk * af
        km_s[f] = kf * (1.0 + (af - 1.0) * prow(ka_p, f))

    vt_s[...] = jnp.swapaxes(v_ref[0], 0, 1)
    zeros = tuple(jnp.zeros((SUBLANES, LANES), F32) for _ in range(nb))

    def reduce_keys(kx, acc):
        nk = nkk_s[kx, pl.ds(0, 1), :]
        return tuple(acc[jb] + s1_ref[0, jb, kx] * nk for jb in range(nb))

    sa0 = lax.fori_loop(0, n, reduce_keys, zeros, unroll=8)

    def time_step(i, sa):
        row = pl.ds(i, 1)
        nxt = pl.ds(jnp.minimum(i + 1, tt - 1), 1)
        vt = [vt_s[i, jb * SUBLANES:(jb + 1) * SUBLANES, :] for jb in range(nb)]

        def update_keys(kx, acc):
            wr = w_ref[0, kx, row, :]
            ar = kka_s[kx, row, :]
            mr = km_s[kx, row, :]
            rr = r_ref[0, kx, row, :]
            nk = nkk_s[kx, nxt, :]
            ys, sn = [], []
            for jb in range(nb):
                s = s1_ref[0, jb, kx] * wr + sa[jb] * ar + vt[jb] * mr
                s1_ref[0, jb, kx] = s
                ys.append(acc[jb] + s * rr)
                sn.append(acc[nb + jb] + s * nk)
            return tuple(ys + sn)

        acc = lax.fori_loop(0, n, update_keys, zeros + zeros, unroll=8)
        for jb in range(nb):
            ys_s[i, jb * SUBLANES:(jb + 1) * SUBLANES, :] = acc[jb]
        return tuple(acc[nb:])

    lax.fori_loop(0, tt, time_step, sa0)
    y_ref[0] = jnp.swapaxes(ys_s[...], 0, 1)

    mu = jnp.zeros((tt, LANES), F32)
    cb = jnp.zeros((tt, LANES), F32)
    for f in range(n):
        mu = mu + y_ref[0, f]
        cb = cb + r_ref[0, f] * km_s[f] * prow(rk_p, f)
    mu = mu * (1.0 / n)
    var = jnp.zeros((tt, LANES), F32)
    for f in range(n):
        d = y_ref[0, f] - mu
        var = var + d * d
    rs = lax.rsqrt(var * (1.0 / n) + RWKV_GN_EPS)
    for f in range(n):
        y_ref[0, f] = (y_ref[0, f] - mu) * rs * prow(gg_p, f) + prow(gb_p, f) + cb * v_ref[0, f]


def rwkv_scan(r, k, v, w, a, kk_p, ka_p, rk_p, gg_p, gb_p, s0):
    G, n, T, _ = r.shape
    tt = _tile(T, 32)
    seq = pl.BlockSpec((1, n, tt, LANES), lambda g, t: (g, 0, t, 0))
    par = pl.BlockSpec((n, LANES), lambda g, t: (0, 0))
    st = pl.BlockSpec((1, n // SUBLANES, n, SUBLANES, LANES), lambda g, t: (g, 0, 0, 0, 0))
    return pl.pallas_call(
        functools.partial(_rwkv_scan_kernel, tt=tt, n=n),
        grid=(G, T // tt),
        in_specs=[seq] * 5 + [par] * 5 + [st],
        out_specs=[seq, st],
        out_shape=[jax.ShapeDtypeStruct((G, n, T, LANES), F32),
                   jax.ShapeDtypeStruct(s0.shape, F32)],
        scratch_shapes=[pltpu.VMEM((n, tt, LANES), F32)] * 3 + [pltpu.VMEM((tt, n, LANES), F32)] * 2,
        compiler_params=_params("parallel", "arbitrary"),
        name="rwkv_scan",
    )(r, k, v, w, a, kk_p, ka_p, rk_p, gg_p, gb_p, s0)


def _store_scan_layout(res, z_ref, heads, c0=0):
    bl = LANES // heads
    for c in range(res.shape[1] // LANES):
        sub = [res[s * LANES:(s + 1) * LANES, c * LANES:(c + 1) * LANES].T for s in range(bl)]
        for ni in range(bl):
            tile = jnp.concatenate([sub[s][ni * heads:(ni + 1) * heads, :] for s in range(bl)], 0)
            z_ref[0, (c0 + c) * bl + ni] = tile.T


MXU_WIDTH = 256


def _mm_scan_kernel(a_ref, w_ref, z_ref, *, heads):
    bl, tq, K = a_ref.shape[1:]
    a = a_ref[0].reshape(bl * tq, K)
    for c in range(w_ref.shape[1] // MXU_WIDTH):
        cols = slice(c * MXU_WIDTH, (c + 1) * MXU_WIDTH)
        res = jnp.dot(a, w_ref[:, cols], preferred_element_type=F32)
        _store_scan_layout(res, z_ref, heads, c * (MXU_WIDTH // LANES))


def matmul_scan(xm, j, w, heads, tn=2048):
    _, B, T, K = xm.shape
    N = w.shape[1]
    bl = LANES // heads
    tn = _tile(N, tn)
    return pl.pallas_call(
        functools.partial(_mm_scan_kernel, heads=heads),
        grid=(B // bl, T // LANES, N // tn),
        in_specs=[pl.BlockSpec((1, bl, LANES, K), lambda g, t, c: (j, g, t, 0)),
                  pl.BlockSpec((K, tn), lambda g, t, c: (0, c))],
        out_specs=pl.BlockSpec((1, tn // heads, LANES, LANES), lambda g, t, c: (g, c, t, 0)),
        out_shape=jax.ShapeDtypeStruct((B // bl, N // heads, T, LANES), F32),
        compiler_params=_params("parallel", "parallel", "arbitrary"),
        name="matmul_scan",
    )(xm, w)


def _lora_scan_kernel(a_ref, w1_ref, w2_ref, bias_ref, z_ref, *, heads, mid, post):
    bl, tq, K = a_ref.shape[1:]
    a = a_ref[0].reshape(bl * tq, K)
    _store_scan_layout(_lora_math(a, w1_ref, w2_ref, bias_ref, mid, post), z_ref, heads)


def lora_scan(xm, j, w1, w2, bias, mid, post, heads):
    _, B, T, K = xm.shape
    R = w1.shape[1]
    N = w2.shape[1]
    bl = LANES // heads
    return pl.pallas_call(
        functools.partial(_lora_scan_kernel, heads=heads, mid=mid, post=post),
        grid=(B // bl, T // LANES),
        in_specs=[pl.BlockSpec((1, bl, LANES, K), lambda g, t: (j, g, t, 0)),
                  pl.BlockSpec((K, R), lambda g, t: (0, 0)),
                  pl.BlockSpec((R, N), lambda g, t: (0, 0)),
                  pl.BlockSpec((1, N), lambda g, t: (0, 0))],
        out_specs=pl.BlockSpec((1, N // heads, LANES, LANES), lambda g, t: (g, 0, t, 0)),
        out_shape=jax.ShapeDtypeStruct((B // bl, N // heads, T, LANES), F32),
        compiler_params=_params("parallel", "parallel"),
        name="lora_scan",
    )(xm, w1, w2, bias)


def _from_scan_kernel(y_ref, g_ref, o_ref, *, heads):
    bl = LANES // heads
    for c in range(o_ref.shape[2] // LANES):
        sub = [y_ref[0, c * bl + ni].T for ni in range(bl)]
        for s in range(bl):
            tile = jnp.concatenate([sub[ni][s * heads:(s + 1) * heads, :] for ni in range(bl)], 0)
            cols = slice(c * LANES, (c + 1) * LANES)
            o_ref[s, :, cols] = (tile.T * g_ref[s, :, cols]).astype(BF16)


def from_scan_gated(y, gate, heads):
    G, n, T, _ = y.shape
    B, _, D = gate.shape
    bl = LANES // heads
    tok = pl.BlockSpec((bl, LANES, D), lambda g, t: (g, t, 0))
    return pl.pallas_call(
        functools.partial(_from_scan_kernel, heads=heads),
        grid=(G, T // LANES),
        in_specs=[pl.BlockSpec((1, n, LANES, LANES), lambda g, t: (g, 0, t, 0)), tok],
        out_specs=tok,
        out_shape=jax.ShapeDtypeStruct((B, T, D), BF16),
        compiler_params=_params("parallel", "parallel"),
        name="from_scan_gated",
    )(y, gate)


def _pad_cols(w, n):
    return jnp.pad(w, ((0, 0), (0, n - w.shape[1])))


def _pad_rows(w, n):
    return jnp.pad(w, ((0, n - w.shape[0]), (0, 0)))


def _layer_a(x, st, p, li, reset_first, alpha):
    c0, n0, m0, conv0, h0 = st
    B, T, D = x.shape
    heads = p['a_b_ig'].shape[1]
    dv = p['a_mlstm_norm'].shape[1] // heads
    dk = dv // 2
    qw, vw = heads * dk, heads * dv
    W = p['a_conv_w'].shape[2]
    w_in = p['a_w_in'][li]
    n_qkvo = 2 * qw + 2 * vw
    w_qkvo = w_in[:, :n_qkvo].astype(BF16)
    w_gate = _pad_cols(w_in[:, n_qkvo:n_qkvo + 2 * heads], LANES).astype(BF16)
    w_xy = w_in[:, n_qkvo + 2 * heads:].astype(BF16)
    x2 = x.reshape(B * T, D)
    u_qkvo = matmul(x2, w_qkvo).reshape(B, T, n_qkvo)
    gates = matmul(x2, w_gate).reshape(B, T, LANES)
    u_xy = matmul(x2, w_xy).reshape(B, T, 2 * W)
    gbias = _pad_cols(jnp.concatenate([p['a_b_ig'][li], p['a_b_fg'][li]])[None, :], LANES)
    hm, c1, n1, m1 = mlstm(u_qkvo, gates, gbias, p['a_mlstm_norm'][li][None, :],
                           c0, n0, m0.reshape(B, 1, heads), heads, dk, dv)
    yb, conv1, h1 = lru(u_xy, conv0, h0.reshape(B, 1, W), p['a_conv_w'][li], p['a_conv_b'][li][None, :],
                        p['a_lru_wa'][li].astype(BF16), p['a_lru_ba'][li][None, :],
                        p['a_lru_wx'][li].astype(BF16), p['a_lru_bx'][li][None, :],
                        p['a_lru_lambda'][li][None, :], reset_first)
    assert vw == W
    y = proj_ln([hm.reshape(B * T, vw), yb.reshape(B * T, W)], None, p['a_w_out'][li].astype(BF16), x2,
                p['ln1_g'][2 * li], p['ln1_b'][2 * li], alpha)
    return y.reshape(B, T, D), (c1, n1, m1.reshape(B, heads), conv1, h1.reshape(B, W))


def _layer_c(x, shift, s0, p, li, layer, alpha):
    B, T, D = x.shape
    H, N = p['c_r_k'].shape[1:]
    M = B * T
    bl = LANES // H
    G = B // bl
    nb = N // SUBLANES
    def cols(w):
        return w.reshape(w.shape[0], H, N).swapaxes(1, 2).reshape(w.shape[0], D)

    xm = rwkv_mix(x, shift.reshape(B, 1, D), p['c_mu'][li])
    w_r = cols(p['c_w_r'][li].astype(BF16))
    w_k = cols(p['c_w_k'][li].astype(BF16))
    w_v = cols(p['c_w_v'][li].astype(BF16))
    rd = -(-p['c_w1'].shape[2] // LANES) * LANES
    ra = -(-p['c_a1'].shape[2] // LANES) * LANES
    d_args = (_pad_cols(p['c_w1'][li], rd).astype(BF16), _pad_rows(cols(p['c_w2'][li]), rd).astype(BF16),
              cols(p['c_w0'][li][None, :]), "tanh", "decay")
    a_args = (_pad_cols(p['c_a1'][li], ra).astype(BF16), _pad_rows(cols(p['c_a2'][li]), ra).astype(BF16),
              cols(p['c_a0'][li][None, :]), "none", "sigmoid")
    xm2 = xm.reshape(6, M, D)
    g = lora(xm2, 5, p['c_g1'][li].astype(BF16), cols(p['c_g2'][li].astype(BF16)),
             jnp.zeros((1, D), F32), "sigmoid", "none")
    fused = T % LANES == 0
    if fused:
        r = matmul_scan(xm, 0, w_r, H)
        k = matmul_scan(xm, 2, w_k, H)
        v = matmul_scan(xm, 3, w_v, H)
        decay = lora_scan(xm, 1, *d_args, H)
        a = lora_scan(xm, 4, *a_args, H)
    else:
        def to_scan(z):
            return z.reshape(G, bl, T, N, H).transpose(0, 3, 2, 1, 4).reshape(G, N, T, LANES)

        r = to_scan(matmul(xm2[0], w_r))
        k = to_scan(matmul(xm2[2], w_k))
        v = to_scan(matmul(xm2[3], w_v))
        decay = to_scan(lora(xm2, 1, *d_args))
        a = to_scan(lora(xm2, 4, *a_args))

    def par(z):
        return jnp.tile(z.reshape(H, N).T, (1, bl))

    s0t = s0.reshape(G, bl, H, nb, SUBLANES, N).transpose(0, 3, 5, 4, 1, 2).reshape(G, nb, N, SUBLANES, LANES)
    yt, s1t = rwkv_scan(r, k, v, decay, a,
                        par(p['c_k_k'][li]), par(p['c_k_a'][li]), par(p['c_r_k'][li].reshape(D)),
                        par(p['c_gn_g'][li]), par(p['c_gn_b'][li]), s0t)
    s1 = s1t.reshape(G, nb, N, SUBLANES, bl, H).transpose(0, 4, 5, 1, 3, 2).reshape(B, H, N, N)
    w_o = p['c_w_o'][li].astype(BF16).reshape(H, N, D).swapaxes(0, 1).reshape(D, D)
    if fused:
        y = from_scan_gated(yt, g.reshape(B, T, D), H).reshape(M, D)
        gate = None
    else:
        y = yt.reshape(G, N, T, bl, H).transpose(0, 3, 2, 1, 4).reshape(M, D)
        gate = g
    out = proj_ln([y], gate, w_o, x.reshape(M, D), p['ln1_g'][layer], p['ln1_b'][layer], alpha)
    return out.reshape(B, T, D), (x[:, -1], s1)


def _trunk(x, states, p, reset_first):
    mC, mn, mm, cv, hl, sh, S = states
    depth = p['ln1_g'].shape[0]
    alpha = (2 * depth) ** 0.25
    B, T, D = x.shape
    new_a, new_c = [], []
    for layer in range(depth):
        li = layer // 2
        if layer % 2 == 0:
            x, st = _layer_a(x, (mC[li], mn[li], mm[li], cv[li], hl[li]), p, li, reset_first, alpha)
            new_a.append(st)
        else:
            x, st = _layer_c(x, sh[li], S[li], p, li, layer, alpha)
            new_c.append(st)
        x = mlp_ln(x.reshape(B * T, D), p['mlp_w1'].astype(BF16), p['mlp_w2'].astype(BF16), layer,
                   p['ln2_g'][layer], p['ln2_b'][layer], alpha).reshape(B, T, D)
    sa = [jnp.stack([s[j] for s in new_a]) for j in range(5)]
    sc = [jnp.stack([s[j] for s in new_c]) for j in range(2)]
    return x, sa + sc


def kernel(x_prompt, x_sample, state_mlstm_C, state_mlstm_n, state_mlstm_m, state_lru_conv, state_lru_h,
           state_rwkv_shift, state_rwkv_S, a_w_in, a_b_ig, a_b_fg, a_mlstm_norm, a_conv_w, a_conv_b,
           a_lru_wa, a_lru_ba, a_lru_wx, a_lru_bx, a_lru_lambda, a_w_out, c_mu, c_w_r, c_w_k, c_w_v,
           c_w0, c_w1, c_w2, c_a0, c_a1, c_a2, c_g1, c_g2, c_k_k, c_k_a, c_r_k, c_gn_g, c_gn_b, c_w_o,
           ln1_g, ln1_b, ln2_g, ln2_b, mlp_w1, mlp_w2):
    p = dict(a_w_in=a_w_in, a_b_ig=a_b_ig, a_b_fg=a_b_fg, a_mlstm_norm=a_mlstm_norm, a_conv_w=a_conv_w,
             a_conv_b=a_conv_b, a_lru_wa=a_lru_wa, a_lru_ba=a_lru_ba, a_lru_wx=a_lru_wx, a_lru_bx=a_lru_bx,
             a_lru_lambda=a_lru_lambda, a_w_out=a_w_out, c_mu=c_mu, c_w_r=c_w_r, c_w_k=c_w_k, c_w_v=c_w_v,
             c_w0=c_w0, c_w1=c_w1, c_w2=c_w2, c_a0=c_a0, c_a1=c_a1, c_a2=c_a2, c_g1=c_g1, c_g2=c_g2,
             c_k_k=c_k_k, c_k_a=c_k_a, c_r_k=c_r_k, c_gn_g=c_gn_g, c_gn_b=c_gn_b, c_w_o=c_w_o,
             ln1_g=ln1_g, ln1_b=ln1_b, ln2_g=ln2_g, ln2_b=ln2_b, mlp_w1=mlp_w1, mlp_w2=mlp_w2)
    Bp = x_prompt.shape[0]
    init = tuple(jnp.zeros((s.shape[0], Bp) + s.shape[2:], s.dtype)
                 for s in (state_mlstm_C, state_mlstm_n, state_mlstm_m, state_lru_conv, state_lru_h,
                           state_rwkv_shift, state_rwkv_S))
    y_prompt, ps = _trunk(x_prompt, init, p, True)
    y_sample, ss = _trunk(x_sample, (state_mlstm_C, state_mlstm_n, state_mlstm_m, state_lru_conv,
                                     state_lru_h, state_rwkv_shift, state_rwkv_S), p, False)
    return (y_prompt, y_sample, *ps, *ss)
```

```python
import functools

import jax
import jax.numpy as jnp
from jax import lax
from jax.experimental import pallas as pl
from jax.experimental.pallas import tpu as pltpu

F32 = jnp.float32
BF16 = jnp.bfloat16

LANES = 128
SUBLANES = 8
VMEM_LIMIT_BYTES = 56 * 1024 * 1024

CHUNK = 256
LRU_C = 8.0
LN_EPS = 1e-5
RWKV_GN_EPS = 64e-5
HIGHEST = lax.Precision.HIGHEST
EXP_NEG_HALF = 0.6065306597126334


def _params(*sem):
    return pltpu.CompilerParams(dimension_semantics=sem, vmem_limit_bytes=VMEM_LIMIT_BYTES)


def _tile(n, pref):
    t = min(n, pref)
    while n % t:
        t -= 1
    return t


def _sigmoid(x):
    return 1.0 / (1.0 + jnp.exp(-x))


def _softplus(x):
    return jnp.maximum(x, 0.0) + jnp.log1p(jnp.exp(-jnp.abs(x)))


def _layer_norm(y, g, b):
    mu = jnp.mean(y, -1, keepdims=True)
    d = y - mu
    var = jnp.mean(d * d, -1, keepdims=True)
    return d * lax.rsqrt(var + LN_EPS) * g + b


def _mm_kernel(a_ref, w_ref, o_ref, ab_ref):
    @pl.when(pl.program_id(1) == 0)
    def _():
        ab_ref[...] = a_ref[...].astype(BF16)

    o_ref[...] = jnp.dot(ab_ref[...], w_ref[...], preferred_element_type=F32)


def matmul(a, w, tm=1024, tn=1024):
    M, K = a.shape
    N = w.shape[1]
    tm, tn = _tile(M, tm), _tile(N, tn)
    return pl.pallas_call(
        _mm_kernel,
        grid=(M // tm, N // tn),
        in_specs=[pl.BlockSpec((tm, K), lambda i, j: (i, 0)),
                  pl.BlockSpec((K, tn), lambda i, j: (0, j))],
        out_specs=pl.BlockSpec((tm, tn), lambda i, j: (i, j)),
        out_shape=jax.ShapeDtypeStruct((M, N), F32),
        scratch_shapes=[pltpu.VMEM((tm, K), BF16)],
        compiler_params=_params("parallel", "arbitrary"),
        name="matmul",
    )(a, w)


def _proj_ln_kernel(*refs, n_a, alpha, gated):
    a_refs, rest = refs[:n_a], list(refs[n_a:])
    gate_ref = rest.pop(0) if gated else None
    w_ref, x_ref, g_ref, b_ref, o_ref = rest[:5]

    def term(a_ref):
        a = a_ref[...]
        if gated:
            a = a * gate_ref[...]
        return jnp.dot(a.astype(BF16), w_ref[...], preferred_element_type=F32)

    def finish(acc):
        o_ref[...] = _layer_norm(alpha * x_ref[...] + acc, g_ref[...], b_ref[...])

    if n_a == 1:
        finish(term(a_refs[0]))
    else:
        acc_ref = rest[5]
        k = pl.program_id(1)

        @pl.when(k == 0)
        def _():
            acc_ref[...] = term(a_refs[0])

        @pl.when(k == 1)
        def _():
            finish(acc_ref[...] + term(a_refs[1]))


PROJ_VMEM_BUDGET = 46 * 1024 * 1024


def proj_ln(a_list, gate, w, x, g, b, alpha):
    n_a = len(a_list)
    M, K = a_list[0].shape
    D = w.shape[1]
    assert n_a in (1, 2) and w.shape[0] == n_a * K
    gated = gate is not None

    def vmem_bytes(tm):
        blocks = sum(tm * K * a.dtype.itemsize for a in a_list) + gated * tm * K * 4
        blocks += K * D * w.dtype.itemsize + 2 * tm * D * 4
        return 2 * blocks + (n_a > 1) * tm * D * 4

    tm = next(t for t in (512, 256, 128, 64, 32, 16, 8) if M % t == 0 and vmem_bytes(t) <= PROJ_VMEM_BUDGET)
    a_spec = pl.BlockSpec((tm, K), lambda i, k: (i, 0))
    row_spec = pl.BlockSpec((tm, D), lambda i, k: (i, 0))
    vec_spec = pl.BlockSpec((1, D), lambda i, k: (0, 0))
    in_specs = [a_spec] * n_a + [a_spec] * gated
    in_specs += [pl.BlockSpec((K, D), lambda i, k: (k, 0)), row_spec, vec_spec, vec_spec]
    args = list(a_list) + [gate] * gated + [w, x, g.reshape(1, D), b.reshape(1, D)]
    return pl.pallas_call(
        functools.partial(_proj_ln_kernel, n_a=n_a, alpha=alpha, gated=gated),
        grid=(M // tm, n_a),
        in_specs=in_specs,
        out_specs=row_spec,
        out_shape=jax.ShapeDtypeStruct((M, D), F32),
        scratch_shapes=[pltpu.VMEM((tm, D), F32)] * (n_a > 1),
        compiler_params=_params("parallel", "arbitrary"),
        name="proj_ln",
    )(*args)


def _mlp_ln_kernel(x_ref, w1_ref, w2_ref, g_ref, b_ref, o_ref, xb_ref, *, nf, alpha):
    f = pl.program_id(1)

    @pl.when(f == 0)
    def _():
        o_ref[...] = jnp.zeros_like(o_ref)
        xb_ref[...] = x_ref[...].astype(BF16)

    h = jnp.maximum(jnp.dot(xb_ref[...], w1_ref[0], preferred_element_type=F32), 0.0)
    o_ref[...] += jnp.dot((h * h).astype(BF16), w2_ref[0], preferred_element_type=F32)

    @pl.when(f == nf - 1)
    def _():
        y = alpha * x_ref[...] + o_ref[...]
        o_ref[...] = _layer_norm(y, g_ref[...], b_ref[...])


def mlp_ln(x, w1, w2, layer, g, b, alpha, tm=1024, tf=512):
    M, D = x.shape
    FF = w1.shape[2]
    tm, tf = _tile(M, tm), _tile(FF, tf)
    nf = FF // tf
    row_spec = pl.BlockSpec((tm, D), lambda i, f: (i, 0))
    vec_spec = pl.BlockSpec((1, D), lambda i, f: (0, 0))
    return pl.pallas_call(
        functools.partial(_mlp_ln_kernel, nf=nf, alpha=alpha),
        grid=(M // tm, nf),
        in_specs=[row_spec,
                  pl.BlockSpec((1, D, tf), lambda i, f: (layer, 0, f)),
                  pl.BlockSpec((1, tf, D), lambda i, f: (layer, f, 0)),
                  vec_spec, vec_spec],
        out_specs=row_spec,
        out_shape=jax.ShapeDtypeStruct((M, D), F32),
        scratch_shapes=[pltpu.VMEM((tm, D), BF16)],
        compiler_params=_params("parallel", "arbitrary"),
        name="mlp_ln",
    )(x, w1, w2, g.reshape(1, D), b.reshape(1, D))


def _mlstm_kernel(q_ref, k_ref, v_ref, o_ref, gate_ref, gbias_ref, nw_ref, c0_ref, n0_ref, m0_ref,
                  h_ref, c1_ref, n1_ref, m1_ref, *, heads, dk, dv, L, n_chunks, nt):
    t = pl.program_id(1)

    @pl.when(t == 0)
    def _():
        c1_ref[...] = c0_ref[...]
        n1_ref[...] = n0_ref[...]
        m1_ref[...] = m0_ref[...]

    row = lax.broadcasted_iota(jnp.int32, (L, L), 0)
    col = lax.broadcasted_iota(jnp.int32, (L, L), 1)
    causal = row >= col
    tril = causal.astype(F32)
    sel_r = lax.broadcasted_iota(jnp.int32, (2 * heads, LANES), 0)
    sel_c = lax.broadcasted_iota(jnp.int32, (2 * heads, LANES), 1)
    pick = (sel_r == sel_c).astype(F32)
    lane = lax.broadcasted_iota(jnp.int32, (L, LANES), 1)
    scale = dk ** -0.5

    for c in range(n_chunks):
        r0 = c * L
        gt = gate_ref[0, r0:r0 + L, :] + gbias_ref[...]
        logf = jnp.minimum(gt, 0.0) - jnp.log1p(jnp.exp(-jnp.abs(gt)))
        gl = jnp.where(lane < heads, gt, logf)
        cum = jnp.dot(tril, gl, preferred_element_type=F32, precision=HIGHEST)
        nt_dims = (((1,), (1,)), ((), ()))
        gl_t = lax.dot_general(pick, gl, nt_dims, preferred_element_type=F32, precision=HIGHEST)
        cum_t = lax.dot_general(pick, cum, nt_dims, preferred_element_type=F32, precision=HIGHEST)
        for h in range(heads):
            ig_col = gl[:, h:h + 1]
            b_col = cum[:, heads + h:heads + h + 1]
            ig_row = gl_t[h:h + 1, :]
            b_row = cum_t[heads + h:heads + h + 1, :]
            b_last = b_row[:, L - 1:L]
            m_prev = m1_ref[0, :, h:h + 1]
            qh = (q_ref[0, r0:r0 + L, h * dk:(h + 1) * dk] * scale).astype(BF16)
            kf = k_ref[0, r0:r0 + L, h * dk:(h + 1) * dk]
            kh = kf.astype(BF16)
            vh = v_ref[0, r0:r0 + L, h * dv:(h + 1) * dv].astype(BF16)
            c_prev = c1_ref[0, h]
            n_prev = n1_ref[0, h:h + 1, :]

            dmat = jnp.where(causal, b_col - b_row + ig_row, -jnp.inf)
            inter = b_col + m_prev
            m_t = jnp.maximum(inter, jnp.max(dmat, -1, keepdims=True))
            w_intra = jnp.exp(dmat - m_t)
            w_inter = jnp.exp(inter - m_t)
            s = lax.dot_general(qh, kh, nt_dims, preferred_element_type=F32)
            qk = s * w_intra
            num = jnp.dot(qk.astype(BF16), vh, preferred_element_type=F32)
            num = num + w_inter * jnp.dot(qh, c_prev.astype(BF16), preferred_element_type=F32)
            qn = jnp.sum(qh.astype(F32) * n_prev, -1, keepdims=True)
            den = jnp.sum(qk, -1, keepdims=True) + w_inter * qn
            hh = num / jnp.maximum(jnp.abs(den), jnp.exp(-m_t))

            mu = jnp.mean(hh, -1, keepdims=True)
            d = hh - mu
            var = jnp.mean(d * d, -1, keepdims=True)
            hn = d * lax.rsqrt(var + LN_EPS)
            og = o_ref[0, r0:r0 + L, h * dv:(h + 1) * dv]
            hg = hn * nw_ref[:, h * dv:(h + 1) * dv] * _sigmoid(og)
            h_ref[0, r0:r0 + L, h * dv:(h + 1) * dv] = hg.astype(h_ref.dtype)

            lw_col = b_last - b_col + ig_col
            lw_row = b_last - b_row + ig_row
            m_new = jnp.maximum(b_last + m_prev, jnp.max(lw_row, -1, keepdims=True))
            ws_col = jnp.exp(lw_col - m_new)
            wc = jnp.exp(b_last + m_prev - m_new)
            kw = kf * ws_col
            tn_dims = (((0,), (0,)), ((), ()))
            c1_ref[0, h] = wc * c_prev + lax.dot_general(kw.astype(BF16), vh, tn_dims,
                                                         preferred_element_type=F32)
            n1_ref[0, h:h + 1, :] = wc * n_prev + jnp.sum(kw, 0, keepdims=True)
            m1_ref[0, :, h:h + 1] = m_new


def mlstm(u_qkvo, gates, gbias, norm_w, c0, n0, m0, heads, dk, dv):
    B, T, _ = u_qkvo.shape
    L = min(T, CHUNK)
    assert T % L == 0
    tb = L
    n_chunks = tb // L
    nt = T // tb
    qw, vw = heads * dk, heads * dv
    assert qw % LANES == 0 and vw == 2 * qw
    st = lambda i, t: (i, 0, 0)
    kern = functools.partial(_mlstm_kernel, heads=heads, dk=dk, dv=dv, L=L, n_chunks=n_chunks, nt=nt)
    return pl.pallas_call(
        kern,
        grid=(B, nt),
        in_specs=[pl.BlockSpec((1, tb, qw), lambda i, t: (i, t, 0)),
                  pl.BlockSpec((1, tb, qw), lambda i, t: (i, t, 1)),
                  pl.BlockSpec((1, tb, vw), lambda i, t: (i, t, 1)),
                  pl.BlockSpec((1, tb, vw), lambda i, t: (i, t, 2)),
                  pl.BlockSpec((1, tb, LANES), lambda i, t: (i, t, 0)),
                  pl.BlockSpec((1, LANES), lambda i, t: (0, 0)),
                  pl.BlockSpec((1, vw), lambda i, t: (0, 0)),
                  pl.BlockSpec((1, heads, dk, dv), lambda i, t: (i, 0, 0, 0)),
                  pl.BlockSpec((1, heads, dk), st),
                  pl.BlockSpec((1, 1, heads), st)],
        out_specs=[pl.BlockSpec((1, tb, vw), lambda i, t: (i, t, 0)),
                   pl.BlockSpec((1, heads, dk, dv), lambda i, t: (i, 0, 0, 0)),
                   pl.BlockSpec((1, heads, dk), st),
                   pl.BlockSpec((1, 1, heads), st)],
        out_shape=[jax.ShapeDtypeStruct((B, T, vw), BF16),
                   jax.ShapeDtypeStruct((B, heads, dk, dv), F32),
                   jax.ShapeDtypeStruct((B, heads, dk), F32),
                   jax.ShapeDtypeStruct((B, 1, heads), F32)],
        compiler_params=_params("parallel", "arbitrary"),
        name="mlstm",
    )(u_qkvo, u_qkvo, u_qkvo, u_qkvo, gates, gbias, norm_w, c0, n0, m0)


def _gelu_tanh(x):
    return 0.5 * x * (1.0 + jnp.tanh(0.7978845608028654 * (x + 0.044715 * x * x * x)))


def _lru_kernel(xr_ref, yg_ref, conv0_ref, h0_ref, cw_ref, cb_ref, wa_ref, ba_ref, wx_ref, bx_ref, lam_ref,
                y_ref, conv1_ref, h1_ref, xp_ref, a_ref, b_ref, *, tb, blocks, cw, reset_first):
    t = pl.program_id(1)
    halo = cw - 1

    @pl.when(t == 0)
    def _():
        xp_ref[...] = jnp.zeros_like(xp_ref)
        xp_ref[SUBLANES - halo:, :] = conv0_ref[0]
        h1_ref[0] = h0_ref[0]

    x = xr_ref[0]
    prev = xp_ref[...]
    head_rows = lax.broadcasted_iota(jnp.int32, prev.shape, 0)
    xc = cb_ref[...] + x * cw_ref[cw - 1:cw, :]
    for d in range(1, cw):
        xs = pltpu.roll(x, d, 0)
        first = jnp.where(head_rows < d, pltpu.roll(prev, d, 0), xs[:SUBLANES])
        xs = jnp.concatenate([first, xs[SUBLANES:]], 0)
        xc = xc + xs * cw_ref[cw - 1 - d:cw - d, :]

    sp = _softplus(-lam_ref[...])
    bw = xc.shape[1] // blocks
    for g in range(blocks):
        sl = slice(g * bw, (g + 1) * bw)
        xg = xc[:, sl]
        xgb = xg.astype(BF16)
        gr = _sigmoid(jnp.dot(xgb, wa_ref[g], preferred_element_type=F32) + ba_ref[:, sl])
        gi = _sigmoid(jnp.dot(xgb, wx_ref[g], preferred_element_type=F32) + bx_ref[:, sl])
        log_a = -LRU_C * gr * sp[:, sl]
        th = jnp.tanh(log_a)
        z = -2.0 * th / (1.0 - th)
        mult = jnp.where(z > 0.0, z * lax.rsqrt(z), 0.0)
        if reset_first:
            first = (lax.broadcasted_iota(jnp.int32, mult.shape, 0) == 0) & (t == 0)
            mult = jnp.where(first, 1.0, mult)
        a_ref[:, sl] = jnp.exp(log_a)
        b_ref[:, sl] = mult * gi * xg

    def step(i, h):
        h = a_ref[pl.ds(i, 1), :] * h + b_ref[pl.ds(i, 1), :]
        b_ref[pl.ds(i, 1), :] = h
        return h

    h_last = lax.fori_loop(0, tb, step, h1_ref[0], unroll=8)
    h1_ref[0] = h_last
    y_ref[0] = (b_ref[...] * _gelu_tanh(yg_ref[0])).astype(y_ref.dtype)
    last = xr_ref[0, tb - SUBLANES:, :]
    xp_ref[...] = last
    conv1_ref[0] = last[SUBLANES - halo:, :]


def lru(u_xy, conv0, h0, conv_w, conv_b, wa, ba, wx, bx, lam, reset_first):
    B, T, W2 = u_xy.shape
    W = W2 // 2
    cw = conv_w.shape[0]
    blocks = wa.shape[0]
    tb = _tile(T, 256)
    assert tb >= cw - 1 and cw - 1 <= SUBLANES
    st = lambda i, t: (i, 0, 0)
    vec = pl.BlockSpec((1, W), lambda i, t: (0, 0))
    wsp = pl.BlockSpec(wa.shape, lambda i, t: (0, 0, 0))
    kern = functools.partial(_lru_kernel, tb=tb, blocks=blocks, cw=cw, reset_first=reset_first)
    return pl.pallas_call(
        kern,
        grid=(B, T // tb),
        in_specs=[pl.BlockSpec((1, tb, W), lambda i, t: (i, t, 0)),
                  pl.BlockSpec((1, tb, W), lambda i, t: (i, t, 1)),
                  pl.BlockSpec((1, cw - 1, W), st),
                  pl.BlockSpec((1, 1, W), st),
                  pl.BlockSpec((cw, W), lambda i, t: (0, 0)),
                  vec, wsp, vec, wsp, vec, vec],
        out_specs=[pl.BlockSpec((1, tb, W), lambda i, t: (i, t, 0)),
                   pl.BlockSpec((1, cw - 1, W), st),
                   pl.BlockSpec((1, 1, W), st)],
        out_shape=[jax.ShapeDtypeStruct((B, T, W), BF16),
                   jax.ShapeDtypeStruct((B, cw - 1, W), F32),
                   jax.ShapeDtypeStruct((B, 1, W), F32)],
        scratch_shapes=[pltpu.VMEM((SUBLANES, W), F32),
                        pltpu.VMEM((tb, W), F32),
                        pltpu.VMEM((tb, W), F32)],
        compiler_params=_params("parallel", "arbitrary"),
        name="lru",
    )(u_xy, u_xy, conv0, h0, conv_w, conv_b, wa, ba, wx, bx, lam)


def _lora_math(x, w1_ref, w2_ref, bias_ref, mid, post):
    z = jnp.dot(x, w1_ref[...], preferred_element_type=F32)
    if mid == "tanh":
        z = jnp.tanh(z)
    elif mid == "sigmoid":
        z = _sigmoid(z)
    y = jnp.dot(z.astype(BF16), w2_ref[...], preferred_element_type=F32)
    if post == "decay":
        y = jnp.exp(-EXP_NEG_HALF * _sigmoid(bias_ref[...] + y))
    elif post == "sigmoid":
        y = _sigmoid(bias_ref[...] + y)
    return y


def _rwkv_scan_kernel(r_ref, k_ref, v_ref, w_ref, a_ref, kk_p, ka_p, rk_p, gg_p, gb_p, s0_ref,
                      y_ref, s1_ref, nkk_s, kka_s, km_s, vt_s, ys_s, *, tt, n):
    t = pl.program_id(1)
    nb = n // SUBLANES

    @pl.when(t == 0)
    def _():
        s1_ref[...] = s0_ref[...]

    def prow(p_ref, f):
        return p_ref[f:f + 1, :]

    nrm = jnp.zeros((tt, LANES), F32)
    for f in range(n):
        kk = k_ref[0, f] * prow(kk_p, f)
        nrm = nrm + kk * kk
    inv = lax.rsqrt(jnp.maximum(nrm, 1e-24))
    for f in range(n):
        kf, af = k_ref[0, f], a_ref[0, f]
        kk = kf * prow(kk_p, f) * inv
        nkk_s[f] = -kk
        kka_s[f] = kk * af
        km_s[f] = kf * (1.0 + (af - 1.0) * prow(ka_p, f))

    vt_s[...] = jnp.swapaxes(v_ref[0], 0, 1)
    zeros = tuple(jnp.zeros((SUBLANES, LANES), F32) for _ in range(nb))

    def reduce_keys(kx, acc):
        nk = nkk_s[kx, pl.ds(0, 1), :]
        return tuple(acc[jb] + s1_ref[0, jb, kx] * nk for jb in range(nb))

    sa0 = lax.fori_loop(0, n, reduce_keys, zeros, unroll=8)

    def time_step(i, sa):
        row = pl.ds(i, 1)
        nxt = pl.ds(jnp.minimum(i + 1, tt - 1), 1)
        vt = [vt_s[i, jb * SUBLANES:(jb + 1) * SUBLANES, :] for jb in range(nb)]

        def update_keys(kx, acc):
            wr = w_ref[0, kx, row, :]
            ar = kka_s[kx, row, :]
            mr = km_s[kx, row, :]
            rr = r_ref[0, kx, row, :]
            nk = nkk_s[kx, nxt, :]
            ys, sn = [], []
            for jb in range(nb):
                s = s1_ref[0, jb, kx] * wr + sa[jb] * ar + vt[jb] * mr
                s1_ref[0, jb, kx] = s
                ys.append(acc[jb] + s * rr)
                sn.append(acc[nb + jb] + s * nk)
            return tuple(ys + sn)

        acc = lax.fori_loop(0, n, update_keys, zeros + zeros, unroll=8)
        for jb in range(nb):
            ys_s[i, jb * SUBLANES:(jb + 1) * SUBLANES, :] = acc[jb]
        return tuple(acc[nb:])

    lax.fori_loop(0, tt, time_step, sa0)
    y_ref[0] = jnp.swapaxes(ys_s[...], 0, 1)

    mu = jnp.zeros((tt, LANES), F32)
    cb = jnp.zeros((tt, LANES), F32)
    for f in range(n):
        mu = mu + y_ref[0, f]
        cb = cb + r_ref[0, f] * km_s[f] * prow(rk_p, f)
    mu = mu * (1.0 / n)
    var = jnp.zeros((tt, LANES), F32)
    for f in range(n):
        d = y_ref[0, f] - mu
        var = var + d * d
    rs = lax.rsqrt(var * (1.0 / n) + RWKV_GN_EPS)
    for f in range(n):
        y_ref[0, f] = (y_ref[0, f] - mu) * rs * prow(gg_p, f) + prow(gb_p, f) + cb * v_ref[0, f]


def rwkv_scan(r, k, v, w, a, kk_p, ka_p, rk_p, gg_p, gb_p, s0):
    G, n, T, _ = r.shape
    tt = _tile(T, 32)
    seq = pl.BlockSpec((1, n, tt, LANES), lambda g, t: (g, 0, t, 0))
    par = pl.BlockSpec((n, LANES), lambda g, t: (0, 0))
    st = pl.BlockSpec((1, n // SUBLANES, n, SUBLANES, LANES), lambda g, t: (g, 0, 0, 0, 0))
    return pl.pallas_call(
        functools.partial(_rwkv_scan_kernel, tt=tt, n=n),
        grid=(G, T // tt),
        in_specs=[seq] * 5 + [par] * 5 + [st],
        out_specs=[seq, st],
        out_shape=[jax.ShapeDtypeStruct((G, n, T, LANES), F32),
                   jax.ShapeDtypeStruct(s0.shape, F32)],
        scratch_shapes=[pltpu.VMEM((n, tt, LANES), F32)] * 3 + [pltpu.VMEM((tt, n, LANES), F32)] * 2,
        compiler_params=_params("parallel", "arbitrary"),
        name="rwkv_scan",
    )(r, k, v, w, a, kk_p, ka_p, rk_p, gg_p, gb_p, s0)


def _store_scan_layout(res, z_ref, heads, c0=0):
    bl = LANES // heads
    for c in range(res.shape[1] // LANES):
        sub = [res[s * LANES:(s + 1) * LANES, c * LANES:(c + 1) * LANES].T for s in range(bl)]
        for ni in range(bl):
            tile = jnp.concatenate([sub[s][ni * heads:(ni + 1) * heads, :] for s in range(bl)], 0)
            z_ref[0, (c0 + c) * bl + ni] = tile.T


MXU_WIDTH = 256


def _mixed_tokens(x_ref, prev_ref, shift_ref, mu_ref):
    x = x_ref[...]
    bl, tq, D = x.shape
    prev = jnp.where(pl.program_id(1) == 0, shift_ref[...], prev_ref[:, SUBLANES - 1:, :])
    xs = pltpu.roll(x, 1, 1)
    head_rows = lax.broadcasted_iota(jnp.int32, (bl, SUBLANES, D), 1)
    first = jnp.where(head_rows == 0, prev, xs[:, :SUBLANES, :])
    x_prev = jnp.concatenate([first, xs[:, SUBLANES:, :]], 1)
    xm = x + (x_prev - x) * mu_ref[0]
    return xm.astype(BF16).reshape(bl * tq, D)


def _mix_proj_kernel(x_ref, prev_ref, shift_ref, mu_ref, *refs, heads, lora_args, scan_out):
    o_ref = refs[-1]
    bl, tq, _ = x_ref.shape
    a = _mixed_tokens(x_ref, prev_ref, shift_ref, mu_ref)
    if lora_args is not None:
        res = _lora_math(a, *refs[:3], *lora_args)
        if scan_out:
            _store_scan_layout(res, o_ref, heads)
        else:
            o_ref[...] = res.reshape(bl, tq, res.shape[1])
    elif scan_out:
        w_ref = refs[0]
        for c in range(w_ref.shape[1] // MXU_WIDTH):
            cols = slice(c * MXU_WIDTH, (c + 1) * MXU_WIDTH)
            res = jnp.dot(a, w_ref[:, cols], preferred_element_type=F32)
            _store_scan_layout(res, o_ref, heads, c * (MXU_WIDTH // LANES))
    else:
        res = jnp.dot(a, refs[0][...], preferred_element_type=F32)
        o_ref[...] = res.reshape(bl, tq, res.shape[1])


def mix_proj(x, shift, mu, j, weights, heads, scan_out, lora_args=None):
    B, T, D = x.shape
    N = weights[-1].shape[1]
    bl = LANES // heads
    tq = min(T, LANES)
    assert T % tq == 0 and tq % SUBLANES == 0 and not (scan_out and tq != LANES)
    steps = tq // SUBLANES
    const = lambda w: pl.BlockSpec(w.shape, lambda g, t: (0,) * w.ndim, pipeline_mode=pl.Buffered(1))
    if scan_out:
        out_spec = pl.BlockSpec((1, N // heads, LANES, LANES), lambda g, t: (g, 0, t, 0))
        out_shape = jax.ShapeDtypeStruct((B // bl, N // heads, T, LANES), F32)
    else:
        out_spec = pl.BlockSpec((bl, tq, N), lambda g, t: (g, t, 0))
        out_shape = jax.ShapeDtypeStruct((B, T, N), F32)
    return pl.pallas_call(
        functools.partial(_mix_proj_kernel, heads=heads, lora_args=lora_args, scan_out=scan_out),
        grid=(B // bl, T // tq),
        in_specs=[pl.BlockSpec((bl, tq, D), lambda g, t: (g, t, 0)),
                  pl.BlockSpec((bl, SUBLANES, D), lambda g, t: (g, jnp.maximum(t * steps - 1, 0), 0)),
                  pl.BlockSpec((bl, 1, D), lambda g, t: (g, 0, 0)),
                  pl.BlockSpec((1, 1, D), lambda g, t: (j, 0, 0))] + [const(w) for w in weights],
        out_specs=out_spec,
        out_shape=out_shape,
        compiler_params=_params("parallel", "arbitrary"),
        name="mix_proj",
    )(x, x, shift, mu.reshape(mu.shape[0], 1, D), *weights)


def _scan_proj_ln_kernel(y_ref, gate_ref, w_ref, x_ref, g_ref, b_ref, o_ref, *, heads, alpha):
    bl = LANES // heads
    D = o_ref.shape[2]
    per = MXU_WIDTH // LANES
    acc = None
    for kc in range(D // MXU_WIDTH):
        parts = []
        for c in range(kc * per, (kc + 1) * per):
            sub = [y_ref[0, c * bl + ni].T for ni in range(bl)]
            cols = slice(c * LANES, (c + 1) * LANES)
            tiles = []
            for s in range(bl):
                tile = jnp.concatenate([sub[ni][s * heads:(s + 1) * heads, :] for ni in range(bl)], 0)
                tiles.append((tile.T * gate_ref[s, :, cols]).astype(BF16))
            parts.append(jnp.concatenate(tiles, 0))
        a = jnp.concatenate(parts, 1)
        part = jnp.dot(a, w_ref[kc * MXU_WIDTH:(kc + 1) * MXU_WIDTH, :], preferred_element_type=F32)
        acc = part if acc is None else acc + part
    y = alpha * x_ref[...].reshape(bl * LANES, D) + acc
    o_ref[...] = _layer_norm(y, g_ref[...], b_ref[...]).reshape(bl, LANES, D)


def scan_proj_ln(y, gate, w, x, g, b, alpha, heads):
    G, n, T, _ = y.shape
    B, _, D = x.shape
    bl = LANES // heads
    tok = pl.BlockSpec((bl, LANES, D), lambda g_, t: (g_, t, 0))
    vec = pl.BlockSpec((1, D), lambda g_, t: (0, 0))
    return pl.pallas_call(
        functools.partial(_scan_proj_ln_kernel, heads=heads, alpha=alpha),
        grid=(G, T // LANES),
        in_specs=[pl.BlockSpec((1, n, LANES, LANES), lambda g_, t: (g_, 0, t, 0)), tok,
                  pl.BlockSpec((D, D), lambda g_, t: (0, 0), pipeline_mode=pl.Buffered(1)),
                  tok, vec, vec],
        out_specs=tok,
        out_shape=jax.ShapeDtypeStruct((B, T, D), F32),
        compiler_params=_params("parallel", "parallel"),
        name="scan_proj_ln",
    )(y, gate, w, x, g.reshape(1, D), b.reshape(1, D))


def _pad_cols(w, n):
    return jnp.pad(w, ((0, 0), (0, n - w.shape[1])))


def _pad_rows(w, n):
    return jnp.pad(w, ((0, n - w.shape[0]), (0, 0)))


def _layer_a(x, st, p, li, reset_first, alpha):
    c0, n0, m0, conv0, h0 = st
    B, T, D = x.shape
    heads = p['a_b_ig'].shape[1]
    dv = p['a_mlstm_norm'].shape[1] // heads
    dk = dv // 2
    qw, vw = heads * dk, heads * dv
    W = p['a_conv_w'].shape[2]
    w_in = p['a_w_in'][li]
    n_qkvo = 2 * qw + 2 * vw
    w_qkvo = w_in[:, :n_qkvo].astype(BF16)
    w_gate = _pad_cols(w_in[:, n_qkvo:n_qkvo + 2 * heads], LANES).astype(BF16)
    w_xy = w_in[:, n_qkvo + 2 * heads:].astype(BF16)
    x2 = x.reshape(B * T, D)
    u_qkvo = matmul(x2, w_qkvo).reshape(B, T, n_qkvo)
    gates = matmul(x2, w_gate).reshape(B, T, LANES)
    u_xy = matmul(x2, w_xy).reshape(B, T, 2 * W)
    gbias = _pad_cols(jnp.concatenate([p['a_b_ig'][li], p['a_b_fg'][li]])[None, :], LANES)
    hm, c1, n1, m1 = mlstm(u_qkvo, gates, gbias, p['a_mlstm_norm'][li][None, :],
                           c0, n0, m0.reshape(B, 1, heads), heads, dk, dv)
    yb, conv1, h1 = lru(u_xy, conv0, h0.reshape(B, 1, W), p['a_conv_w'][li], p['a_conv_b'][li][None, :],
                        p['a_lru_wa'][li].astype(BF16), p['a_lru_ba'][li][None, :],
                        p['a_lru_wx'][li].astype(BF16), p['a_lru_bx'][li][None, :],
                        p['a_lru_lambda'][li][None, :], reset_first)
    assert vw == W
    y = proj_ln([hm.reshape(B * T, vw), yb.reshape(B * T, W)], None, p['a_w_out'][li].astype(BF16), x2,
                p['ln1_g'][2 * li], p['ln1_b'][2 * li], alpha)
    return y.reshape(B, T, D), (c1, n1, m1.reshape(B, heads), conv1, h1.reshape(B, W))


def _layer_c(x, shift, s0, p, li, layer, alpha):
    B, T, D = x.shape
    H, N = p['c_r_k'].shape[1:]
    M = B * T
    bl = LANES // H
    G = B // bl
    nb = N // SUBLANES
    def cols(w):
        return w.reshape(w.shape[0], H, N).swapaxes(1, 2).reshape(w.shape[0], D)

    w_r = cols(p['c_w_r'][li].astype(BF16))
    w_k = cols(p['c_w_k'][li].astype(BF16))
    w_v = cols(p['c_w_v'][li].astype(BF16))
    rd = -(-p['c_w1'].shape[2] // LANES) * LANES
    ra = -(-p['c_a1'].shape[2] // LANES) * LANES
    w_d = (_pad_cols(p['c_w1'][li], rd).astype(BF16), _pad_rows(cols(p['c_w2'][li]), rd).astype(BF16),
           cols(p['c_w0'][li][None, :]))
    w_a = (_pad_cols(p['c_a1'][li], ra).astype(BF16), _pad_rows(cols(p['c_a2'][li]), ra).astype(BF16),
           cols(p['c_a0'][li][None, :]))
    w_g = (p['c_g1'][li].astype(BF16), cols(p['c_g2'][li].astype(BF16)), jnp.zeros((1, D), F32))
    fused = T % LANES == 0
    mix = functools.partial(mix_proj, x, shift.reshape(B, 1, D), p['c_mu'][li])
    r = mix(0, (w_r,), H, fused)
    k = mix(2, (w_k,), H, fused)
    v = mix(3, (w_v,), H, fused)
    decay = mix(1, w_d, H, fused, ("tanh", "decay"))
    a = mix(4, w_a, H, fused, ("none", "sigmoid"))
    g = mix(5, w_g, H, False, ("sigmoid", "none"))
    if not fused:
        def to_scan(z):
            return z.reshape(G, bl, T, N, H).transpose(0, 3, 2, 1, 4).reshape(G, N, T, LANES)

        r, k, v, decay, a = (to_scan(z) for z in (r, k, v, decay, a))

    def par(z):
        return jnp.tile(z.reshape(H, N).T, (1, bl))

    s0t = s0.reshape(G, bl, H, nb, SUBLANES, N).transpose(0, 3, 5, 4, 1, 2).reshape(G, nb, N, SUBLANES, LANES)
    yt, s1t = rwkv_scan(r, k, v, decay, a,
                        par(p['c_k_k'][li]), par(p['c_k_a'][li]), par(p['c_r_k'][li].reshape(D)),
                        par(p['c_gn_g'][li]), par(p['c_gn_b'][li]), s0t)
    s1 = s1t.reshape(G, nb, N, SUBLANES, bl, H).transpose(0, 4, 5, 1, 3, 2).reshape(B, H, N, N)
    w_o = p['c_w_o'][li].astype(BF16).reshape(H, N, D).swapaxes(0, 1).reshape(D, D)
    if fused:
        out = scan_proj_ln(yt, g, w_o, x, p['ln1_g'][layer], p['ln1_b'][layer], alpha, H)
    else:
        y = yt.reshape(G, N, T, bl, H).transpose(0, 3, 2, 1, 4).reshape(M, D)
        out = proj_ln([y], g.reshape(M, D), w_o, x.reshape(M, D), p['ln1_g'][layer], p['ln1_b'][layer], alpha)
    return out.reshape(B, T, D), (x[:, -1], s1)


def _trunk(x, states, p, reset_first):
    mC, mn, mm, cv, hl, sh, S = states
    depth = p['ln1_g'].shape[0]
    alpha = (2 * depth) ** 0.25
    B, T, D = x.shape
    new_a, new_c = [], []
    for layer in range(depth):
        li = layer // 2
        if layer % 2 == 0:
            x, st = _layer_a(x, (mC[li], mn[li], mm[li], cv[li], hl[li]), p, li, reset_first, alpha)
            new_a.append(st)
        else:
            x, st = _layer_c(x, sh[li], S[li], p, li, layer, alpha)
            new_c.append(st)
        x = mlp_ln(x.reshape(B * T, D), p['mlp_w1'].astype(BF16), p['mlp_w2'].astype(BF16), layer,
                   p['ln2_g'][layer], p['ln2_b'][layer], alpha).reshape(B, T, D)
    sa = [jnp.stack([s[j] for s in new_a]) for j in range(5)]
    sc = [jnp.stack([s[j] for s in new_c]) for j in range(2)]
    return x, sa + sc


def kernel(x_prompt, x_sample, state_mlstm_C, state_mlstm_n, state_mlstm_m, state_lru_conv, state_lru_h,
           state_rwkv_shift, state_rwkv_S, a_w_in, a_b_ig, a_b_fg, a_mlstm_norm, a_conv_w, a_conv_b,
           a_lru_wa, a_lru_ba, a_lru_wx, a_lru_bx, a_lru_lambda, a_w_out, c_mu, c_w_r, c_w_k, c_w_v,
           c_w0, c_w1, c_w2, c_a0, c_a1, c_a2, c_g1, c_g2, c_k_k, c_k_a, c_r_k, c_gn_g, c_gn_b, c_w_o,
           ln1_g, ln1_b, ln2_g, ln2_b, mlp_w1, mlp_w2):
    p = dict(a_w_in=a_w_in, a_b_ig=a_b_ig, a_b_fg=a_b_fg, a_mlstm_norm=a_mlstm_norm, a_conv_w=a_conv_w,
             a_conv_b=a_conv_b, a_lru_wa=a_lru_wa, a_lru_ba=a_lru_ba, a_lru_wx=a_lru_wx, a_lru_bx=a_lru_bx,
             a_lru_lambda=a_lru_lambda, a_w_out=a_w_out, c_mu=c_mu, c_w_r=c_w_r, c_w_k=c_w_k, c_w_v=c_w_v,
             c_w0=c_w0, c_w1=c_w1, c_w2=c_w2, c_a0=c_a0, c_a1=c_a1, c_a2=c_a2, c_g1=c_g1, c_g2=c_g2,
             c_k_k=c_k_k, c_k_a=c_k_a, c_r_k=c_r_k, c_gn_g=c_gn_g, c_gn_b=c_gn_b, c_w_o=c_w_o,
             ln1_g=ln1_g, ln1_b=ln1_b, ln2_g=ln2_g, ln2_b=ln2_b, mlp_w1=mlp_w1, mlp_w2=mlp_w2)
    Bp = x_prompt.shape[0]
    init = tuple(jnp.zeros((s.shape[0], Bp) + s.shape[2:], s.dtype)
                 for s in (state_mlstm_C, state_mlstm_n, state_mlstm_m, state_lru_conv, state_lru_h,
                           state_rwkv_shift, state_rwkv_S))
    y_prompt, ps = _trunk(x_prompt, init, p, True)
    y_sample, ss = _trunk(x_sample, (state_mlstm_C, state_mlstm_n, state_mlstm_m, state_lru_conv,
                                     state_lru_h, state_rwkv_shift, state_rwkv_S), p, False)
    return (y_prompt, y_sample, *ps, *ss)
```

```python
import functools

import jax
import jax.numpy as jnp
from jax import lax
from jax.experimental import pallas as pl
from jax.experimental.pallas import tpu as pltpu

F32 = jnp.float32
BF16 = jnp.bfloat16

LANES = 128
SUBLANES = 8
VMEM_LIMIT_BYTES = 56 * 1024 * 1024

CHUNK = 256
LRU_C = 8.0
LN_EPS = 1e-5
RWKV_GN_EPS = 64e-5
HIGHEST = lax.Precision.HIGHEST
EXP_NEG_HALF = 0.6065306597126334


def _params(*sem):
    return pltpu.CompilerParams(dimension_semantics=sem, vmem_limit_bytes=VMEM_LIMIT_BYTES)


def _tile(n, pref):
    t = min(n, pref)
    while n % t:
        t -= 1
    return t


def _sigmoid(x):
    return 1.0 / (1.0 + jnp.exp(-x))


def _softplus(x):
    return jnp.maximum(x, 0.0) + jnp.log1p(jnp.exp(-jnp.abs(x)))


def _layer_norm(y, g, b):
    mu = jnp.mean(y, -1, keepdims=True)
    d = y - mu
    var = jnp.mean(d * d, -1, keepdims=True)
    return d * lax.rsqrt(var + LN_EPS) * g + b


def _mm_kernel(a_ref, w_ref, o_ref, ab_ref):
    @pl.when(pl.program_id(1) == 0)
    def _():
        ab_ref[...] = a_ref[...].astype(BF16)

    o_ref[...] = jnp.dot(ab_ref[...], w_ref[...], preferred_element_type=F32).astype(o_ref.dtype)


def matmul(a, w, out_dtype=F32, tm=1024, tn=1024):
    M, K = a.shape
    N = w.shape[1]
    tm, tn = _tile(M, tm), _tile(N, tn)
    return pl.pallas_call(
        _mm_kernel,
        grid=(M // tm, N // tn),
        in_specs=[pl.BlockSpec((tm, K), lambda i, j: (i, 0)),
                  pl.BlockSpec((K, tn), lambda i, j: (0, j))],
        out_specs=pl.BlockSpec((tm, tn), lambda i, j: (i, j)),
        out_shape=jax.ShapeDtypeStruct((M, N), out_dtype),
        scratch_shapes=[pltpu.VMEM((tm, K), BF16)],
        compiler_params=_params("parallel", "arbitrary"),
        name="matmul",
    )(a, w)


def _proj_ln_kernel(*refs, n_a, alpha, gated):
    a_refs, rest = refs[:n_a], list(refs[n_a:])
    gate_ref = rest.pop(0) if gated else None
    w_ref, x_ref, g_ref, b_ref, o_ref = rest[:5]

    def term(a_ref):
        a = a_ref[...]
        if gated:
            a = a * gate_ref[...]
        return jnp.dot(a.astype(BF16), w_ref[...], preferred_element_type=F32)

    def finish(acc):
        o_ref[...] = _layer_norm(alpha * x_ref[...] + acc, g_ref[...], b_ref[...])

    if n_a == 1:
        finish(term(a_refs[0]))
    else:
        acc_ref = rest[5]
        k = pl.program_id(1)

        @pl.when(k == 0)
        def _():
            acc_ref[...] = term(a_refs[0])

        @pl.when(k == 1)
        def _():
            finish(acc_ref[...] + term(a_refs[1]))


PROJ_VMEM_BUDGET = 46 * 1024 * 1024


def proj_ln(a_list, gate, w, x, g, b, alpha):
    n_a = len(a_list)
    M, K = a_list[0].shape
    D = w.shape[1]
    assert n_a in (1, 2) and w.shape[0] == n_a * K
    gated = gate is not None

    def vmem_bytes(tm):
        blocks = sum(tm * K * a.dtype.itemsize for a in a_list) + gated * tm * K * 4
        blocks += K * D * w.dtype.itemsize + 2 * tm * D * 4
        return 2 * blocks + (n_a > 1) * tm * D * 4

    tm = next(t for t in (512, 256, 128, 64, 32, 16, 8) if M % t == 0 and vmem_bytes(t) <= PROJ_VMEM_BUDGET)
    a_spec = pl.BlockSpec((tm, K), lambda i, k: (i, 0))
    row_spec = pl.BlockSpec((tm, D), lambda i, k: (i, 0))
    vec_spec = pl.BlockSpec((1, D), lambda i, k: (0, 0))
    in_specs = [a_spec] * n_a + [a_spec] * gated
    in_specs += [pl.BlockSpec((K, D), lambda i, k: (k, 0)), row_spec, vec_spec, vec_spec]
    args = list(a_list) + [gate] * gated + [w, x, g.reshape(1, D), b.reshape(1, D)]
    return pl.pallas_call(
        functools.partial(_proj_ln_kernel, n_a=n_a, alpha=alpha, gated=gated),
        grid=(M // tm, n_a),
        in_specs=in_specs,
        out_specs=row_spec,
        out_shape=jax.ShapeDtypeStruct((M, D), F32),
        scratch_shapes=[pltpu.VMEM((tm, D), F32)] * (n_a > 1),
        compiler_params=_params("parallel", "arbitrary"),
        name="proj_ln",
    )(*args)


def _mlp_ln_kernel(x_ref, w1_ref, w2_ref, g_ref, b_ref, o_ref, xb_ref, *, nf, alpha):
    f = pl.program_id(1)

    @pl.when(f == 0)
    def _():
        o_ref[...] = jnp.zeros_like(o_ref)
        xb_ref[...] = x_ref[...].astype(BF16)

    h = jnp.maximum(jnp.dot(xb_ref[...], w1_ref[0], preferred_element_type=F32), 0.0)
    o_ref[...] += jnp.dot((h * h).astype(BF16), w2_ref[0], preferred_element_type=F32)

    @pl.when(f == nf - 1)
    def _():
        y = alpha * x_ref[...] + o_ref[...]
        o_ref[...] = _layer_norm(y, g_ref[...], b_ref[...])


def mlp_ln(x, w1, w2, layer, g, b, alpha, tm=1024, tf=512):
    M, D = x.shape
    FF = w1.shape[2]
    tm, tf = _tile(M, tm), _tile(FF, tf)
    nf = FF // tf
    row_spec = pl.BlockSpec((tm, D), lambda i, f: (i, 0))
    vec_spec = pl.BlockSpec((1, D), lambda i, f: (0, 0))
    return pl.pallas_call(
        functools.partial(_mlp_ln_kernel, nf=nf, alpha=alpha),
        grid=(M // tm, nf),
        in_specs=[row_spec,
                  pl.BlockSpec((1, D, tf), lambda i, f: (layer, 0, f)),
                  pl.BlockSpec((1, tf, D), lambda i, f: (layer, f, 0)),
                  vec_spec, vec_spec],
        out_specs=row_spec,
        out_shape=jax.ShapeDtypeStruct((M, D), F32),
        scratch_shapes=[pltpu.VMEM((tm, D), BF16)],
        compiler_params=_params("parallel", "arbitrary"),
        name="mlp_ln",
    )(x, w1, w2, g.reshape(1, D), b.reshape(1, D))


def _mlstm_kernel(q_ref, k_ref, v_ref, o_ref, gate_ref, gbias_ref, nw_ref, c0_ref, n0_ref, m0_ref,
                  h_ref, c1_ref, n1_ref, m1_ref, *, heads, dk, dv, L, n_chunks, nt):
    t = pl.program_id(1)

    @pl.when(t == 0)
    def _():
        c1_ref[...] = c0_ref[...]
        n1_ref[...] = n0_ref[...]
        m1_ref[...] = m0_ref[...]

    row = lax.broadcasted_iota(jnp.int32, (L, L), 0)
    col = lax.broadcasted_iota(jnp.int32, (L, L), 1)
    causal = row >= col
    tril = causal.astype(F32)
    sel_r = lax.broadcasted_iota(jnp.int32, (2 * heads, LANES), 0)
    sel_c = lax.broadcasted_iota(jnp.int32, (2 * heads, LANES), 1)
    pick = (sel_r == sel_c).astype(F32)
    lane = lax.broadcasted_iota(jnp.int32, (L, LANES), 1)
    scale = dk ** -0.5

    for c in range(n_chunks):
        r0 = c * L
        gt = gate_ref[0, r0:r0 + L, :] + gbias_ref[...]
        logf = jnp.minimum(gt, 0.0) - jnp.log1p(jnp.exp(-jnp.abs(gt)))
        gl = jnp.where(lane < heads, gt, logf)
        cum = jnp.dot(tril, gl, preferred_element_type=F32, precision=HIGHEST)
        nt_dims = (((1,), (1,)), ((), ()))
        gl_t = lax.dot_general(pick, gl, nt_dims, preferred_element_type=F32, precision=HIGHEST)
        cum_t = lax.dot_general(pick, cum, nt_dims, preferred_element_type=F32, precision=HIGHEST)
        for h in range(heads):
            ig_col = gl[:, h:h + 1]
            b_col = cum[:, heads + h:heads + h + 1]
            ig_row = gl_t[h:h + 1, :]
            b_row = cum_t[heads + h:heads + h + 1, :]
            b_last = b_row[:, L - 1:L]
            m_prev = m1_ref[0, :, h:h + 1]
            qh = (q_ref[0, r0:r0 + L, h * dk:(h + 1) * dk].astype(F32) * scale).astype(BF16)
            kh = k_ref[0, r0:r0 + L, h * dk:(h + 1) * dk].astype(BF16)
            kf = kh.astype(F32)
            vh = v_ref[0, r0:r0 + L, h * dv:(h + 1) * dv].astype(BF16)
            c_prev = c1_ref[0, h]
            n_prev = n1_ref[0, h:h + 1, :]

            dmat = jnp.where(causal, b_col - b_row + ig_row, -jnp.inf)
            inter = b_col + m_prev
            m_t = jnp.maximum(inter, jnp.max(dmat, -1, keepdims=True))
            w_intra = jnp.exp(dmat - m_t)
            w_inter = jnp.exp(inter - m_t)
            s = lax.dot_general(qh, kh, nt_dims, preferred_element_type=F32)
            qk = s * w_intra
            num = jnp.dot(qk.astype(BF16), vh, preferred_element_type=F32)
            num = num + w_inter * jnp.dot(qh, c_prev.astype(BF16), preferred_element_type=F32)
            qn = jnp.sum(qh.astype(F32) * n_prev, -1, keepdims=True)
            den = jnp.sum(qk, -1, keepdims=True) + w_inter * qn
            hh = num / jnp.maximum(jnp.abs(den), jnp.exp(-m_t))

            mu = jnp.mean(hh, -1, keepdims=True)
            d = hh - mu
            var = jnp.mean(d * d, -1, keepdims=True)
            hn = d * lax.rsqrt(var + LN_EPS)
            og = o_ref[0, r0:r0 + L, h * dv:(h + 1) * dv].astype(F32)
            hg = hn * nw_ref[:, h * dv:(h + 1) * dv] * _sigmoid(og)
            h_ref[0, r0:r0 + L, h * dv:(h + 1) * dv] = hg.astype(h_ref.dtype)

            lw_col = b_last - b_col + ig_col
            lw_row = b_last - b_row + ig_row
            m_new = jnp.maximum(b_last + m_prev, jnp.max(lw_row, -1, keepdims=True))
            ws_col = jnp.exp(lw_col - m_new)
            wc = jnp.exp(b_last + m_prev - m_new)
            kw = kf * ws_col
            tn_dims = (((0,), (0,)), ((), ()))
            c1_ref[0, h] = wc * c_prev + lax.dot_general(kw.astype(BF16), vh, tn_dims,
                                                         preferred_element_type=F32)
            n1_ref[0, h:h + 1, :] = wc * n_prev + jnp.sum(kw, 0, keepdims=True)
            m1_ref[0, :, h:h + 1] = m_new


def mlstm(u_qkvo, gates, gbias, norm_w, c0, n0, m0, heads, dk, dv):
    B, T, _ = u_qkvo.shape
    L = min(T, CHUNK)
    assert T % L == 0
    tb = L
    n_chunks = tb // L
    nt = T // tb
    qw, vw = heads * dk, heads * dv
    assert qw % LANES == 0 and vw == 2 * qw
    st = lambda i, t: (i, 0, 0)
    kern = functools.partial(_mlstm_kernel, heads=heads, dk=dk, dv=dv, L=L, n_chunks=n_chunks, nt=nt)
    return pl.pallas_call(
        kern,
        grid=(B, nt),
        in_specs=[pl.BlockSpec((1, tb, qw), lambda i, t: (i, t, 0)),
                  pl.BlockSpec((1, tb, qw), lambda i, t: (i, t, 1)),
                  pl.BlockSpec((1, tb, vw), lambda i, t: (i, t, 1)),
                  pl.BlockSpec((1, tb, vw), lambda i, t: (i, t, 2)),
                  pl.BlockSpec((1, tb, LANES), lambda i, t: (i, t, 0)),
                  pl.BlockSpec((1, LANES), lambda i, t: (0, 0)),
                  pl.BlockSpec((1, vw), lambda i, t: (0, 0)),
                  pl.BlockSpec((1, heads, dk, dv), lambda i, t: (i, 0, 0, 0)),
                  pl.BlockSpec((1, heads, dk), st),
                  pl.BlockSpec((1, 1, heads), st)],
        out_specs=[pl.BlockSpec((1, tb, vw), lambda i, t: (i, t, 0)),
                   pl.BlockSpec((1, heads, dk, dv), lambda i, t: (i, 0, 0, 0)),
                   pl.BlockSpec((1, heads, dk), st),
                   pl.BlockSpec((1, 1, heads), st)],
        out_shape=[jax.ShapeDtypeStruct((B, T, vw), BF16),
                   jax.ShapeDtypeStruct((B, heads, dk, dv), F32),
                   jax.ShapeDtypeStruct((B, heads, dk), F32),
                   jax.ShapeDtypeStruct((B, 1, heads), F32)],
        compiler_params=_params("parallel", "arbitrary"),
        name="mlstm",
    )(u_qkvo, u_qkvo, u_qkvo, u_qkvo, gates, gbias, norm_w, c0, n0, m0)


def _gelu_tanh(x):
    return 0.5 * x * (1.0 + jnp.tanh(0.7978845608028654 * (x + 0.044715 * x * x * x)))


def _lru_kernel(xr_ref, yg_ref, conv0_ref, h0_ref, cw_ref, cb_ref, wa_ref, ba_ref, wx_ref, bx_ref, lam_ref,
                y_ref, conv1_ref, h1_ref, xp_ref, a_ref, b_ref, *, tb, blocks, cw, reset_first):
    t = pl.program_id(1)
    halo = cw - 1

    @pl.when(t == 0)
    def _():
        xp_ref[...] = jnp.zeros_like(xp_ref)
        xp_ref[SUBLANES - halo:, :] = conv0_ref[0]
        h1_ref[0] = h0_ref[0]

    x = xr_ref[0].astype(F32)
    prev = xp_ref[...]
    head_rows = lax.broadcasted_iota(jnp.int32, prev.shape, 0)
    xc = cb_ref[...] + x * cw_ref[cw - 1:cw, :]
    for d in range(1, cw):
        xs = pltpu.roll(x, d, 0)
        first = jnp.where(head_rows < d, pltpu.roll(prev, d, 0), xs[:SUBLANES])
        xs = jnp.concatenate([first, xs[SUBLANES:]], 0)
        xc = xc + xs * cw_ref[cw - 1 - d:cw - d, :]

    sp = _softplus(-lam_ref[...])
    bw = xc.shape[1] // blocks
    for g in range(blocks):
        sl = slice(g * bw, (g + 1) * bw)
        xg = xc[:, sl]
        xgb = xg.astype(BF16)
        gr = _sigmoid(jnp.dot(xgb, wa_ref[g], preferred_element_type=F32) + ba_ref[:, sl])
        gi = _sigmoid(jnp.dot(xgb, wx_ref[g], preferred_element_type=F32) + bx_ref[:, sl])
        log_a = -LRU_C * gr * sp[:, sl]
        th = jnp.tanh(log_a)
        z = -2.0 * th / (1.0 - th)
        mult = jnp.where(z > 0.0, z * lax.rsqrt(z), 0.0)
        if reset_first:
            first = (lax.broadcasted_iota(jnp.int32, mult.shape, 0) == 0) & (t == 0)
            mult = jnp.where(first, 1.0, mult)
        a_ref[:, sl] = jnp.exp(log_a)
        b_ref[:, sl] = mult * gi * xg

    def step(i, h):
        h = a_ref[pl.ds(i, 1), :] * h + b_ref[pl.ds(i, 1), :]
        b_ref[pl.ds(i, 1), :] = h
        return h

    h_last = lax.fori_loop(0, tb, step, h1_ref[0], unroll=8)
    h1_ref[0] = h_last
    y_ref[0] = (b_ref[...] * _gelu_tanh(yg_ref[0].astype(F32))).astype(y_ref.dtype)
    last = x[tb - SUBLANES:, :]
    xp_ref[...] = last
    conv1_ref[0] = last[SUBLANES - halo:, :]


def lru(u_xy, conv0, h0, conv_w, conv_b, wa, ba, wx, bx, lam, reset_first):
    B, T, W2 = u_xy.shape
    W = W2 // 2
    cw = conv_w.shape[0]
    blocks = wa.shape[0]
    tb = _tile(T, 256)
    assert tb >= cw - 1 and cw - 1 <= SUBLANES
    st = lambda i, t: (i, 0, 0)
    vec = pl.BlockSpec((1, W), lambda i, t: (0, 0))
    wsp = pl.BlockSpec(wa.shape, lambda i, t: (0, 0, 0))
    kern = functools.partial(_lru_kernel, tb=tb, blocks=blocks, cw=cw, reset_first=reset_first)
    return pl.pallas_call(
        kern,
        grid=(B, T // tb),
        in_specs=[pl.BlockSpec((1, tb, W), lambda i, t: (i, t, 0)),
                  pl.BlockSpec((1, tb, W), lambda i, t: (i, t, 1)),
                  pl.BlockSpec((1, cw - 1, W), st),
                  pl.BlockSpec((1, 1, W), st),
                  pl.BlockSpec((cw, W), lambda i, t: (0, 0)),
                  vec, wsp, vec, wsp, vec, vec],
        out_specs=[pl.BlockSpec((1, tb, W), lambda i, t: (i, t, 0)),
                   pl.BlockSpec((1, cw - 1, W), st),
                   pl.BlockSpec((1, 1, W), st)],
        out_shape=[jax.ShapeDtypeStruct((B, T, W), BF16),
                   jax.ShapeDtypeStruct((B, cw - 1, W), F32),
                   jax.ShapeDtypeStruct((B, 1, W), F32)],
        scratch_shapes=[pltpu.VMEM((SUBLANES, W), F32),
                        pltpu.VMEM((tb, W), F32),
                        pltpu.VMEM((tb, W), F32)],
        compiler_params=_params("parallel", "arbitrary"),
        name="lru",
    )(u_xy, u_xy, conv0, h0, conv_w, conv_b, wa, ba, wx, bx, lam)


def _lora_math(x, w1_ref, w2_ref, bias_ref, mid, post):
    z = jnp.dot(x, w1_ref[...], preferred_element_type=F32)
    if mid == "tanh":
        z = jnp.tanh(z)
    elif mid == "sigmoid":
        z = _sigmoid(z)
    y = jnp.dot(z.astype(BF16), w2_ref[...], preferred_element_type=F32)
    if post == "decay":
        y = jnp.exp(-EXP_NEG_HALF * _sigmoid(bias_ref[...] + y))
    elif post == "sigmoid":
        y = _sigmoid(bias_ref[...] + y)
    return y


def _rwkv_scan_kernel(r_ref, k_ref, v_ref, w_ref, a_ref, kk_p, ka_p, rk_p, gg_p, gb_p, s0_ref,
                      y_ref, s1_ref, nkk_s, kka_s, km_s, vt_s, ys_s, *, tt, n):
    t = pl.program_id(1)
    nb = n // SUBLANES

    @pl.when(t == 0)
    def _():
        s1_ref[...] = s0_ref[...]

    def prow(p_ref, f):
        return p_ref[f:f + 1, :]

    nrm = jnp.zeros((tt, LANES), F32)
    for f in range(n):
        kk = k_ref[0, f] * prow(kk_p, f)
        nrm = nrm + kk * kk
    inv = lax.rsqrt(jnp.maximum(nrm, 1e-24))
    for f in range(n):
        kf, af = k_ref[0, f], a_ref[0, f]
        kk = kf * prow(kk_p, f) * inv
        nkk_s[f] = -kk
        kka_s[f] = kk * af
        km_s[f] = kf * (1.0 + (af - 1.0) * prow(ka_p, f))

    vt_s[...] = jnp.swapaxes(v_ref[0], 0, 1)
    zeros = tuple(jnp.zeros((SUBLANES, LANES), F32) for _ in range(nb))

    def reduce_keys(kx, acc):
        nk = nkk_s[kx, pl.ds(0, 1), :]
        return tuple(acc[jb] + s1_ref[0, jb, kx] * nk for jb in range(nb))

    sa0 = lax.fori_loop(0, n, reduce_keys, zeros, unroll=8)

    def time_step(i, sa):
        row = pl.ds(i, 1)
        nxt = pl.ds(jnp.minimum(i + 1, tt - 1), 1)
        vt = [vt_s[i, jb * SUBLANES:(jb + 1) * SUBLANES, :] for jb in range(nb)]

        def update_keys(kx, acc):
            wr = w_ref[0, kx, row, :]
            ar = kka_s[kx, row, :]
            mr = km_s[kx, row, :]
            rr = r_ref[0, kx, row, :]
            nk = nkk_s[kx, nxt, :]
            ys, sn = [], []
            for jb in range(nb):
                s = s1_ref[0, jb, kx] * wr + sa[jb] * ar + vt[jb] * mr
                s1_ref[0, jb, kx] = s
                ys.append(acc[jb] + s * rr)
                sn.append(acc[nb + jb] + s * nk)
            return tuple(ys + sn)

        acc = lax.fori_loop(0, n, update_keys, zeros + zeros, unroll=8)
        for jb in range(nb):
            ys_s[i, jb * SUBLANES:(jb + 1) * SUBLANES, :] = acc[jb]
        return tuple(acc[nb:])

    lax.fori_loop(0, tt, time_step, sa0)
    y_ref[0] = jnp.swapaxes(ys_s[...], 0, 1)

    mu = jnp.zeros((tt, LANES), F32)
    cb = jnp.zeros((tt, LANES), F32)
    for f in range(n):
        mu = mu + y_ref[0, f]
        cb = cb + r_ref[0, f] * km_s[f] * prow(rk_p, f)
    mu = mu * (1.0 / n)
    var = jnp.zeros((tt, LANES), F32)
    for f in range(n):
        d = y_ref[0, f] - mu
        var = var + d * d
    rs = lax.rsqrt(var * (1.0 / n) + RWKV_GN_EPS)
    for f in range(n):
        y_ref[0, f] = (y_ref[0, f] - mu) * rs * prow(gg_p, f) + prow(gb_p, f) + cb * v_ref[0, f]


def rwkv_scan(r, k, v, w, a, kk_p, ka_p, rk_p, gg_p, gb_p, s0):
    G, n, T, _ = r.shape
    tt = _tile(T, 32)
    seq = pl.BlockSpec((1, n, tt, LANES), lambda g, t: (g, 0, t, 0))
    par = pl.BlockSpec((n, LANES), lambda g, t: (0, 0))
    st = pl.BlockSpec((1, n // SUBLANES, n, SUBLANES, LANES), lambda g, t: (g, 0, 0, 0, 0))
    return pl.pallas_call(
        functools.partial(_rwkv_scan_kernel, tt=tt, n=n),
        grid=(G, T // tt),
        in_specs=[seq] * 5 + [par] * 5 + [st],
        out_specs=[seq, st],
        out_shape=[jax.ShapeDtypeStruct((G, n, T, LANES), F32),
                   jax.ShapeDtypeStruct(s0.shape, F32)],
        scratch_shapes=[pltpu.VMEM((n, tt, LANES), F32)] * 3 + [pltpu.VMEM((tt, n, LANES), F32)] * 2,
        compiler_params=_params("parallel", "arbitrary"),
        name="rwkv_scan",
    )(r, k, v, w, a, kk_p, ka_p, rk_p, gg_p, gb_p, s0)


def _store_scan_layout(res, z_ref, heads, c0=0):
    bl = LANES // heads
    for c in range(res.shape[1] // LANES):
        sub = [res[s * LANES:(s + 1) * LANES, c * LANES:(c + 1) * LANES].T for s in range(bl)]
        for ni in range(bl):
            tile = jnp.concatenate([sub[s][ni * heads:(ni + 1) * heads, :] for s in range(bl)], 0)
            z_ref[0, (c0 + c) * bl + ni] = tile.T


MXU_WIDTH = 256


def _mixed_tokens(x_ref, prev_ref, shift_ref, mu_ref):
    x = x_ref[...]
    bl, tq, D = x.shape
    prev = jnp.where(pl.program_id(1) == 0, shift_ref[...], prev_ref[:, SUBLANES - 1:, :])
    xs = pltpu.roll(x, 1, 1)
    head_rows = lax.broadcasted_iota(jnp.int32, (bl, SUBLANES, D), 1)
    first = jnp.where(head_rows == 0, prev, xs[:, :SUBLANES, :])
    x_prev = jnp.concatenate([first, xs[:, SUBLANES:, :]], 1)
    xm = x + (x_prev - x) * mu_ref[0]
    return xm.astype(BF16).reshape(bl * tq, D)


def _mix_proj_kernel(x_ref, prev_ref, shift_ref, mu_ref, *refs, heads, lora_args, scan_out):
    o_ref = refs[-1]
    bl, tq, _ = x_ref.shape
    a = _mixed_tokens(x_ref, prev_ref, shift_ref, mu_ref)
    if lora_args is not None:
        res = _lora_math(a, *refs[:3], *lora_args)
        if scan_out:
            _store_scan_layout(res, o_ref, heads)
        else:
            o_ref[...] = res.reshape(bl, tq, res.shape[1])
    elif scan_out:
        w_ref = refs[0]
        for c in range(w_ref.shape[1] // MXU_WIDTH):
            cols = slice(c * MXU_WIDTH, (c + 1) * MXU_WIDTH)
            res = jnp.dot(a, w_ref[:, cols], preferred_element_type=F32)
            _store_scan_layout(res, o_ref, heads, c * (MXU_WIDTH // LANES))
    else:
        res = jnp.dot(a, refs[0][...], preferred_element_type=F32)
        o_ref[...] = res.reshape(bl, tq, res.shape[1])


def mix_proj(x, shift, mu, j, weights, heads, scan_out, lora_args=None):
    B, T, D = x.shape
    N = weights[-1].shape[1]
    bl = LANES // heads
    tq = min(T, LANES)
    assert T % tq == 0 and tq % SUBLANES == 0 and not (scan_out and tq != LANES)
    steps = tq // SUBLANES
    const = lambda w: pl.BlockSpec(w.shape, lambda g, t: (0,) * w.ndim, pipeline_mode=pl.Buffered(1))
    if scan_out:
        out_spec = pl.BlockSpec((1, N // heads, LANES, LANES), lambda g, t: (g, 0, t, 0))
        out_shape = jax.ShapeDtypeStruct((B // bl, N // heads, T, LANES), F32)
    else:
        out_spec = pl.BlockSpec((bl, tq, N), lambda g, t: (g, t, 0))
        out_shape = jax.ShapeDtypeStruct((B, T, N), F32)
    return pl.pallas_call(
        functools.partial(_mix_proj_kernel, heads=heads, lora_args=lora_args, scan_out=scan_out),
        grid=(B // bl, T // tq),
        in_specs=[pl.BlockSpec((bl, tq, D), lambda g, t: (g, t, 0)),
                  pl.BlockSpec((bl, SUBLANES, D), lambda g, t: (g, jnp.maximum(t * steps - 1, 0), 0)),
                  pl.BlockSpec((bl, 1, D), lambda g, t: (g, 0, 0)),
                  pl.BlockSpec((1, 1, D), lambda g, t: (j, 0, 0))] + [const(w) for w in weights],
        out_specs=out_spec,
        out_shape=out_shape,
        compiler_params=_params("parallel", "arbitrary"),
        name="mix_proj",
    )(x, x, shift, mu.reshape(mu.shape[0], 1, D), *weights)


def _scan_proj_ln_kernel(y_ref, gate_ref, w_ref, x_ref, g_ref, b_ref, o_ref, *, heads, alpha):
    bl = LANES // heads
    D = o_ref.shape[2]
    per = MXU_WIDTH // LANES
    acc = None
    for kc in range(D // MXU_WIDTH):
        parts = []
        for c in range(kc * per, (kc + 1) * per):
            sub = [y_ref[0, c * bl + ni].T for ni in range(bl)]
            cols = slice(c * LANES, (c + 1) * LANES)
            tiles = []
            for s in range(bl):
                tile = jnp.concatenate([sub[ni][s * heads:(s + 1) * heads, :] for ni in range(bl)], 0)
                tiles.append((tile.T * gate_ref[s, :, cols]).astype(BF16))
            parts.append(jnp.concatenate(tiles, 0))
        a = jnp.concatenate(parts, 1)
        part = jnp.dot(a, w_ref[kc * MXU_WIDTH:(kc + 1) * MXU_WIDTH, :], preferred_element_type=F32)
        acc = part if acc is None else acc + part
    y = alpha * x_ref[...].reshape(bl * LANES, D) + acc
    o_ref[...] = _layer_norm(y, g_ref[...], b_ref[...]).reshape(bl, LANES, D)


def scan_proj_ln(y, gate, w, x, g, b, alpha, heads):
    G, n, T, _ = y.shape
    B, _, D = x.shape
    bl = LANES // heads
    tok = pl.BlockSpec((bl, LANES, D), lambda g_, t: (g_, t, 0))
    vec = pl.BlockSpec((1, D), lambda g_, t: (0, 0))
    return pl.pallas_call(
        functools.partial(_scan_proj_ln_kernel, heads=heads, alpha=alpha),
        grid=(G, T // LANES),
        in_specs=[pl.BlockSpec((1, n, LANES, LANES), lambda g_, t: (g_, 0, t, 0)), tok,
                  pl.BlockSpec((D, D), lambda g_, t: (0, 0), pipeline_mode=pl.Buffered(1)),
                  tok, vec, vec],
        out_specs=tok,
        out_shape=jax.ShapeDtypeStruct((B, T, D), F32),
        compiler_params=_params("parallel", "parallel"),
        name="scan_proj_ln",
    )(y, gate, w, x, g.reshape(1, D), b.reshape(1, D))


def _pad_cols(w, n):
    return jnp.pad(w, ((0, 0), (0, n - w.shape[1])))


def _pad_rows(w, n):
    return jnp.pad(w, ((0, n - w.shape[0]), (0, 0)))


def _layer_a(x, st, p, li, reset_first, alpha):
    c0, n0, m0, conv0, h0 = st
    B, T, D = x.shape
    heads = p['a_b_ig'].shape[1]
    dv = p['a_mlstm_norm'].shape[1] // heads
    dk = dv // 2
    qw, vw = heads * dk, heads * dv
    W = p['a_conv_w'].shape[2]
    w_in = p['a_w_in'][li]
    n_qkvo = 2 * qw + 2 * vw
    w_qkvo = w_in[:, :n_qkvo].astype(BF16)
    w_gate = _pad_cols(w_in[:, n_qkvo:n_qkvo + 2 * heads], LANES).astype(BF16)
    w_xy = w_in[:, n_qkvo + 2 * heads:].astype(BF16)
    x2 = x.reshape(B * T, D)
    u_qkvo = matmul(x2, w_qkvo, BF16).reshape(B, T, n_qkvo)
    gates = matmul(x2, w_gate).reshape(B, T, LANES)
    u_xy = matmul(x2, w_xy, BF16).reshape(B, T, 2 * W)
    gbias = _pad_cols(jnp.concatenate([p['a_b_ig'][li], p['a_b_fg'][li]])[None, :], LANES)
    hm, c1, n1, m1 = mlstm(u_qkvo, gates, gbias, p['a_mlstm_norm'][li][None, :],
                           c0, n0, m0.reshape(B, 1, heads), heads, dk, dv)
    yb, conv1, h1 = lru(u_xy, conv0, h0.reshape(B, 1, W), p['a_conv_w'][li], p['a_conv_b'][li][None, :],
                        p['a_lru_wa'][li].astype(BF16), p['a_lru_ba'][li][None, :],
                        p['a_lru_wx'][li].astype(BF16), p['a_lru_bx'][li][None, :],
                        p['a_lru_lambda'][li][None, :], reset_first)
    assert vw == W
    y = proj_ln([hm.reshape(B * T, vw), yb.reshape(B * T, W)], None, p['a_w_out'][li].astype(BF16), x2,
                p['ln1_g'][2 * li], p['ln1_b'][2 * li], alpha)
    return y.reshape(B, T, D), (c1, n1, m1.reshape(B, heads), conv1, h1.reshape(B, W))


def _layer_c(x, shift, s0, p, li, layer, alpha):
    B, T, D = x.shape
    H, N = p['c_r_k'].shape[1:]
    M = B * T
    bl = LANES // H
    G = B // bl
    nb = N // SUBLANES
    def cols(w):
        return w.reshape(w.shape[0], H, N).swapaxes(1, 2).reshape(w.shape[0], D)

    w_r = cols(p['c_w_r'][li].astype(BF16))
    w_k = cols(p['c_w_k'][li].astype(BF16))
    w_v = cols(p['c_w_v'][li].astype(BF16))
    rd = -(-p['c_w1'].shape[2] // LANES) * LANES
    ra = -(-p['c_a1'].shape[2] // LANES) * LANES
    w_d = (_pad_cols(p['c_w1'][li], rd).astype(BF16), _pad_rows(cols(p['c_w2'][li]), rd).astype(BF16),
           cols(p['c_w0'][li][None, :]))
    w_a = (_pad_cols(p['c_a1'][li], ra).astype(BF16), _pad_rows(cols(p['c_a2'][li]), ra).astype(BF16),
           cols(p['c_a0'][li][None, :]))
    w_g = (p['c_g1'][li].astype(BF16), cols(p['c_g2'][li].astype(BF16)), jnp.zeros((1, D), F32))
    fused = T % LANES == 0
    mix = functools.partial(mix_proj, x, shift.reshape(B, 1, D), p['c_mu'][li])
    r = mix(0, (w_r,), H, fused)
    k = mix(2, (w_k,), H, fused)
    v = mix(3, (w_v,), H, fused)
    decay = mix(1, w_d, H, fused, ("tanh", "decay"))
    a = mix(4, w_a, H, fused, ("none", "sigmoid"))
    g = mix(5, w_g, H, False, ("sigmoid", "none"))
    if not fused:
        def to_scan(z):
            return z.reshape(G, bl, T, N, H).transpose(0, 3, 2, 1, 4).reshape(G, N, T, LANES)

        r, k, v, decay, a = (to_scan(z) for z in (r, k, v, decay, a))

    def par(z):
        return jnp.tile(z.reshape(H, N).T, (1, bl))

    s0t = s0.reshape(G, bl, H, nb, SUBLANES, N).transpose(0, 3, 5, 4, 1, 2).reshape(G, nb, N, SUBLANES, LANES)
    yt, s1t = rwkv_scan(r, k, v, decay, a,
                        par(p['c_k_k'][li]), par(p['c_k_a'][li]), par(p['c_r_k'][li].reshape(D)),
                        par(p['c_gn_g'][li]), par(p['c_gn_b'][li]), s0t)
    s1 = s1t.reshape(G, nb, N, SUBLANES, bl, H).transpose(0, 4, 5, 1, 3, 2).reshape(B, H, N, N)
    w_o = p['c_w_o'][li].astype(BF16).reshape(H, N, D).swapaxes(0, 1).reshape(D, D)
    if fused:
        out = scan_proj_ln(yt, g, w_o, x, p['ln1_g'][layer], p['ln1_b'][layer], alpha, H)
    else:
        y = yt.reshape(G, N, T, bl, H).transpose(0, 3, 2, 1, 4).reshape(M, D)
        out = proj_ln([y], g.reshape(M, D), w_o, x.reshape(M, D), p['ln1_g'][layer], p['ln1_b'][layer], alpha)
    return out.reshape(B, T, D), (x[:, -1], s1)


def _trunk(x, states, p, reset_first):
    mC, mn, mm, cv, hl, sh, S = states
    depth = p['ln1_g'].shape[0]
    alpha = (2 * depth) ** 0.25
    B, T, D = x.shape
    new_a, new_c = [], []
    for layer in range(depth):
        li = layer // 2
        if layer % 2 == 0:
            x, st = _layer_a(x, (mC[li], mn[li], mm[li], cv[li], hl[li]), p, li, reset_first, alpha)
            new_a.append(st)
        else:
            x, st = _layer_c(x, sh[li], S[li], p, li, layer, alpha)
            new_c.append(st)
        x = mlp_ln(x.reshape(B * T, D), p['mlp_w1'].astype(BF16), p['mlp_w2'].astype(BF16), layer,
                   p['ln2_g'][layer], p['ln2_b'][layer], alpha).reshape(B, T, D)
    sa = [jnp.stack([s[j] for s in new_a]) for j in range(5)]
    sc = [jnp.stack([s[j] for s in new_c]) for j in range(2)]
    return x, sa + sc


def kernel(x_prompt, x_sample, state_mlstm_C, state_mlstm_n, state_mlstm_m, state_lru_conv, state_lru_h,
           state_rwkv_shift, state_rwkv_S, a_w_in, a_b_ig, a_b_fg, a_mlstm_norm, a_conv_w, a_conv_b,
           a_lru_wa, a_lru_ba, a_lru_wx, a_lru_bx, a_lru_lambda, a_w_out, c_mu, c_w_r, c_w_k, c_w_v,
           c_w0, c_w1, c_w2, c_a0, c_a1, c_a2, c_g1, c_g2, c_k_k, c_k_a, c_r_k, c_gn_g, c_gn_b, c_w_o,
           ln1_g, ln1_b, ln2_g, ln2_b, mlp_w1, mlp_w2):
    p = dict(a_w_in=a_w_in, a_b_ig=a_b_ig, a_b_fg=a_b_fg, a_mlstm_norm=a_mlstm_norm, a_conv_w=a_conv_w,
             a_conv_b=a_conv_b, a_lru_wa=a_lru_wa, a_lru_ba=a_lru_ba, a_lru_wx=a_lru_wx, a_lru_bx=a_lru_bx,
             a_lru_lambda=a_lru_lambda, a_w_out=a_w_out, c_mu=c_mu, c_w_r=c_w_r, c_w_k=c_w_k, c_w_v=c_w_v,
             c_w0=c_w0, c_w1=c_w1, c_w2=c_w2, c_a0=c_a0, c_a1=c_a1, c_a2=c_a2, c_g1=c_g1, c_g2=c_g2,
             c_k_k=c_k_k, c_k_a=c_k_a, c_r_k=c_r_k, c_gn_g=c_gn_g, c_gn_b=c_gn_b, c_w_o=c_w_o,
             ln1_g=ln1_g, ln1_b=ln1_b, ln2_g=ln2_g, ln2_b=ln2_b, mlp_w1=mlp_w1, mlp_w2=mlp_w2)
    Bp = x_prompt.shape[0]
    init = tuple(jnp.zeros((s.shape[0], Bp) + s.shape[2:], s.dtype)
                 for s in (state_mlstm_C, state_mlstm_n, state_mlstm_m, state_lru_conv, state_lru_h,
                           state_rwkv_shift, state_rwkv_S))
    y_prompt, ps = _trunk(x_prompt, init, p, True)
    y_sample, ss = _trunk(x_sample, (state_mlstm_C, state_mlstm_n, state_mlstm_m, state_lru_conv,
                                     state_lru_h, state_rwkv_shift, state_rwkv_S), p, False)
    return (y_prompt, y_sample, *ps, *ss)
```

```python
import functools

import jax
import jax.numpy as jnp
from jax import lax
from jax.experimental import pallas as pl
from jax.experimental.pallas import tpu as pltpu

F32 = jnp.float32
BF16 = jnp.bfloat16

LANES = 128
SUBLANES = 8
VMEM_LIMIT_BYTES = 56 * 1024 * 1024

CHUNK = 256
LRU_C = 8.0
LN_EPS = 1e-5
RWKV_GN_EPS = 64e-5
HIGHEST = lax.Precision.HIGHEST
EXP_NEG_HALF = 0.6065306597126334


def _params(*sem):
    return pltpu.CompilerParams(dimension_semantics=sem, vmem_limit_bytes=VMEM_LIMIT_BYTES)


def _tile(n, pref):
    t = min(n, pref)
    while n % t:
        t -= 1
    return t


def _sigmoid(x):
    return 1.0 / (1.0 + jnp.exp(-x))


def _softplus(x):
    return jnp.maximum(x, 0.0) + jnp.log1p(jnp.exp(-jnp.abs(x)))


def _layer_norm(y, g, b):
    mu = jnp.mean(y, -1, keepdims=True)
    d = y - mu
    var = jnp.mean(d * d, -1, keepdims=True)
    return d * lax.rsqrt(var + LN_EPS) * g + b


def _mm_kernel(a_ref, w_ref, o_ref, ab_ref):
    @pl.when(pl.program_id(1) == 0)
    def _():
        ab_ref[...] = a_ref[...].astype(BF16)

    o_ref[...] = jnp.dot(ab_ref[...], w_ref[...], preferred_element_type=F32)


def matmul(a, w, tm=1024, tn=1024):
    M, K = a.shape
    N = w.shape[1]
    tm, tn = _tile(M, tm), _tile(N, tn)
    return pl.pallas_call(
        _mm_kernel,
        grid=(M // tm, N // tn),
        in_specs=[pl.BlockSpec((tm, K), lambda i, j: (i, 0)),
                  pl.BlockSpec((K, tn), lambda i, j: (0, j))],
        out_specs=pl.BlockSpec((tm, tn), lambda i, j: (i, j)),
        out_shape=jax.ShapeDtypeStruct((M, N), F32),
        scratch_shapes=[pltpu.VMEM((tm, K), BF16)],
        compiler_params=_params("parallel", "arbitrary"),
        name="matmul",
    )(a, w)


def _proj_ln_kernel(*refs, n_a, alpha, gated):
    a_refs, rest = refs[:n_a], list(refs[n_a:])
    gate_ref = rest.pop(0) if gated else None
    w_ref, x_ref, g_ref, b_ref, o_ref = rest
    a = jnp.concatenate([r[...] for r in a_refs], 1) if n_a > 1 else a_refs[0][...]
    if gated:
        a = a * gate_ref[...]
    acc = jnp.dot(a.astype(BF16), w_ref[...], preferred_element_type=F32)
    o_ref[...] = _layer_norm(alpha * x_ref[...] + acc, g_ref[...], b_ref[...])


PROJ_VMEM_BUDGET = 46 * 1024 * 1024


def proj_ln(a_list, gate, w, x, g, b, alpha):
    n_a = len(a_list)
    M, K = a_list[0].shape
    D = w.shape[1]
    assert w.shape[0] == n_a * K and not (gate is not None and n_a > 1)
    gated = gate is not None

    def vmem_bytes(tm):
        blocks = sum(tm * K * a.dtype.itemsize for a in a_list) + gated * tm * K * 4 + 2 * tm * D * 4
        return 2 * blocks + w.size * w.dtype.itemsize

    tm = next(t for t in (512, 256, 128, 64, 32, 16, 8) if M % t == 0 and vmem_bytes(t) <= PROJ_VMEM_BUDGET)
    a_spec = pl.BlockSpec((tm, K), lambda i: (i, 0))
    row_spec = pl.BlockSpec((tm, D), lambda i: (i, 0))
    vec_spec = pl.BlockSpec((1, D), lambda i: (0, 0))
    in_specs = [a_spec] * n_a + [a_spec] * gated
    in_specs += [pl.BlockSpec(w.shape, lambda i: (0, 0), pipeline_mode=pl.Buffered(1)),
                 row_spec, vec_spec, vec_spec]
    args = list(a_list) + [gate] * gated + [w, x, g.reshape(1, D), b.reshape(1, D)]
    return pl.pallas_call(
        functools.partial(_proj_ln_kernel, n_a=n_a, alpha=alpha, gated=gated),
        grid=(M // tm,),
        in_specs=in_specs,
        out_specs=row_spec,
        out_shape=jax.ShapeDtypeStruct((M, D), F32),
        compiler_params=_params("parallel"),
        name="proj_ln",
    )(*args)


def _mlp_ln_kernel(x_ref, w1_ref, w2_ref, g_ref, b_ref, o_ref, xb_ref, *, nf, alpha):
    f = pl.program_id(1)

    @pl.when(f == 0)
    def _():
        o_ref[...] = jnp.zeros_like(o_ref)
        xb_ref[...] = x_ref[...].astype(BF16)

    h = jnp.maximum(jnp.dot(xb_ref[...], w1_ref[0], preferred_element_type=F32), 0.0)
    o_ref[...] += jnp.dot((h * h).astype(BF16), w2_ref[0], preferred_element_type=F32)

    @pl.when(f == nf - 1)
    def _():
        y = alpha * x_ref[...] + o_ref[...]
        o_ref[...] = _layer_norm(y, g_ref[...], b_ref[...])


def mlp_ln(x, w1, w2, layer, g, b, alpha, tm=1024, tf=512):
    M, D = x.shape
    FF = w1.shape[2]
    tm, tf = _tile(M, tm), _tile(FF, tf)
    nf = FF // tf
    row_spec = pl.BlockSpec((tm, D), lambda i, f: (i, 0))
    vec_spec = pl.BlockSpec((1, D), lambda i, f: (0, 0))
    return pl.pallas_call(
        functools.partial(_mlp_ln_kernel, nf=nf, alpha=alpha),
        grid=(M // tm, nf),
        in_specs=[row_spec,
                  pl.BlockSpec((1, D, tf), lambda i, f: (layer, 0, f)),
                  pl.BlockSpec((1, tf, D), lambda i, f: (layer, f, 0)),
                  vec_spec, vec_spec],
        out_specs=row_spec,
        out_shape=jax.ShapeDtypeStruct((M, D), F32),
        scratch_shapes=[pltpu.VMEM((tm, D), BF16)],
        compiler_params=_params("parallel", "arbitrary"),
        name="mlp_ln",
    )(x, w1, w2, g.reshape(1, D), b.reshape(1, D))


def _mlstm_kernel(q_ref, k_ref, v_ref, o_ref, x_ref, wg_ref, gbias_ref, nw_ref, c0_ref, n0_ref, m0_ref,
                  h_ref, c1_ref, n1_ref, m1_ref, *, heads, dk, dv, L):
    t = pl.program_id(1)

    @pl.when(t == 0)
    def _():
        c1_ref[...] = c0_ref[...]
        n1_ref[...] = n0_ref[...]
        m1_ref[...] = m0_ref[...]

    row = lax.broadcasted_iota(jnp.int32, (L, L), 0)
    col = lax.broadcasted_iota(jnp.int32, (L, L), 1)
    causal = row >= col
    tril = causal.astype(F32)
    sel_r = lax.broadcasted_iota(jnp.int32, (2 * heads, LANES), 0)
    sel_c = lax.broadcasted_iota(jnp.int32, (2 * heads, LANES), 1)
    pick = (sel_r == sel_c).astype(F32)
    lane = lax.broadcasted_iota(jnp.int32, (L, LANES), 1)
    scale = dk ** -0.5

    gt = jnp.dot(x_ref[0].astype(BF16), wg_ref[...], preferred_element_type=F32) + gbias_ref[...]
    logf = jnp.minimum(gt, 0.0) - jnp.log1p(jnp.exp(-jnp.abs(gt)))
    gl = jnp.where(lane < heads, gt, logf)
    cum = jnp.dot(tril, gl, preferred_element_type=F32, precision=HIGHEST)
    nt_dims = (((1,), (1,)), ((), ()))
    gl_t = lax.dot_general(pick, gl, nt_dims, preferred_element_type=F32, precision=HIGHEST)
    cum_t = lax.dot_general(pick, cum, nt_dims, preferred_element_type=F32, precision=HIGHEST)
    for h in range(heads):
        ig_col = gl[:, h:h + 1]
        b_col = cum[:, heads + h:heads + h + 1]
        ig_row = gl_t[h:h + 1, :]
        b_row = cum_t[heads + h:heads + h + 1, :]
        b_last = b_row[:, L - 1:L]
        m_prev = m1_ref[0, :, h:h + 1]
        qh = (q_ref[0, :, h * dk:(h + 1) * dk] * scale).astype(BF16)
        kf = k_ref[0, :, h * dk:(h + 1) * dk]
        kh = kf.astype(BF16)
        vh = v_ref[0, :, h * dv:(h + 1) * dv].astype(BF16)
        c_prev = c1_ref[0, h]
        n_prev = n1_ref[0, h:h + 1, :]

        dmat = jnp.where(causal, b_col - b_row + ig_row, -jnp.inf)
        inter = b_col + m_prev
        m_t = jnp.maximum(inter, jnp.max(dmat, -1, keepdims=True))
        w_intra = jnp.exp(dmat - m_t)
        w_inter = jnp.exp(inter - m_t)
        s = lax.dot_general(qh, kh, nt_dims, preferred_element_type=F32)
        qk = s * w_intra
        num = jnp.dot(qk.astype(BF16), vh, preferred_element_type=F32)
        num = num + w_inter * jnp.dot(qh, c_prev.astype(BF16), preferred_element_type=F32)
        qn = jnp.sum(qh.astype(F32) * n_prev, -1, keepdims=True)
        den = jnp.sum(qk, -1, keepdims=True) + w_inter * qn
        hh = num / jnp.maximum(jnp.abs(den), jnp.exp(-m_t))

        mu = jnp.mean(hh, -1, keepdims=True)
        d = hh - mu
        var = jnp.mean(d * d, -1, keepdims=True)
        hn = d * lax.rsqrt(var + LN_EPS)
        og = o_ref[0, :, h * dv:(h + 1) * dv]
        hg = hn * nw_ref[:, h * dv:(h + 1) * dv] * _sigmoid(og)
        h_ref[0, :, h * dv:(h + 1) * dv] = hg.astype(h_ref.dtype)

        lw_col = b_last - b_col + ig_col
        lw_row = b_last - b_row + ig_row
        m_new = jnp.maximum(b_last + m_prev, jnp.max(lw_row, -1, keepdims=True))
        ws_col = jnp.exp(lw_col - m_new)
        wc = jnp.exp(b_last + m_prev - m_new)
        kw = kf * ws_col
        tn_dims = (((0,), (0,)), ((), ()))
        c1_ref[0, h] = wc * c_prev + lax.dot_general(kw.astype(BF16), vh, tn_dims,
                                                     preferred_element_type=F32)
        n1_ref[0, h:h + 1, :] = wc * n_prev + jnp.sum(kw, 0, keepdims=True)
        m1_ref[0, :, h:h + 1] = m_new


def mlstm(u, x, w_gate, gbias, norm_w, c0, n0, m0, heads, dk, dv):
    B, T, D = x.shape
    L = min(T, CHUNK)
    assert T % L == 0
    tb = L
    qw, vw = heads * dk, heads * dv
    assert qw % LANES == 0 and vw == 2 * qw
    st = lambda i, t: (i, 0, 0)
    kern = functools.partial(_mlstm_kernel, heads=heads, dk=dk, dv=dv, L=L)
    return pl.pallas_call(
        kern,
        grid=(B, T // tb),
        in_specs=[pl.BlockSpec((1, tb, qw), lambda i, t: (i, t, 0)),
                  pl.BlockSpec((1, tb, qw), lambda i, t: (i, t, 1)),
                  pl.BlockSpec((1, tb, vw), lambda i, t: (i, t, 1)),
                  pl.BlockSpec((1, tb, vw), lambda i, t: (i, t, 2)),
                  pl.BlockSpec((1, tb, D), lambda i, t: (i, t, 0)),
                  pl.BlockSpec((D, LANES), lambda i, t: (0, 0)),
                  pl.BlockSpec((1, LANES), lambda i, t: (0, 0)),
                  pl.BlockSpec((1, vw), lambda i, t: (0, 0)),
                  pl.BlockSpec((1, heads, dk, dv), lambda i, t: (i, 0, 0, 0)),
                  pl.BlockSpec((1, heads, dk), st),
                  pl.BlockSpec((1, 1, heads), st)],
        out_specs=[pl.BlockSpec((1, tb, vw), lambda i, t: (i, t, 0)),
                   pl.BlockSpec((1, heads, dk, dv), lambda i, t: (i, 0, 0, 0)),
                   pl.BlockSpec((1, heads, dk), st),
                   pl.BlockSpec((1, 1, heads), st)],
        out_shape=[jax.ShapeDtypeStruct((B, T, vw), BF16),
                   jax.ShapeDtypeStruct((B, heads, dk, dv), F32),
                   jax.ShapeDtypeStruct((B, heads, dk), F32),
                   jax.ShapeDtypeStruct((B, 1, heads), F32)],
        compiler_params=_params("parallel", "arbitrary"),
        name="mlstm",
    )(u, u, u, u, x, w_gate, gbias, norm_w, c0, n0, m0)


def _gelu_tanh(x):
    return 0.5 * x * (1.0 + jnp.tanh(0.7978845608028654 * (x + 0.044715 * x * x * x)))


def _lru_kernel(xr_ref, yg_ref, conv0_ref, h0_ref, cw_ref, cb_ref, wa_ref, ba_ref, wx_ref, bx_ref, lam_ref,
                y_ref, conv1_ref, h1_ref, xp_ref, a_ref, b_ref, *, tb, blocks, cw, reset_first):
    t = pl.program_id(1)
    halo = cw - 1

    @pl.when(t == 0)
    def _():
        xp_ref[...] = jnp.zeros_like(xp_ref)
        xp_ref[SUBLANES - halo:, :] = conv0_ref[0]
        h1_ref[0] = h0_ref[0]

    x = xr_ref[0]
    prev = xp_ref[...]
    head_rows = lax.broadcasted_iota(jnp.int32, prev.shape, 0)
    xc = cb_ref[...] + x * cw_ref[cw - 1:cw, :]
    for d in range(1, cw):
        xs = pltpu.roll(x, d, 0)
        first = jnp.where(head_rows < d, pltpu.roll(prev, d, 0), xs[:SUBLANES])
        xs = jnp.concatenate([first, xs[SUBLANES:]], 0)
        xc = xc + xs * cw_ref[cw - 1 - d:cw - d, :]

    sp = _softplus(-lam_ref[...])
    bw = xc.shape[1] // blocks
    for g in range(blocks):
        sl = slice(g * bw, (g + 1) * bw)
        xg = xc[:, sl]
        xgb = xg.astype(BF16)
        gr = _sigmoid(jnp.dot(xgb, wa_ref[g], preferred_element_type=F32) + ba_ref[:, sl])
        gi = _sigmoid(jnp.dot(xgb, wx_ref[g], preferred_element_type=F32) + bx_ref[:, sl])
        log_a = -LRU_C * gr * sp[:, sl]
        th = jnp.tanh(log_a)
        z = -2.0 * th / (1.0 - th)
        mult = jnp.where(z > 0.0, z * lax.rsqrt(z), 0.0)
        if reset_first:
            first = (lax.broadcasted_iota(jnp.int32, mult.shape, 0) == 0) & (t == 0)
            mult = jnp.where(first, 1.0, mult)
        a_ref[:, sl] = jnp.exp(log_a)
        b_ref[:, sl] = mult * gi * xg

    def step(i, h):
        h = a_ref[pl.ds(i, 1), :] * h + b_ref[pl.ds(i, 1), :]
        b_ref[pl.ds(i, 1), :] = h
        return h

    h_last = lax.fori_loop(0, tb, step, h1_ref[0], unroll=8)
    h1_ref[0] = h_last
    y_ref[0] = (b_ref[...] * _gelu_tanh(yg_ref[0])).astype(y_ref.dtype)
    last = xr_ref[0, tb - SUBLANES:, :]
    xp_ref[...] = last
    conv1_ref[0] = last[SUBLANES - halo:, :]


def lru(u, col0, conv0, h0, conv_w, conv_b, wa, ba, wx, bx, lam, reset_first):
    B, T, _ = u.shape
    W = conv_w.shape[1]
    assert col0 % W == 0
    cb0 = col0 // W
    cw = conv_w.shape[0]
    blocks = wa.shape[0]
    tb = _tile(T, 256)
    assert tb >= cw - 1 and cw - 1 <= SUBLANES
    st = lambda i, t: (i, 0, 0)
    vec = pl.BlockSpec((1, W), lambda i, t: (0, 0))
    wsp = pl.BlockSpec(wa.shape, lambda i, t: (0, 0, 0))
    kern = functools.partial(_lru_kernel, tb=tb, blocks=blocks, cw=cw, reset_first=reset_first)
    return pl.pallas_call(
        kern,
        grid=(B, T // tb),
        in_specs=[pl.BlockSpec((1, tb, W), lambda i, t: (i, t, cb0)),
                  pl.BlockSpec((1, tb, W), lambda i, t: (i, t, cb0 + 1)),
                  pl.BlockSpec((1, cw - 1, W), st),
                  pl.BlockSpec((1, 1, W), st),
                  pl.BlockSpec((cw, W), lambda i, t: (0, 0)),
                  vec, wsp, vec, wsp, vec, vec],
        out_specs=[pl.BlockSpec((1, tb, W), lambda i, t: (i, t, 0)),
                   pl.BlockSpec((1, cw - 1, W), st),
                   pl.BlockSpec((1, 1, W), st)],
        out_shape=[jax.ShapeDtypeStruct((B, T, W), BF16),
                   jax.ShapeDtypeStruct((B, cw - 1, W), F32),
                   jax.ShapeDtypeStruct((B, 1, W), F32)],
        scratch_shapes=[pltpu.VMEM((SUBLANES, W), F32),
                        pltpu.VMEM((tb, W), F32),
                        pltpu.VMEM((tb, W), F32)],
        compiler_params=_params("parallel", "arbitrary"),
        name="lru",
    )(u, u, conv0, h0, conv_w, conv_b, wa, ba, wx, bx, lam)


def _lora_math(x, w1_ref, w2_ref, bias_ref, mid, post):
    z = jnp.dot(x, w1_ref[...], preferred_element_type=F32)
    if mid == "tanh":
        z = jnp.tanh(z)
    elif mid == "sigmoid":
        z = _sigmoid(z)
    y = jnp.dot(z.astype(BF16), w2_ref[...], preferred_element_type=F32)
    if post == "decay":
        y = jnp.exp(-EXP_NEG_HALF * _sigmoid(bias_ref[...] + y))
    elif post == "sigmoid":
        y = _sigmoid(bias_ref[...] + y)
    return y


def _rwkv_scan_kernel(r_ref, k_ref, v_ref, w_ref, a_ref, kk_p, ka_p, rk_p, gg_p, gb_p, s0_ref,
                      y_ref, s1_ref, nkk_s, kka_s, km_s, vt_s, ys_s, *, tt, n):
    t = pl.program_id(1)
    nb = n // SUBLANES

    @pl.when(t == 0)
    def _():
        s1_ref[...] = s0_ref[...]

    def prow(p_ref, f):
        return p_ref[f:f + 1, :]

    nrm = jnp.zeros((tt, LANES), F32)
    for f in range(n):
        kk = k_ref[0, f] * prow(kk_p, f)
        nrm = nrm + kk * kk
    inv = lax.rsqrt(jnp.maximum(nrm, 1e-24))
    for f in range(n):
        kf, af = k_ref[0, f], a_ref[0, f]
        kk = kf * prow(kk_p, f) * inv
        nkk_s[f] = -kk
        kka_s[f] = kk * af
        km_s[f] = kf * (1.0 + (af - 1.0) * prow(ka_p, f))

    vt_s[...] = jnp.swapaxes(v_ref[0], 0, 1)
    zeros = tuple(jnp.zeros((SUBLANES, LANES), F32) for _ in range(nb))

    def reduce_keys(kx, acc):
        nk = nkk_s[kx, pl.ds(0, 1), :]
        return tuple(acc[jb] + s1_ref[0, jb, kx] * nk for jb in range(nb))

    sa0 = lax.fori_loop(0, n, reduce_keys, zeros, unroll=8)

    def time_step(i, sa):
        row = pl.ds(i, 1)
        nxt = pl.ds(jnp.minimum(i + 1, tt - 1), 1)
        vt = [vt_s[i, jb * SUBLANES:(jb + 1) * SUBLANES, :] for jb in range(nb)]

        def update_keys(kx, acc):
            wr = w_ref[0, kx, row, :]
            ar = kka_s[kx, row, :]
            mr = km_s[kx, row, :]
            rr = r_ref[0, kx, row, :]
            nk = nkk_s[kx, nxt, :]
            ys, sn = [], []
            for jb in range(nb):
                s = s1_ref[0, jb, kx] * wr + sa[jb] * ar + vt[jb] * mr
                s1_ref[0, jb, kx] = s
                ys.append(acc[jb] + s * rr)
                sn.append(acc[nb + jb] + s * nk)
            return tuple(ys + sn)

        acc = lax.fori_loop(0, n, update_keys, zeros + zeros, unroll=8)
        for jb in range(nb):
            ys_s[i, jb * SUBLANES:(jb + 1) * SUBLANES, :] = acc[jb]
        return tuple(acc[nb:])

    lax.fori_loop(0, tt, time_step, sa0)
    y_ref[0] = jnp.swapaxes(ys_s[...], 0, 1)

    mu = jnp.zeros((tt, LANES), F32)
    cb = jnp.zeros((tt, LANES), F32)
    for f in range(n):
        mu = mu + y_ref[0, f]
        cb = cb + r_ref[0, f] * km_s[f] * prow(rk_p, f)
    mu = mu * (1.0 / n)
    var = jnp.zeros((tt, LANES), F32)
    for f in range(n):
        d = y_ref[0, f] - mu
        var = var + d * d
    rs = lax.rsqrt(var * (1.0 / n) + RWKV_GN_EPS)
    for f in range(n):
        y_ref[0, f] = (y_ref[0, f] - mu) * rs * prow(gg_p, f) + prow(gb_p, f) + cb * v_ref[0, f]


def rwkv_scan(r, k, v, w, a, kk_p, ka_p, rk_p, gg_p, gb_p, s0):
    G, n, T, _ = r.shape
    tt = _tile(T, 64)
    seq = pl.BlockSpec((1, n, tt, LANES), lambda g, t: (g, 0, t, 0))
    par = pl.BlockSpec((n, LANES), lambda g, t: (0, 0))
    st = pl.BlockSpec((1, n // SUBLANES, n, SUBLANES, LANES), lambda g, t: (g, 0, 0, 0, 0))
    return pl.pallas_call(
        functools.partial(_rwkv_scan_kernel, tt=tt, n=n),
        grid=(G, T // tt),
        in_specs=[seq] * 5 + [par] * 5 + [st],
        out_specs=[seq, st],
        out_shape=[jax.ShapeDtypeStruct((G, n, T, LANES), F32),
                   jax.ShapeDtypeStruct(s0.shape, F32)],
        scratch_shapes=[pltpu.VMEM((n, tt, LANES), F32)] * 3 + [pltpu.VMEM((tt, n, LANES), F32)] * 2,
        compiler_params=_params("parallel", "arbitrary"),
        name="rwkv_scan",
    )(r, k, v, w, a, kk_p, ka_p, rk_p, gg_p, gb_p, s0)


def _store_scan_layout(res, z_ref, heads, c0=0):
    bl = LANES // heads
    for c in range(res.shape[1] // LANES):
        sub = [res[s * LANES:(s + 1) * LANES, c * LANES:(c + 1) * LANES].T for s in range(bl)]
        for ni in range(bl):
            tile = jnp.concatenate([sub[s][ni * heads:(ni + 1) * heads, :] for s in range(bl)], 0)
            z_ref[0, (c0 + c) * bl + ni] = tile.T


MXU_WIDTH = 256


def _mixed_tokens(x_ref, prev_ref, shift_ref, mu_ref):
    x = x_ref[...]
    bl, tq, D = x.shape
    prev = jnp.where(pl.program_id(1) == 0, shift_ref[...], prev_ref[:, SUBLANES - 1:, :])
    xs = pltpu.roll(x, 1, 1)
    head_rows = lax.broadcasted_iota(jnp.int32, (bl, SUBLANES, D), 1)
    first = jnp.where(head_rows == 0, prev, xs[:, :SUBLANES, :])
    x_prev = jnp.concatenate([first, xs[:, SUBLANES:, :]], 1)
    xm = x + (x_prev - x) * mu_ref[0]
    return xm.astype(BF16).reshape(bl * tq, D)


def _mix_proj_kernel(x_ref, prev_ref, shift_ref, mu_ref, *refs, heads, lora_args, scan_out):
    o_ref = refs[-1]
    bl, tq, _ = x_ref.shape
    a = _mixed_tokens(x_ref, prev_ref, shift_ref, mu_ref)
    if lora_args is not None:
        res = _lora_math(a, *refs[:3], *lora_args)
        if scan_out:
            _store_scan_layout(res, o_ref, heads)
        else:
            o_ref[...] = res.reshape(bl, tq, res.shape[1])
    elif scan_out:
        w_ref = refs[0]
        for c in range(w_ref.shape[1] // MXU_WIDTH):
            cols = slice(c * MXU_WIDTH, (c + 1) * MXU_WIDTH)
            res = jnp.dot(a, w_ref[:, cols], preferred_element_type=F32)
            _store_scan_layout(res, o_ref, heads, c * (MXU_WIDTH // LANES))
    else:
        res = jnp.dot(a, refs[0][...], preferred_element_type=F32)
        o_ref[...] = res.reshape(bl, tq, res.shape[1])


def mix_proj(x, shift, mu, j, weights, heads, scan_out, lora_args=None):
    B, T, D = x.shape
    N = weights[-1].shape[1]
    bl = LANES // heads
    tq = min(T, LANES)
    assert T % tq == 0 and tq % SUBLANES == 0 and not (scan_out and tq != LANES)
    steps = tq // SUBLANES
    const = lambda w: pl.BlockSpec(w.shape, lambda g, t: (0,) * w.ndim, pipeline_mode=pl.Buffered(1))
    if scan_out:
        out_spec = pl.BlockSpec((1, N // heads, LANES, LANES), lambda g, t: (g, 0, t, 0))
        out_shape = jax.ShapeDtypeStruct((B // bl, N // heads, T, LANES), F32)
    else:
        out_spec = pl.BlockSpec((bl, tq, N), lambda g, t: (g, t, 0))
        out_shape = jax.ShapeDtypeStruct((B, T, N), F32)
    return pl.pallas_call(
        functools.partial(_mix_proj_kernel, heads=heads, lora_args=lora_args, scan_out=scan_out),
        grid=(B // bl, T // tq),
        in_specs=[pl.BlockSpec((bl, tq, D), lambda g, t: (g, t, 0)),
                  pl.BlockSpec((bl, SUBLANES, D), lambda g, t: (g, jnp.maximum(t * steps - 1, 0), 0)),
                  pl.BlockSpec((bl, 1, D), lambda g, t: (g, 0, 0)),
                  pl.BlockSpec((1, 1, D), lambda g, t: (j, 0, 0))] + [const(w) for w in weights],
        out_specs=out_spec,
        out_shape=out_shape,
        compiler_params=_params("parallel", "arbitrary"),
        name="mix_proj",
    )(x, x, shift, mu.reshape(mu.shape[0], 1, D), *weights)


def _scan_proj_ln_kernel(y_ref, gate_ref, w_ref, x_ref, g_ref, b_ref, o_ref, *, heads, alpha):
    bl = LANES // heads
    D = o_ref.shape[2]
    per = MXU_WIDTH // LANES
    acc = None
    for kc in range(D // MXU_WIDTH):
        parts = []
        for c in range(kc * per, (kc + 1) * per):
            sub = [y_ref[0, c * bl + ni].T for ni in range(bl)]
            cols = slice(c * LANES, (c + 1) * LANES)
            tiles = []
            for s in range(bl):
                tile = jnp.concatenate([sub[ni][s * heads:(s + 1) * heads, :] for ni in range(bl)], 0)
                tiles.append((tile.T * gate_ref[s, :, cols]).astype(BF16))
            parts.append(jnp.concatenate(tiles, 0))
        a = jnp.concatenate(parts, 1)
        part = jnp.dot(a, w_ref[kc * MXU_WIDTH:(kc + 1) * MXU_WIDTH, :], preferred_element_type=F32)
        acc = part if acc is None else acc + part
    y = alpha * x_ref[...].reshape(bl * LANES, D) + acc
    o_ref[...] = _layer_norm(y, g_ref[...], b_ref[...]).reshape(bl, LANES, D)


def scan_proj_ln(y, gate, w, x, g, b, alpha, heads):
    G, n, T, _ = y.shape
    B, _, D = x.shape
    bl = LANES // heads
    tok = pl.BlockSpec((bl, LANES, D), lambda g_, t: (g_, t, 0))
    vec = pl.BlockSpec((1, D), lambda g_, t: (0, 0))
    return pl.pallas_call(
        functools.partial(_scan_proj_ln_kernel, heads=heads, alpha=alpha),
        grid=(G, T // LANES),
        in_specs=[pl.BlockSpec((1, n, LANES, LANES), lambda g_, t: (g_, 0, t, 0)), tok,
                  pl.BlockSpec((D, D), lambda g_, t: (0, 0), pipeline_mode=pl.Buffered(1)),
                  tok, vec, vec],
        out_specs=tok,
        out_shape=jax.ShapeDtypeStruct((B, T, D), F32),
        compiler_params=_params("parallel", "parallel"),
        name="scan_proj_ln",
    )(y, gate, w, x, g.reshape(1, D), b.reshape(1, D))


def _pad_cols(w, n):
    return jnp.pad(w, ((0, 0), (0, n - w.shape[1])))


def _pad_rows(w, n):
    return jnp.pad(w, ((0, n - w.shape[0]), (0, 0)))


def _layer_a(x, st, p, li, reset_first, alpha):
    c0, n0, m0, conv0, h0 = st
    B, T, D = x.shape
    heads = p['a_b_ig'].shape[1]
    dv = p['a_mlstm_norm'].shape[1] // heads
    dk = dv // 2
    qw, vw = heads * dk, heads * dv
    W = p['a_conv_w'].shape[2]
    w_in = p['a_w_in'][li]
    n_qkvo = 2 * qw + 2 * vw
    w_gate = _pad_cols(w_in[:, n_qkvo:n_qkvo + 2 * heads], LANES).astype(BF16)
    w_u = jnp.concatenate([w_in[:, :n_qkvo], w_in[:, n_qkvo + 2 * heads:]], 1).astype(BF16)
    x2 = x.reshape(B * T, D)
    u = matmul(x2, w_u).reshape(B, T, n_qkvo + 2 * W)
    gbias = _pad_cols(jnp.concatenate([p['a_b_ig'][li], p['a_b_fg'][li]])[None, :], LANES)
    hm, c1, n1, m1 = mlstm(u, x, w_gate, gbias, p['a_mlstm_norm'][li][None, :],
                           c0, n0, m0.reshape(B, 1, heads), heads, dk, dv)
    yb, conv1, h1 = lru(u, n_qkvo, conv0, h0.reshape(B, 1, W), p['a_conv_w'][li], p['a_conv_b'][li][None, :],
                        p['a_lru_wa'][li].astype(BF16), p['a_lru_ba'][li][None, :],
                        p['a_lru_wx'][li].astype(BF16), p['a_lru_bx'][li][None, :],
                        p['a_lru_lambda'][li][None, :], reset_first)
    assert vw == W
    y = proj_ln([hm.reshape(B * T, vw), yb.reshape(B * T, W)], None, p['a_w_out'][li].astype(BF16), x2,
                p['ln1_g'][2 * li], p['ln1_b'][2 * li], alpha)
    return y.reshape(B, T, D), (c1, n1, m1.reshape(B, heads), conv1, h1.reshape(B, W))


def _layer_c(x, shift, s0, p, li, layer, alpha):
    B, T, D = x.shape
    H, N = p['c_r_k'].shape[1:]
    M = B * T
    bl = LANES // H
    G = B // bl
    nb = N // SUBLANES
    def cols(w):
        return w.reshape(w.shape[0], H, N).swapaxes(1, 2).reshape(w.shape[0], D)

    w_r = cols(p['c_w_r'][li].astype(BF16))
    w_k = cols(p['c_w_k'][li].astype(BF16))
    w_v = cols(p['c_w_v'][li].astype(BF16))
    rd = -(-p['c_w1'].shape[2] // LANES) * LANES
    ra = -(-p['c_a1'].shape[2] // LANES) * LANES
    w_d = (_pad_cols(p['c_w1'][li], rd).astype(BF16), _pad_rows(cols(p['c_w2'][li]), rd).astype(BF16),
           cols(p['c_w0'][li][None, :]))
    w_a = (_pad_cols(p['c_a1'][li], ra).astype(BF16), _pad_rows(cols(p['c_a2'][li]), ra).astype(BF16),
           cols(p['c_a0'][li][None, :]))
    w_g = (p['c_g1'][li].astype(BF16), cols(p['c_g2'][li].astype(BF16)), jnp.zeros((1, D), F32))
    fused = T % LANES == 0
    mix = functools.partial(mix_proj, x, shift.reshape(B, 1, D), p['c_mu'][li])
    r = mix(0, (w_r,), H, fused)
    k = mix(2, (w_k,), H, fused)
    v = mix(3, (w_v,), H, fused)
    decay = mix(1, w_d, H, fused, ("tanh", "decay"))
    a = mix(4, w_a, H, fused, ("none", "sigmoid"))
    g = mix(5, w_g, H, False, ("sigmoid", "none"))
    if not fused:
        def to_scan(z):
            return z.reshape(G, bl, T, N, H).transpose(0, 3, 2, 1, 4).reshape(G, N, T, LANES)

        r, k, v, decay, a = (to_scan(z) for z in (r, k, v, decay, a))

    def par(z):
        return jnp.tile(z.reshape(H, N).T, (1, bl))

    s0t = s0.reshape(G, bl, H, nb, SUBLANES, N).transpose(0, 3, 5, 4, 1, 2).reshape(G, nb, N, SUBLANES, LANES)
    yt, s1t = rwkv_scan(r, k, v, decay, a,
                        par(p['c_k_k'][li]), par(p['c_k_a'][li]), par(p['c_r_k'][li].reshape(D)),
                        par(p['c_gn_g'][li]), par(p['c_gn_b'][li]), s0t)
    s1 = s1t.reshape(G, nb, N, SUBLANES, bl, H).transpose(0, 4, 5, 1, 3, 2).reshape(B, H, N, N)
    w_o = p['c_w_o'][li].astype(BF16).reshape(H, N, D).swapaxes(0, 1).reshape(D, D)
    if fused:
        out = scan_proj_ln(yt, g, w_o, x, p['ln1_g'][layer], p['ln1_b'][layer], alpha, H)
    else:
        y = yt.reshape(G, N, T, bl, H).transpose(0, 3, 2, 1, 4).reshape(M, D)
        out = proj_ln([y], g.reshape(M, D), w_o, x.reshape(M, D), p['ln1_g'][layer], p['ln1_b'][layer], alpha)
    return out.reshape(B, T, D), (x[:, -1], s1)


def _trunk(x, states, p, reset_first):
    mC, mn, mm, cv, hl, sh, S = states
    depth = p['ln1_g'].shape[0]
    alpha = (2 * depth) ** 0.25
    B, T, D = x.shape
    new_a, new_c = [], []
    for layer in range(depth):
        li = layer // 2
        if layer % 2 == 0:
            x, st = _layer_a(x, (mC[li], mn[li], mm[li], cv[li], hl[li]), p, li, reset_first, alpha)
            new_a.append(st)
        else:
            x, st = _layer_c(x, sh[li], S[li], p, li, layer, alpha)
            new_c.append(st)
        x = mlp_ln(x.reshape(B * T, D), p['mlp_w1'].astype(BF16), p['mlp_w2'].astype(BF16), layer,
                   p['ln2_g'][layer], p['ln2_b'][layer], alpha).reshape(B, T, D)
    sa = [jnp.stack([s[j] for s in new_a]) for j in range(5)]
    sc = [jnp.stack([s[j] for s in new_c]) for j in range(2)]
    return x, sa + sc


def kernel(x_prompt, x_sample, state_mlstm_C, state_mlstm_n, state_mlstm_m, state_lru_conv, state_lru_h,
           state_rwkv_shift, state_rwkv_S, a_w_in, a_b_ig, a_b_fg, a_mlstm_norm, a_conv_w, a_conv_b,
           a_lru_wa, a_lru_ba, a_lru_wx, a_lru_bx, a_lru_lambda, a_w_out, c_mu, c_w_r, c_w_k, c_w_v,
           c_w0, c_w1, c_w2, c_a0, c_a1, c_a2, c_g1, c_g2, c_k_k, c_k_a, c_r_k, c_gn_g, c_gn_b, c_w_o,
           ln1_g, ln1_b, ln2_g, ln2_b, mlp_w1, mlp_w2):
    p = dict(a_w_in=a_w_in, a_b_ig=a_b_ig, a_b_fg=a_b_fg, a_mlstm_norm=a_mlstm_norm, a_conv_w=a_conv_w,
             a_conv_b=a_conv_b, a_lru_wa=a_lru_wa, a_lru_ba=a_lru_ba, a_lru_wx=a_lru_wx, a_lru_bx=a_lru_bx,
             a_lru_lambda=a_lru_lambda, a_w_out=a_w_out, c_mu=c_mu, c_w_r=c_w_r, c_w_k=c_w_k, c_w_v=c_w_v,
             c_w0=c_w0, c_w1=c_w1, c_w2=c_w2, c_a0=c_a0, c_a1=c_a1, c_a2=c_a2, c_g1=c_g1, c_g2=c_g2,
             c_k_k=c_k_k, c_k_a=c_k_a, c_r_k=c_r_k, c_gn_g=c_gn_g, c_gn_b=c_gn_b, c_w_o=c_w_o,
             ln1_g=ln1_g, ln1_b=ln1_b, ln2_g=ln2_g, ln2_b=ln2_b, mlp_w1=mlp_w1, mlp_w2=mlp_w2)
    Bp = x_prompt.shape[0]
    init = tuple(jnp.zeros((s.shape[0], Bp) + s.shape[2:], s.dtype)
                 for s in (state_mlstm_C, state_mlstm_n, state_mlstm_m, state_lru_conv, state_lru_h,
                           state_rwkv_shift, state_rwkv_S))
    y_prompt, ps = _trunk(x_prompt, init, p, True)
    y_sample, ss = _trunk(x_sample, (state_mlstm_C, state_mlstm_n, state_mlstm_m, state_lru_conv,
                                     state_lru_h, state_rwkv_shift, state_rwkv_S), p, False)
    return (y_prompt, y_sample, *ps, *ss)
```

```python
import functools

import jax
import jax.numpy as jnp
from jax import lax
from jax.experimental import pallas as pl
from jax.experimental.pallas import tpu as pltpu

F32 = jnp.float32
BF16 = jnp.bfloat16

LANES = 128
SUBLANES = 8
VMEM_LIMIT_BYTES = 56 * 1024 * 1024

CHUNK = 256
LRU_C = 8.0
LN_EPS = 1e-5
RWKV_GN_EPS = 64e-5
HIGHEST = lax.Precision.HIGHEST
EXP_NEG_HALF = 0.6065306597126334


def _params(*sem):
    return pltpu.CompilerParams(dimension_semantics=sem, vmem_limit_bytes=VMEM_LIMIT_BYTES)


def _tile(n, pref):
    t = min(n, pref)
    while n % t:
        t -= 1
    return t


def _sigmoid(x):
    return 1.0 / (1.0 + jnp.exp(-x))


def _softplus(x):
    return jnp.maximum(x, 0.0) + jnp.log1p(jnp.exp(-jnp.abs(x)))


def _layer_norm(y, g, b):
    mu = jnp.mean(y, -1, keepdims=True)
    d = y - mu
    var = jnp.mean(d * d, -1, keepdims=True)
    return d * lax.rsqrt(var + LN_EPS) * g + b


def _mm_kernel(a_ref, w_ref, o_ref, ab_ref):
    @pl.when(pl.program_id(1) == 0)
    def _():
        ab_ref[...] = a_ref[...].astype(BF16)

    o_ref[...] = jnp.dot(ab_ref[...], w_ref[...], preferred_element_type=F32)


def matmul(a, w, tm=1024, tn=1024):
    M, K = a.shape
    N = w.shape[1]
    tm, tn = _tile(M, tm), _tile(N, tn)
    return pl.pallas_call(
        _mm_kernel,
        grid=(M // tm, N // tn),
        in_specs=[pl.BlockSpec((tm, K), lambda i, j: (i, 0)),
                  pl.BlockSpec((K, tn), lambda i, j: (0, j))],
        out_specs=pl.BlockSpec((tm, tn), lambda i, j: (i, j)),
        out_shape=jax.ShapeDtypeStruct((M, N), F32),
        scratch_shapes=[pltpu.VMEM((tm, K), BF16)],
        compiler_params=_params("parallel", "arbitrary"),
        name="matmul",
    )(a, w)


def _proj_ln_kernel(*refs, n_a, alpha, gated):
    a_refs, rest = refs[:n_a], list(refs[n_a:])
    gate_ref = rest.pop(0) if gated else None
    w_ref, x_ref, g_ref, b_ref, o_ref = rest
    a = jnp.concatenate([r[...] for r in a_refs], 1) if n_a > 1 else a_refs[0][...]
    if gated:
        a = a * gate_ref[...]
    acc = jnp.dot(a.astype(BF16), w_ref[...], preferred_element_type=F32)
    o_ref[...] = _layer_norm(alpha * x_ref[...] + acc, g_ref[...], b_ref[...])


PROJ_VMEM_BUDGET = 46 * 1024 * 1024


def proj_ln(a_list, gate, w, x, g, b, alpha):
    n_a = len(a_list)
    M, K = a_list[0].shape
    D = w.shape[1]
    assert w.shape[0] == n_a * K and not (gate is not None and n_a > 1)
    gated = gate is not None

    def vmem_bytes(tm):
        blocks = sum(tm * K * a.dtype.itemsize for a in a_list) + gated * tm * K * 4 + 2 * tm * D * 4
        return 2 * blocks + w.size * w.dtype.itemsize

    tm = next(t for t in (512, 256, 128, 64, 32, 16, 8) if M % t == 0 and vmem_bytes(t) <= PROJ_VMEM_BUDGET)
    a_spec = pl.BlockSpec((tm, K), lambda i: (i, 0))
    row_spec = pl.BlockSpec((tm, D), lambda i: (i, 0))
    vec_spec = pl.BlockSpec((1, D), lambda i: (0, 0))
    in_specs = [a_spec] * n_a + [a_spec] * gated
    in_specs += [pl.BlockSpec(w.shape, lambda i: (0, 0), pipeline_mode=pl.Buffered(1)),
                 row_spec, vec_spec, vec_spec]
    args = list(a_list) + [gate] * gated + [w, x, g.reshape(1, D), b.reshape(1, D)]
    return pl.pallas_call(
        functools.partial(_proj_ln_kernel, n_a=n_a, alpha=alpha, gated=gated),
        grid=(M // tm,),
        in_specs=in_specs,
        out_specs=row_spec,
        out_shape=jax.ShapeDtypeStruct((M, D), F32),
        compiler_params=_params("parallel"),
        name="proj_ln",
    )(*args)


def _mlp_ln_kernel(x_ref, w1_ref, w2_ref, g_ref, b_ref, o_ref, xb_ref, *, nf, alpha):
    f = pl.program_id(1)

    @pl.when(f == 0)
    def _():
        o_ref[...] = jnp.zeros_like(o_ref)
        xb_ref[...] = x_ref[...].astype(BF16)

    h = jnp.maximum(jnp.dot(xb_ref[...], w1_ref[0], preferred_element_type=F32), 0.0)
    o_ref[...] += jnp.dot((h * h).astype(BF16), w2_ref[0], preferred_element_type=F32)

    @pl.when(f == nf - 1)
    def _():
        y = alpha * x_ref[...] + o_ref[...]
        o_ref[...] = _layer_norm(y, g_ref[...], b_ref[...])


def mlp_ln(x, w1, w2, layer, g, b, alpha, tm=1024, tf=512):
    M, D = x.shape
    FF = w1.shape[2]
    tm, tf = _tile(M, tm), _tile(FF, tf)
    nf = FF // tf
    row_spec = pl.BlockSpec((tm, D), lambda i, f: (i, 0))
    vec_spec = pl.BlockSpec((1, D), lambda i, f: (0, 0))
    return pl.pallas_call(
        functools.partial(_mlp_ln_kernel, nf=nf, alpha=alpha),
        grid=(M // tm, nf),
        in_specs=[row_spec,
                  pl.BlockSpec((1, D, tf), lambda i, f: (layer, 0, f)),
                  pl.BlockSpec((1, tf, D), lambda i, f: (layer, f, 0)),
                  vec_spec, vec_spec],
        out_specs=row_spec,
        out_shape=jax.ShapeDtypeStruct((M, D), F32),
        scratch_shapes=[pltpu.VMEM((tm, D), BF16)],
        compiler_params=_params("parallel", "arbitrary"),
        name="mlp_ln",
    )(x, w1, w2, g.reshape(1, D), b.reshape(1, D))


def _mlstm_kernel(q_ref, k_ref, v_ref, o_ref, x_ref, wg_ref, gbias_ref, nw_ref, c0_ref, n0_ref, m0_ref,
                  h_ref, c1_ref, n1_ref, m1_ref, *, heads, dk, dv, L):
    t = pl.program_id(1)

    @pl.when(t == 0)
    def _():
        c1_ref[...] = c0_ref[...]
        n1_ref[...] = n0_ref[...]
        m1_ref[...] = m0_ref[...]

    row = lax.broadcasted_iota(jnp.int32, (L, L), 0)
    col = lax.broadcasted_iota(jnp.int32, (L, L), 1)
    causal = row >= col
    tril = causal.astype(F32)
    sel_r = lax.broadcasted_iota(jnp.int32, (2 * heads, LANES), 0)
    sel_c = lax.broadcasted_iota(jnp.int32, (2 * heads, LANES), 1)
    pick = (sel_r == sel_c).astype(F32)
    lane = lax.broadcasted_iota(jnp.int32, (L, LANES), 1)
    scale = dk ** -0.5

    gt = jnp.dot(x_ref[0].astype(BF16), wg_ref[...], preferred_element_type=F32) + gbias_ref[...]
    logf = jnp.minimum(gt, 0.0) - jnp.log1p(jnp.exp(-jnp.abs(gt)))
    gl = jnp.where(lane < heads, gt, logf)
    cum = jnp.dot(tril, gl, preferred_element_type=F32, precision=HIGHEST)
    nt_dims = (((1,), (1,)), ((), ()))
    gl_t = lax.dot_general(pick, gl, nt_dims, preferred_element_type=F32, precision=HIGHEST)
    cum_t = lax.dot_general(pick, cum, nt_dims, preferred_element_type=F32, precision=HIGHEST)
    for h in range(heads):
        ig_col = gl[:, h:h + 1]
        b_col = cum[:, heads + h:heads + h + 1]
        ig_row = gl_t[h:h + 1, :]
        b_row = cum_t[heads + h:heads + h + 1, :]
        b_last = b_row[:, L - 1:L]
        m_prev = m1_ref[0, :, h:h + 1]
        qh = (q_ref[0, :, h * dk:(h + 1) * dk] * scale).astype(BF16)
        kf = k_ref[0, :, h * dk:(h + 1) * dk]
        kh = kf.astype(BF16)
        vh = v_ref[0, :, h * dv:(h + 1) * dv].astype(BF16)
        c_prev = c1_ref[0, h]
        n_prev = n1_ref[0, h:h + 1, :]

        dmat = jnp.where(causal, b_col - b_row + ig_row, -jnp.inf)
        inter = b_col + m_prev
        m_t = jnp.maximum(inter, jnp.max(dmat, -1, keepdims=True))
        w_intra = jnp.exp(dmat - m_t)
        w_inter = jnp.exp(inter - m_t)
        s = lax.dot_general(qh, kh, nt_dims, preferred_element_type=F32)
        qk = s * w_intra
        num = jnp.dot(qk.astype(BF16), vh, preferred_element_type=F32)
        num = num + w_inter * jnp.dot(qh, c_prev.astype(BF16), preferred_element_type=F32)
        qn = jnp.sum(qh.astype(F32) * n_prev, -1, keepdims=True)
        den = jnp.sum(qk, -1, keepdims=True) + w_inter * qn
        hh = num / jnp.maximum(jnp.abs(den), jnp.exp(-m_t))

        mu = jnp.mean(hh, -1, keepdims=True)
        d = hh - mu
        var = jnp.mean(d * d, -1, keepdims=True)
        hn = d * lax.rsqrt(var + LN_EPS)
        og = o_ref[0, :, h * dv:(h + 1) * dv]
        hg = hn * nw_ref[:, h * dv:(h + 1) * dv] * _sigmoid(og)
        h_ref[0, :, h * dv:(h + 1) * dv] = hg.astype(h_ref.dtype)

        lw_col = b_last - b_col + ig_col
        lw_row = b_last - b_row + ig_row
        m_new = jnp.maximum(b_last + m_prev, jnp.max(lw_row, -1, keepdims=True))
        ws_col = jnp.exp(lw_col - m_new)
        wc = jnp.exp(b_last + m_prev - m_new)
        kw = kf * ws_col
        tn_dims = (((0,), (0,)), ((), ()))
        c1_ref[0, h] = wc * c_prev + lax.dot_general(kw.astype(BF16), vh, tn_dims,
                                                     preferred_element_type=F32)
        n1_ref[0, h:h + 1, :] = wc * n_prev + jnp.sum(kw, 0, keepdims=True)
        m1_ref[0, :, h:h + 1] = m_new


def mlstm(u, x, w_gate, gbias, norm_w, c0, n0, m0, heads, dk, dv):
    B, T, D = x.shape
    L = min(T, CHUNK)
    assert T % L == 0
    tb = L
    qw, vw = heads * dk, heads * dv
    assert qw % LANES == 0 and vw == 2 * qw
    st = lambda i, t: (i, 0, 0)
    kern = functools.partial(_mlstm_kernel, heads=heads, dk=dk, dv=dv, L=L)
    return pl.pallas_call(
        kern,
        grid=(B, T // tb),
        in_specs=[pl.BlockSpec((1, tb, qw), lambda i, t: (i, t, 0)),
                  pl.BlockSpec((1, tb, qw), lambda i, t: (i, t, 1)),
                  pl.BlockSpec((1, tb, vw), lambda i, t: (i, t, 1)),
                  pl.BlockSpec((1, tb, vw), lambda i, t: (i, t, 2)),
                  pl.BlockSpec((1, tb, D), lambda i, t: (i, t, 0)),
                  pl.BlockSpec((D, LANES), lambda i, t: (0, 0)),
                  pl.BlockSpec((1, LANES), lambda i, t: (0, 0)),
                  pl.BlockSpec((1, vw), lambda i, t: (0, 0)),
                  pl.BlockSpec((1, heads, dk, dv), lambda i, t: (i, 0, 0, 0)),
                  pl.BlockSpec((1, heads, dk), st),
                  pl.BlockSpec((1, 1, heads), st)],
        out_specs=[pl.BlockSpec((1, tb, vw), lambda i, t: (i, t, 0)),
                   pl.BlockSpec((1, heads, dk, dv), lambda i, t: (i, 0, 0, 0)),
                   pl.BlockSpec((1, heads, dk), st),
                   pl.BlockSpec((1, 1, heads), st)],
        out_shape=[jax.ShapeDtypeStruct((B, T, vw), BF16),
                   jax.ShapeDtypeStruct((B, heads, dk, dv), F32),
                   jax.ShapeDtypeStruct((B, heads, dk), F32),
                   jax.ShapeDtypeStruct((B, 1, heads), F32)],
        compiler_params=_params("parallel", "arbitrary"),
        name="mlstm",
    )(u, u, u, u, x, w_gate, gbias, norm_w, c0, n0, m0)


def _gelu_tanh(x):
    return 0.5 * x * (1.0 + jnp.tanh(0.7978845608028654 * (x + 0.044715 * x * x * x)))


def _lru_kernel(xr_ref, yg_ref, conv0_ref, h0_ref, cw_ref, cb_ref, wa_ref, ba_ref, wx_ref, bx_ref, lam_ref,
                y_ref, conv1_ref, h1_ref, xp_ref, a_ref, b_ref, *, tb, blocks, cw, reset_first):
    t = pl.program_id(1)
    halo = cw - 1

    @pl.when(t == 0)
    def _():
        xp_ref[...] = jnp.zeros_like(xp_ref)
        xp_ref[SUBLANES - halo:, :] = conv0_ref[0]
        h1_ref[0] = h0_ref[0]

    x = xr_ref[0]
    prev = xp_ref[...]
    head_rows = lax.broadcasted_iota(jnp.int32, prev.shape, 0)
    xc = cb_ref[...] + x * cw_ref[cw - 1:cw, :]
    for d in range(1, cw):
        xs = pltpu.roll(x, d, 0)
        first = jnp.where(head_rows < d, pltpu.roll(prev, d, 0), xs[:SUBLANES])
        xs = jnp.concatenate([first, xs[SUBLANES:]], 0)
        xc = xc + xs * cw_ref[cw - 1 - d:cw - d, :]

    sp = _softplus(-lam_ref[...])
    bw = xc.shape[1] // blocks
    for g in range(blocks):
        sl = slice(g * bw, (g + 1) * bw)
        xg = xc[:, sl]
        xgb = xg.astype(BF16)
        gr = _sigmoid(jnp.dot(xgb, wa_ref[g], preferred_element_type=F32) + ba_ref[:, sl])
        gi = _sigmoid(jnp.dot(xgb, wx_ref[g], preferred_element_type=F32) + bx_ref[:, sl])
        log_a = -LRU_C * gr * sp[:, sl]
        th = jnp.tanh(log_a)
        z = -2.0 * th / (1.0 - th)
        mult = jnp.where(z > 0.0, z * lax.rsqrt(z), 0.0)
        if reset_first:
            first = (lax.broadcasted_iota(jnp.int32, mult.shape, 0) == 0) & (t == 0)
            mult = jnp.where(first, 1.0, mult)
        a_ref[:, sl] = jnp.exp(log_a)
        b_ref[:, sl] = mult * gi * xg

    def step(i, h):
        h = a_ref[pl.ds(i, 1), :] * h + b_ref[pl.ds(i, 1), :]
        b_ref[pl.ds(i, 1), :] = h
        return h

    h_last = lax.fori_loop(0, tb, step, h1_ref[0], unroll=8)
    h1_ref[0] = h_last
    y_ref[0] = (b_ref[...] * _gelu_tanh(yg_ref[0])).astype(y_ref.dtype)
    last = xr_ref[0, tb - SUBLANES:, :]
    xp_ref[...] = last
    conv1_ref[0] = last[SUBLANES - halo:, :]


def lru(u, col0, conv0, h0, conv_w, conv_b, wa, ba, wx, bx, lam, reset_first):
    B, T, _ = u.shape
    W = conv_w.shape[1]
    assert col0 % W == 0
    cb0 = col0 // W
    cw = conv_w.shape[0]
    blocks = wa.shape[0]
    tb = _tile(T, 256)
    assert tb >= cw - 1 and cw - 1 <= SUBLANES
    st = lambda i, t: (i, 0, 0)
    vec = pl.BlockSpec((1, W), lambda i, t: (0, 0))
    wsp = pl.BlockSpec(wa.shape, lambda i, t: (0, 0, 0))
    kern = functools.partial(_lru_kernel, tb=tb, blocks=blocks, cw=cw, reset_first=reset_first)
    return pl.pallas_call(
        kern,
        grid=(B, T // tb),
        in_specs=[pl.BlockSpec((1, tb, W), lambda i, t: (i, t, cb0)),
                  pl.BlockSpec((1, tb, W), lambda i, t: (i, t, cb0 + 1)),
                  pl.BlockSpec((1, cw - 1, W), st),
                  pl.BlockSpec((1, 1, W), st),
                  pl.BlockSpec((cw, W), lambda i, t: (0, 0)),
                  vec, wsp, vec, wsp, vec, vec],
        out_specs=[pl.BlockSpec((1, tb, W), lambda i, t: (i, t, 0)),
                   pl.BlockSpec((1, cw - 1, W), st),
                   pl.BlockSpec((1, 1, W), st)],
        out_shape=[jax.ShapeDtypeStruct((B, T, W), BF16),
                   jax.ShapeDtypeStruct((B, cw - 1, W), F32),
                   jax.ShapeDtypeStruct((B, 1, W), F32)],
        scratch_shapes=[pltpu.VMEM((SUBLANES, W), F32),
                        pltpu.VMEM((tb, W), F32),
                        pltpu.VMEM((tb, W), F32)],
        compiler_params=_params("parallel", "arbitrary"),
        name="lru",
    )(u, u, conv0, h0, conv_w, conv_b, wa, ba, wx, bx, lam)


def _lora_math(x, w1_ref, w2_ref, bias_ref, mid, post):
    z = jnp.dot(x, w1_ref[...], preferred_element_type=F32)
    if mid == "tanh":
        z = jnp.tanh(z)
    elif mid == "sigmoid":
        z = _sigmoid(z)
    y = jnp.dot(z.astype(BF16), w2_ref[...], preferred_element_type=F32)
    if post == "decay":
        y = jnp.exp(-EXP_NEG_HALF * _sigmoid(bias_ref[...] + y))
    elif post == "sigmoid":
        y = _sigmoid(bias_ref[...] + y)
    return y


def _rwkv_scan_kernel(r_ref, k_ref, v_ref, w_ref, a_ref, kk_p, ka_p, rk_p, gg_p, gb_p, s0_ref,
                      y_ref, s1_ref, nkk_s, kka_s, km_s, vt_s, ys_s, *, tt, n):
    t = pl.program_id(1)
    nb = n // SUBLANES

    @pl.when(t == 0)
    def _():
        s1_ref[...] = s0_ref[...]

    def prow(p_ref, f):
        return p_ref[f:f + 1, :]

    nrm = jnp.zeros((tt, LANES), F32)
    for f in range(n):
        kk = k_ref[0, f] * prow(kk_p, f)
        nrm = nrm + kk * kk
    inv = lax.rsqrt(jnp.maximum(nrm, 1e-24))
    for f in range(n):
        kf, af = k_ref[0, f], a_ref[0, f]
        kk = kf * prow(kk_p, f) * inv
        nkk_s[f] = -kk
        kka_s[f] = kk * af
        km_s[f] = kf * (1.0 + (af - 1.0) * prow(ka_p, f))

    vt_s[...] = jnp.swapaxes(v_ref[0], 0, 1)
    zeros = tuple(jnp.zeros((SUBLANES, LANES), F32) for _ in range(nb))

    def reduce_keys(kx, acc):
        nk = nkk_s[kx, pl.ds(0, 1), :]
        return tuple(acc[jb] + s1_ref[0, jb, kx] * nk for jb in range(nb))

    sa0 = lax.fori_loop(0, n, reduce_keys, zeros, unroll=8)

    def time_step(i, sa):
        row = pl.ds(i, 1)
        nxt = pl.ds(jnp.minimum(i + 1, tt - 1), 1)
        vt = [vt_s[i, jb * SUBLANES:(jb + 1) * SUBLANES, :] for jb in range(nb)]

        def update_keys(kx, acc):
            wr = w_ref[0, kx, row, :]
            ar = kka_s[kx, row, :]
            mr = km_s[kx, row, :]
            rr = r_ref[0, kx, row, :]
            nk = nkk_s[kx, nxt, :]
            ys, sn = [], []
            for jb in range(nb):
                s = s1_ref[0, jb, kx] * wr + sa[jb] * ar + vt[jb] * mr
                s1_ref[0, jb, kx] = s
                ys.append(acc[jb] + s * rr)
                sn.append(acc[nb + jb] + s * nk)
            return tuple(ys + sn)

        acc = lax.fori_loop(0, n, update_keys, zeros + zeros, unroll=32)
        for jb in range(nb):
            ys_s[i, jb * SUBLANES:(jb + 1) * SUBLANES, :] = acc[jb]
        return tuple(acc[nb:])

    lax.fori_loop(0, tt, time_step, sa0)
    y_ref[0] = jnp.swapaxes(ys_s[...], 0, 1)

    mu = jnp.zeros((tt, LANES), F32)
    cb = jnp.zeros((tt, LANES), F32)
    for f in range(n):
        mu = mu + y_ref[0, f]
        cb = cb + r_ref[0, f] * km_s[f] * prow(rk_p, f)
    mu = mu * (1.0 / n)
    var = jnp.zeros((tt, LANES), F32)
    for f in range(n):
        d = y_ref[0, f] - mu
        var = var + d * d
    rs = lax.rsqrt(var * (1.0 / n) + RWKV_GN_EPS)
    for f in range(n):
        y_ref[0, f] = (y_ref[0, f] - mu) * rs * prow(gg_p, f) + prow(gb_p, f) + cb * v_ref[0, f]


def rwkv_scan(r, k, v, w, a, kk_p, ka_p, rk_p, gg_p, gb_p, s0):
    G, n, T, _ = r.shape
    tt = _tile(T, 64)
    seq = pl.BlockSpec((1, n, tt, LANES), lambda g, t: (g, 0, t, 0))
    par = pl.BlockSpec((n, LANES), lambda g, t: (0, 0))
    st = pl.BlockSpec((1, n // SUBLANES, n, SUBLANES, LANES), lambda g, t: (g, 0, 0, 0, 0))
    return pl.pallas_call(
        functools.partial(_rwkv_scan_kernel, tt=tt, n=n),
        grid=(G, T // tt),
        in_specs=[seq] * 5 + [par] * 5 + [st],
        out_specs=[seq, st],
        out_shape=[jax.ShapeDtypeStruct((G, n, T, LANES), F32),
                   jax.ShapeDtypeStruct(s0.shape, F32)],
        scratch_shapes=[pltpu.VMEM((n, tt, LANES), F32)] * 3 + [pltpu.VMEM((tt, n, LANES), F32)] * 2,
        compiler_params=_params("parallel", "arbitrary"),
        name="rwkv_scan",
    )(r, k, v, w, a, kk_p, ka_p, rk_p, gg_p, gb_p, s0)


def _store_scan_layout(res, z_ref, heads, c0=0):
    bl = LANES // heads
    for c in range(res.shape[1] // LANES):
        sub = [res[s * LANES:(s + 1) * LANES, c * LANES:(c + 1) * LANES].T for s in range(bl)]
        for ni in range(bl):
            tile = jnp.concatenate([sub[s][ni * heads:(ni + 1) * heads, :] for s in range(bl)], 0)
            z_ref[0, (c0 + c) * bl + ni] = tile.T


MXU_WIDTH = 256


def _mixed_tokens(x_ref, prev_ref, shift_ref, mu_ref):
    x = x_ref[...]
    bl, tq, D = x.shape
    prev = jnp.where(pl.program_id(1) == 0, shift_ref[...], prev_ref[:, SUBLANES - 1:, :])
    xs = pltpu.roll(x, 1, 1)
    head_rows = lax.broadcasted_iota(jnp.int32, (bl, SUBLANES, D), 1)
    first = jnp.where(head_rows == 0, prev, xs[:, :SUBLANES, :])
    x_prev = jnp.concatenate([first, xs[:, SUBLANES:, :]], 1)
    xm = x + (x_prev - x) * mu_ref[0]
    return xm.astype(BF16).reshape(bl * tq, D)


def _mix_proj_kernel(x_ref, prev_ref, shift_ref, mu_ref, *refs, heads, lora_args, scan_out):
    o_ref = refs[-1]
    bl, tq, _ = x_ref.shape
    a = _mixed_tokens(x_ref, prev_ref, shift_ref, mu_ref)
    if lora_args is not None:
        res = _lora_math(a, *refs[:3], *lora_args)
        if scan_out:
            _store_scan_layout(res, o_ref, heads)
        else:
            o_ref[...] = res.reshape(bl, tq, res.shape[1])
    elif scan_out:
        w_ref = refs[0]
        for c in range(w_ref.shape[1] // MXU_WIDTH):
            cols = slice(c * MXU_WIDTH, (c + 1) * MXU_WIDTH)
            res = jnp.dot(a, w_ref[:, cols], preferred_element_type=F32)
            _store_scan_layout(res, o_ref, heads, c * (MXU_WIDTH // LANES))
    else:
        res = jnp.dot(a, refs[0][...], preferred_element_type=F32)
        o_ref[...] = res.reshape(bl, tq, res.shape[1])


def mix_proj(x, shift, mu, j, weights, heads, scan_out, lora_args=None):
    B, T, D = x.shape
    N = weights[-1].shape[1]
    bl = LANES // heads
    tq = min(T, LANES)
    assert T % tq == 0 and tq % SUBLANES == 0 and not (scan_out and tq != LANES)
    steps = tq // SUBLANES
    const = lambda w: pl.BlockSpec(w.shape, lambda g, t: (0,) * w.ndim, pipeline_mode=pl.Buffered(1))
    if scan_out:
        out_spec = pl.BlockSpec((1, N // heads, LANES, LANES), lambda g, t: (g, 0, t, 0))
        out_shape = jax.ShapeDtypeStruct((B // bl, N // heads, T, LANES), F32)
    else:
        out_spec = pl.BlockSpec((bl, tq, N), lambda g, t: (g, t, 0))
        out_shape = jax.ShapeDtypeStruct((B, T, N), F32)
    return pl.pallas_call(
        functools.partial(_mix_proj_kernel, heads=heads, lora_args=lora_args, scan_out=scan_out),
        grid=(B // bl, T // tq),
        in_specs=[pl.BlockSpec((bl, tq, D), lambda g, t: (g, t, 0)),
                  pl.BlockSpec((bl, SUBLANES, D), lambda g, t: (g, jnp.maximum(t * steps - 1, 0), 0)),
                  pl.BlockSpec((bl, 1, D), lambda g, t: (g, 0, 0)),
                  pl.BlockSpec((1, 1, D), lambda g, t: (j, 0, 0))] + [const(w) for w in weights],
        out_specs=out_spec,
        out_shape=out_shape,
        compiler_params=_params("parallel", "arbitrary"),
        name="mix_proj",
    )(x, x, shift, mu.reshape(mu.shape[0], 1, D), *weights)


def _scan_proj_ln_kernel(y_ref, gate_ref, w_ref, x_ref, g_ref, b_ref, o_ref, *, heads, alpha):
    bl = LANES // heads
    D = o_ref.shape[2]
    per = MXU_WIDTH // LANES
    acc = None
    for kc in range(D // MXU_WIDTH):
        parts = []
        for c in range(kc * per, (kc + 1) * per):
            sub = [y_ref[0, c * bl + ni].T for ni in range(bl)]
            cols = slice(c * LANES, (c + 1) * LANES)
            tiles = []
            for s in range(bl):
                tile = jnp.concatenate([sub[ni][s * heads:(s + 1) * heads, :] for ni in range(bl)], 0)
                tiles.append((tile.T * gate_ref[s, :, cols]).astype(BF16))
            parts.append(jnp.concatenate(tiles, 0))
        a = jnp.concatenate(parts, 1)
        part = jnp.dot(a, w_ref[kc * MXU_WIDTH:(kc + 1) * MXU_WIDTH, :], preferred_element_type=F32)
        acc = part if acc is None else acc + part
    y = alpha * x_ref[...].reshape(bl * LANES, D) + acc
    o_ref[...] = _layer_norm(y, g_ref[...], b_ref[...]).reshape(bl, LANES, D)


def scan_proj_ln(y, gate, w, x, g, b, alpha, heads):
    G, n, T, _ = y.shape
    B, _, D = x.shape
    bl = LANES // heads
    tok = pl.BlockSpec((bl, LANES, D), lambda g_, t: (g_, t, 0))
    vec = pl.BlockSpec((1, D), lambda g_, t: (0, 0))
    return pl.pallas_call(
        functools.partial(_scan_proj_ln_kernel, heads=heads, alpha=alpha),
        grid=(G, T // LANES),
        in_specs=[pl.BlockSpec((1, n, LANES, LANES), lambda g_, t: (g_, 0, t, 0)), tok,
                  pl.BlockSpec((D, D), lambda g_, t: (0, 0), pipeline_mode=pl.Buffered(1)),
                  tok, vec, vec],
        out_specs=tok,
        out_shape=jax.ShapeDtypeStruct((B, T, D), F32),
        compiler_params=_params("parallel", "parallel"),
        name="scan_proj_ln",
    )(y, gate, w, x, g.reshape(1, D), b.reshape(1, D))


def _pad_cols(w, n):
    return jnp.pad(w, ((0, 0), (0, n - w.shape[1])))


def _pad_rows(w, n):
    return jnp.pad(w, ((0, n - w.shape[0]), (0, 0)))


def _layer_a(x, st, p, li, reset_first, alpha):
    c0, n0, m0, conv0, h0 = st
    B, T, D = x.shape
    heads = p['a_b_ig'].shape[1]
    dv = p['a_mlstm_norm'].shape[1] // heads
    dk = dv // 2
    qw, vw = heads * dk, heads * dv
    W = p['a_conv_w'].shape[2]
    w_in = p['a_w_in'][li]
    n_qkvo = 2 * qw + 2 * vw
    w_gate = _pad_cols(w_in[:, n_qkvo:n_qkvo + 2 * heads], LANES).astype(BF16)
    w_u = jnp.concatenate([w_in[:, :n_qkvo], w_in[:, n_qkvo + 2 * heads:]], 1).astype(BF16)
    x2 = x.reshape(B * T, D)
    u = matmul(x2, w_u).reshape(B, T, n_qkvo + 2 * W)
    gbias = _pad_cols(jnp.concatenate([p['a_b_ig'][li], p['a_b_fg'][li]])[None, :], LANES)
    hm, c1, n1, m1 = mlstm(u, x, w_gate, gbias, p['a_mlstm_norm'][li][None, :],
                           c0, n0, m0.reshape(B, 1, heads), heads, dk, dv)
    yb, conv1, h1 = lru(u, n_qkvo, conv0, h0.reshape(B, 1, W), p['a_conv_w'][li], p['a_conv_b'][li][None, :],
                        p['a_lru_wa'][li].astype(BF16), p['a_lru_ba'][li][None, :],
                        p['a_lru_wx'][li].astype(BF16), p['a_lru_bx'][li][None, :],
                        p['a_lru_lambda'][li][None, :], reset_first)
    assert vw == W
    y = proj_ln([hm.reshape(B * T, vw), yb.reshape(B * T, W)], None, p['a_w_out'][li].astype(BF16), x2,
                p['ln1_g'][2 * li], p['ln1_b'][2 * li], alpha)
    return y.reshape(B, T, D), (c1, n1, m1.reshape(B, heads), conv1, h1.reshape(B, W))


def _layer_c(x, shift, s0, p, li, layer, alpha):
    B, T, D = x.shape
    H, N = p['c_r_k'].shape[1:]
    M = B * T
    bl = LANES // H
    G = B // bl
    nb = N // SUBLANES
    def cols(w):
        return w.reshape(w.shape[0], H, N).swapaxes(1, 2).reshape(w.shape[0], D)

    w_r = cols(p['c_w_r'][li].astype(BF16))
    w_k = cols(p['c_w_k'][li].astype(BF16))
    w_v = cols(p['c_w_v'][li].astype(BF16))
    rd = -(-p['c_w1'].shape[2] // LANES) * LANES
    ra = -(-p['c_a1'].shape[2] // LANES) * LANES
    w_d = (_pad_cols(p['c_w1'][li], rd).astype(BF16), _pad_rows(cols(p['c_w2'][li]), rd).astype(BF16),
           cols(p['c_w0'][li][None, :]))
    w_a = (_pad_cols(p['c_a1'][li], ra).astype(BF16), _pad_rows(cols(p['c_a2'][li]), ra).astype(BF16),
           cols(p['c_a0'][li][None, :]))
    w_g = (p['c_g1'][li].astype(BF16), cols(p['c_g2'][li].astype(BF16)), jnp.zeros((1, D), F32))
    fused = T % LANES == 0
    mix = functools.partial(mix_proj, x, shift.reshape(B, 1, D), p['c_mu'][li])
    r = mix(0, (w_r,), H, fused)
    k = mix(2, (w_k,), H, fused)
    v = mix(3, (w_v,), H, fused)
    decay = mix(1, w_d, H, fused, ("tanh", "decay"))
    a = mix(4, w_a, H, fused, ("none", "sigmoid"))
    g = mix(5, w_g, H, False, ("sigmoid", "none"))
    if not fused:
        def to_scan(z):
            return z.reshape(G, bl, T, N, H).transpose(0, 3, 2, 1, 4).reshape(G, N, T, LANES)

        r, k, v, decay, a = (to_scan(z) for z in (r, k, v, decay, a))

    def par(z):
        return jnp.tile(z.reshape(H, N).T, (1, bl))

    s0t = s0.reshape(G, bl, H, nb, SUBLANES, N).transpose(0, 3, 5, 4, 1, 2).reshape(G, nb, N, SUBLANES, LANES)
    yt, s1t = rwkv_scan(r, k, v, decay, a,
                        par(p['c_k_k'][li]), par(p['c_k_a'][li]), par(p['c_r_k'][li].reshape(D)),
                        par(p['c_gn_g'][li]), par(p['c_gn_b'][li]), s0t)
    s1 = s1t.reshape(G, nb, N, SUBLANES, bl, H).transpose(0, 4, 5, 1, 3, 2).reshape(B, H, N, N)
    w_o = p['c_w_o'][li].astype(BF16).reshape(H, N, D).swapaxes(0, 1).reshape(D, D)
    if fused:
        out = scan_proj_ln(yt, g, w_o, x, p['ln1_g'][layer], p['ln1_b'][layer], alpha, H)
    else:
        y = yt.reshape(G, N, T, bl, H).transpose(0, 3, 2, 1, 4).reshape(M, D)
        out = proj_ln([y], g.reshape(M, D), w_o, x.reshape(M, D), p['ln1_g'][layer], p['ln1_b'][layer], alpha)
    return out.reshape(B, T, D), (x[:, -1], s1)


def _trunk(x, states, p, reset_first):
    mC, mn, mm, cv, hl, sh, S = states
    depth = p['ln1_g'].shape[0]
    alpha = (2 * depth) ** 0.25
    B, T, D = x.shape
    new_a, new_c = [], []
    for layer in range(depth):
        li = layer // 2
        if layer % 2 == 0:
            x, st = _layer_a(x, (mC[li], mn[li], mm[li], cv[li], hl[li]), p, li, reset_first, alpha)
            new_a.append(st)
        else:
            x, st = _layer_c(x, sh[li], S[li], p, li, layer, alpha)
            new_c.append(st)
        x = mlp_ln(x.reshape(B * T, D), p['mlp_w1'].astype(BF16), p['mlp_w2'].astype(BF16), layer,
                   p['ln2_g'][layer], p['ln2_b'][layer], alpha).reshape(B, T, D)
    sa = [jnp.stack([s[j] for s in new_a]) for j in range(5)]
    sc = [jnp.stack([s[j] for s in new_c]) for j in range(2)]
    return x, sa + sc


def kernel(x_prompt, x_sample, state_mlstm_C, state_mlstm_n, state_mlstm_m, state_lru_conv, state_lru_h,
           state_rwkv_shift, state_rwkv_S, a_w_in, a_b_ig, a_b_fg, a_mlstm_norm, a_conv_w, a_conv_b,
           a_lru_wa, a_lru_ba, a_lru_wx, a_lru_bx, a_lru_lambda, a_w_out, c_mu, c_w_r, c_w_k, c_w_v,
           c_w0, c_w1, c_w2, c_a0, c_a1, c_a2, c_g1, c_g2, c_k_k, c_k_a, c_r_k, c_gn_g, c_gn_b, c_w_o,
           ln1_g, ln1_b, ln2_g, ln2_b, mlp_w1, mlp_w2):
    p = dict(a_w_in=a_w_in, a_b_ig=a_b_ig, a_b_fg=a_b_fg, a_mlstm_norm=a_mlstm_norm, a_conv_w=a_conv_w,
             a_conv_b=a_conv_b, a_lru_wa=a_lru_wa, a_lru_ba=a_lru_ba, a_lru_wx=a_lru_wx, a_lru_bx=a_lru_bx,
             a_lru_lambda=a_lru_lambda, a_w_out=a_w_out, c_mu=c_mu, c_w_r=c_w_r, c_w_k=c_w_k, c_w_v=c_w_v,
             c_w0=c_w0, c_w1=c_w1, c_w2=c_w2, c_a0=c_a0, c_a1=c_a1, c_a2=c_a2, c_g1=c_g1, c_g2=c_g2,
             c_k_k=c_k_k, c_k_a=c_k_a, c_r_k=c_r_k, c_gn_g=c_gn_g, c_gn_b=c_gn_b, c_w_o=c_w_o,
             ln1_g=ln1_g, ln1_b=ln1_b, ln2_g=ln2_g, ln2_b=ln2_b, mlp_w1=mlp_w1, mlp_w2=mlp_w2)
    Bp = x_prompt.shape[0]
    init = tuple(jnp.zeros((s.shape[0], Bp) + s.shape[2:], s.dtype)
                 for s in (state_mlstm_C, state_mlstm_n, state_mlstm_m, state_lru_conv, state_lru_h,
                           state_rwkv_shift, state_rwkv_S))
    y_prompt, ps = _trunk(x_prompt, init, p, True)
    y_sample, ss = _trunk(x_sample, (state_mlstm_C, state_mlstm_n, state_mlstm_m, state_lru_conv,
                                     state_lru_h, state_rwkv_shift, state_rwkv_S), p, False)
    return (y_prompt, y_sample, *ps, *ss)
```

```python
import functools

import jax
import jax.numpy as jnp
from jax import lax
from jax.experimental import pallas as pl
from jax.experimental.pallas import tpu as pltpu

F32 = jnp.float32
BF16 = jnp.bfloat16

LANES = 128
SUBLANES = 8
VMEM_LIMIT_BYTES = 56 * 1024 * 1024

CHUNK = 256
LRU_C = 8.0
LN_EPS = 1e-5
RWKV_GN_EPS = 64e-5
HIGHEST = lax.Precision.HIGHEST
EXP_NEG_HALF = 0.6065306597126334


def _params(*sem):
    return pltpu.CompilerParams(dimension_semantics=sem, vmem_limit_bytes=VMEM_LIMIT_BYTES)


def _tile(n, pref):
    t = min(n, pref)
    while n % t:
        t -= 1
    return t


def _sigmoid(x):
    return 1.0 / (1.0 + jnp.exp(-x))


def _softplus(x):
    return jnp.maximum(x, 0.0) + jnp.log1p(jnp.exp(-jnp.abs(x)))


def _layer_norm(y, g, b):
    mu = jnp.mean(y, -1, keepdims=True)
    d = y - mu
    var = jnp.mean(d * d, -1, keepdims=True)
    return d * lax.rsqrt(var + LN_EPS) * g + b


def _mm_kernel(a_ref, w_ref, o_ref, ab_ref):
    @pl.when(pl.program_id(1) == 0)
    def _():
        ab_ref[...] = a_ref[...].astype(BF16)

    o_ref[...] = jnp.dot(ab_ref[...], w_ref[...], preferred_element_type=F32)


def matmul(a, w, tm=1024, tn=1024):
    M, K = a.shape
    N = w.shape[1]
    tm, tn = _tile(M, tm), _tile(N, tn)
    return pl.pallas_call(
        _mm_kernel,
        grid=(M // tm, N // tn),
        in_specs=[pl.BlockSpec((tm, K), lambda i, j: (i, 0)),
                  pl.BlockSpec((K, tn), lambda i, j: (0, j))],
        out_specs=pl.BlockSpec((tm, tn), lambda i, j: (i, j)),
        out_shape=jax.ShapeDtypeStruct((M, N), F32),
        scratch_shapes=[pltpu.VMEM((tm, K), BF16)],
        compiler_params=_params("parallel", "arbitrary"),
        name="matmul",
    )(a, w)


def _proj_ln_kernel(*refs, n_a, alpha, gated):
    a_refs, rest = refs[:n_a], list(refs[n_a:])
    gate_ref = rest.pop(0) if gated else None
    w_ref, x_ref, g_ref, b_ref, o_ref = rest
    a = jnp.concatenate([r[...] for r in a_refs], 1) if n_a > 1 else a_refs[0][...]
    if gated:
        a = a * gate_ref[...]
    acc = jnp.dot(a.astype(BF16), w_ref[...], preferred_element_type=F32)
    o_ref[...] = _layer_norm(alpha * x_ref[...] + acc, g_ref[...], b_ref[...])


PROJ_VMEM_BUDGET = 46 * 1024 * 1024


def proj_ln(a_list, gate, w, x, g, b, alpha):
    n_a = len(a_list)
    M, K = a_list[0].shape
    D = w.shape[1]
    assert w.shape[0] == n_a * K and not (gate is not None and n_a > 1)
    gated = gate is not None

    def vmem_bytes(tm):
        blocks = sum(tm * K * a.dtype.itemsize for a in a_list) + gated * tm * K * 4 + 2 * tm * D * 4
        return 2 * blocks + w.size * w.dtype.itemsize

    tm = next(t for t in (512, 256, 128, 64, 32, 16, 8) if M % t == 0 and vmem_bytes(t) <= PROJ_VMEM_BUDGET)
    a_spec = pl.BlockSpec((tm, K), lambda i: (i, 0))
    row_spec = pl.BlockSpec((tm, D), lambda i: (i, 0))
    vec_spec = pl.BlockSpec((1, D), lambda i: (0, 0))
    in_specs = [a_spec] * n_a + [a_spec] * gated
    in_specs += [pl.BlockSpec(w.shape, lambda i: (0, 0), pipeline_mode=pl.Buffered(1)),
                 row_spec, vec_spec, vec_spec]
    args = list(a_list) + [gate] * gated + [w, x, g.reshape(1, D), b.reshape(1, D)]
    return pl.pallas_call(
        functools.partial(_proj_ln_kernel, n_a=n_a, alpha=alpha, gated=gated),
        grid=(M // tm,),
        in_specs=in_specs,
        out_specs=row_spec,
        out_shape=jax.ShapeDtypeStruct((M, D), F32),
        compiler_params=_params("parallel"),
        name="proj_ln",
    )(*args)


def _mlp_ln_kernel(x_ref, w1_ref, w2_ref, g_ref, b_ref, o_ref, xb_ref, hb_ref, *, nf, alpha):
    f = pl.program_id(1)

    def up():
        h = jnp.maximum(jnp.dot(xb_ref[...], w1_ref[0], preferred_element_type=F32), 0.0)
        return (h * h).astype(BF16)

    def down():
        o_ref[...] += jnp.dot(hb_ref[...], w2_ref[0], preferred_element_type=F32)

    @pl.when(f == 0)
    def _():
        o_ref[...] = jnp.zeros_like(o_ref)
        xb_ref[...] = x_ref[...].astype(BF16)
        hb_ref[...] = up()

    @pl.when((f > 0) & (f < nf))
    def _():
        nxt = up()
        down()
        hb_ref[...] = nxt

    @pl.when(f == nf)
    def _():
        down()
        y = alpha * x_ref[...] + o_ref[...]
        o_ref[...] = _layer_norm(y, g_ref[...], b_ref[...])


def mlp_ln(x, w1, w2, layer, g, b, alpha, tm=1024, tf=512):
    M, D = x.shape
    FF = w1.shape[2]
    tm, tf = _tile(M, tm), _tile(FF, tf)
    nf = FF // tf
    row_spec = pl.BlockSpec((tm, D), lambda i, f: (i, 0))
    vec_spec = pl.BlockSpec((1, D), lambda i, f: (0, 0))
    return pl.pallas_call(
        functools.partial(_mlp_ln_kernel, nf=nf, alpha=alpha),
        grid=(M // tm, nf + 1),
        in_specs=[row_spec,
                  pl.BlockSpec((1, D, tf), lambda i, f: (layer, 0, jnp.minimum(f, nf - 1))),
                  pl.BlockSpec((1, tf, D), lambda i, f: (layer, jnp.maximum(f - 1, 0), 0)),
                  vec_spec, vec_spec],
        out_specs=row_spec,
        out_shape=jax.ShapeDtypeStruct((M, D), F32),
        scratch_shapes=[pltpu.VMEM((tm, D), BF16), pltpu.VMEM((tm, tf), BF16)],
        compiler_params=_params("parallel", "arbitrary"),
        name="mlp_ln",
    )(x, w1, w2, g.reshape(1, D), b.reshape(1, D))


def _mlstm_kernel(q_ref, k_ref, v_ref, o_ref, x_ref, wg_ref, gbias_ref, nw_ref, c0_ref, n0_ref, m0_ref,
                  h_ref, c1_ref, n1_ref, m1_ref, *, heads, dk, dv, L):
    t = pl.program_id(1)

    @pl.when(t == 0)
    def _():
        c1_ref[...] = c0_ref[...]
        n1_ref[...] = n0_ref[...]
        m1_ref[...] = m0_ref[...]

    row = lax.broadcasted_iota(jnp.int32, (L, L), 0)
    col = lax.broadcasted_iota(jnp.int32, (L, L), 1)
    causal = row >= col
    tril = causal.astype(F32)
    sel_r = lax.broadcasted_iota(jnp.int32, (2 * heads, LANES), 0)
    sel_c = lax.broadcasted_iota(jnp.int32, (2 * heads, LANES), 1)
    pick = (sel_r == sel_c).astype(F32)
    lane = lax.broadcasted_iota(jnp.int32, (L, LANES), 1)
    scale = dk ** -0.5

    gt = jnp.dot(x_ref[0].astype(BF16), wg_ref[...], preferred_element_type=F32) + gbias_ref[...]
    logf = jnp.minimum(gt, 0.0) - jnp.log1p(jnp.exp(-jnp.abs(gt)))
    gl = jnp.where(lane < heads, gt, logf)
    cum = jnp.dot(tril, gl, preferred_element_type=F32, precision=HIGHEST)
    nt_dims = (((1,), (1,)), ((), ()))
    gl_t = lax.dot_general(pick, gl, nt_dims, preferred_element_type=F32, precision=HIGHEST)
    cum_t = lax.dot_general(pick, cum, nt_dims, preferred_element_type=F32, precision=HIGHEST)
    for h in range(heads):
        ig_col = gl[:, h:h + 1]
        b_col = cum[:, heads + h:heads + h + 1]
        ig_row = gl_t[h:h + 1, :]
        b_row = cum_t[heads + h:heads + h + 1, :]
        b_last = b_row[:, L - 1:L]
        m_prev = m1_ref[0, :, h:h + 1]
        qh = (q_ref[0, :, h * dk:(h + 1) * dk] * scale).astype(BF16)
        kf = k_ref[0, :, h * dk:(h + 1) * dk]
        kh = kf.astype(BF16)
        vh = v_ref[0, :, h * dv:(h + 1) * dv].astype(BF16)
        c_prev = c1_ref[0, h]
        n_prev = n1_ref[0, h:h + 1, :]

        dmat = jnp.where(causal, b_col - b_row + ig_row, -jnp.inf)
        inter = b_col + m_prev
        m_t = jnp.maximum(inter, jnp.max(dmat, -1, keepdims=True))
        w_intra = jnp.exp(dmat - m_t)
        w_inter = jnp.exp(inter - m_t)
        s = lax.dot_general(qh, kh, nt_dims, preferred_element_type=F32)
        qk = s * w_intra
        num = jnp.dot(qk.astype(BF16), vh, preferred_element_type=F32)
        num = num + w_inter * jnp.dot(qh, c_prev.astype(BF16), preferred_element_type=F32)
        qn = jnp.sum(qh.astype(F32) * n_prev, -1, keepdims=True)
        den = jnp.sum(qk, -1, keepdims=True) + w_inter * qn
        hh = num / jnp.maximum(jnp.abs(den), jnp.exp(-m_t))

        mu = jnp.mean(hh, -1, keepdims=True)
        d = hh - mu
        var = jnp.mean(d * d, -1, keepdims=True)
        hn = d * lax.rsqrt(var + LN_EPS)
        og = o_ref[0, :, h * dv:(h + 1) * dv]
        hg = hn * nw_ref[:, h * dv:(h + 1) * dv] * _sigmoid(og)
        h_ref[0, :, h * dv:(h + 1) * dv] = hg.astype(h_ref.dtype)

        lw_col = b_last - b_col + ig_col
        lw_row = b_last - b_row + ig_row
        m_new = jnp.maximum(b_last + m_prev, jnp.max(lw_row, -1, keepdims=True))
        ws_col = jnp.exp(lw_col - m_new)
        wc = jnp.exp(b_last + m_prev - m_new)
        kw = kf * ws_col
        tn_dims = (((0,), (0,)), ((), ()))
        c1_ref[0, h] = wc * c_prev + lax.dot_general(kw.astype(BF16), vh, tn_dims,
                                                     preferred_element_type=F32)
        n1_ref[0, h:h + 1, :] = wc * n_prev + jnp.sum(kw, 0, keepdims=True)
        m1_ref[0, :, h:h + 1] = m_new


def mlstm(u, x, w_gate, gbias, norm_w, c0, n0, m0, heads, dk, dv):
    B, T, D = x.shape
    L = min(T, CHUNK)
    assert T % L == 0
    tb = L
    qw, vw = heads * dk, heads * dv
    assert qw % LANES == 0 and vw == 2 * qw
    st = lambda i, t: (i, 0, 0)
    kern = functools.partial(_mlstm_kernel, heads=heads, dk=dk, dv=dv, L=L)
    return pl.pallas_call(
        kern,
        grid=(B, T // tb),
        in_specs=[pl.BlockSpec((1, tb, qw), lambda i, t: (i, t, 0)),
                  pl.BlockSpec((1, tb, qw), lambda i, t: (i, t, 1)),
                  pl.BlockSpec((1, tb, vw), lambda i, t: (i, t, 1)),
                  pl.BlockSpec((1, tb, vw), lambda i, t: (i, t, 2)),
                  pl.BlockSpec((1, tb, D), lambda i, t: (i, t, 0)),
                  pl.BlockSpec((D, LANES), lambda i, t: (0, 0)),
                  pl.BlockSpec((1, LANES), lambda i, t: (0, 0)),
                  pl.BlockSpec((1, vw), lambda i, t: (0, 0)),
                  pl.BlockSpec((1, heads, dk, dv), lambda i, t: (i, 0, 0, 0)),
                  pl.BlockSpec((1, heads, dk), st),
                  pl.BlockSpec((1, 1, heads), st)],
        out_specs=[pl.BlockSpec((1, tb, vw), lambda i, t: (i, t, 0)),
                   pl.BlockSpec((1, heads, dk, dv), lambda i, t: (i, 0, 0, 0)),
                   pl.BlockSpec((1, heads, dk), st),
                   pl.BlockSpec((1, 1, heads), st)],
        out_shape=[jax.ShapeDtypeStruct((B, T, vw), BF16),
                   jax.ShapeDtypeStruct((B, heads, dk, dv), F32),
                   jax.ShapeDtypeStruct((B, heads, dk), F32),
                   jax.ShapeDtypeStruct((B, 1, heads), F32)],
        compiler_params=_params("parallel", "arbitrary"),
        name="mlstm",
    )(u, u, u, u, x, w_gate, gbias, norm_w, c0, n0, m0)


def _gelu_tanh(x):
    return 0.5 * x * (1.0 + jnp.tanh(0.7978845608028654 * (x + 0.044715 * x * x * x)))


def _lru_kernel(xr_ref, yg_ref, conv0_ref, h0_ref, cw_ref, cb_ref, wa_ref, ba_ref, wx_ref, bx_ref, lam_ref,
                y_ref, conv1_ref, h1_ref, xp_ref, a_ref, b_ref, *, tb, blocks, cw, reset_first):
    t = pl.program_id(1)
    halo = cw - 1

    @pl.when(t == 0)
    def _():
        xp_ref[...] = jnp.zeros_like(xp_ref)
        xp_ref[SUBLANES - halo:, :] = conv0_ref[0]
        h1_ref[0] = h0_ref[0]

    x = xr_ref[0]
    prev = xp_ref[...]
    head_rows = lax.broadcasted_iota(jnp.int32, prev.shape, 0)
    xc = cb_ref[...] + x * cw_ref[cw - 1:cw, :]
    for d in range(1, cw):
        xs = pltpu.roll(x, d, 0)
        first = jnp.where(head_rows < d, pltpu.roll(prev, d, 0), xs[:SUBLANES])
        xs = jnp.concatenate([first, xs[SUBLANES:]], 0)
        xc = xc + xs * cw_ref[cw - 1 - d:cw - d, :]

    sp = _softplus(-lam_ref[...])
    bw = xc.shape[1] // blocks
    for g in range(blocks):
        sl = slice(g * bw, (g + 1) * bw)
        xg = xc[:, sl]
        xgb = xg.astype(BF16)
        gr = _sigmoid(jnp.dot(xgb, wa_ref[g], preferred_element_type=F32) + ba_ref[:, sl])
        gi = _sigmoid(jnp.dot(xgb, wx_ref[g], preferred_element_type=F32) + bx_ref[:, sl])
        log_a = -LRU_C * gr * sp[:, sl]
        th = jnp.tanh(log_a)
        z = -2.0 * th / (1.0 - th)
        mult = jnp.where(z > 0.0, z * lax.rsqrt(z), 0.0)
        if reset_first:
            first = (lax.broadcasted_iota(jnp.int32, mult.shape, 0) == 0) & (t == 0)
            mult = jnp.where(first, 1.0, mult)
        a_ref[:, sl] = jnp.exp(log_a)
        b_ref[:, sl] = mult * gi * xg

    def step(i, h):
        base = pl.multiple_of(i * SUBLANES, SUBLANES)
        rows = []
        for u in range(SUBLANES):
            h = a_ref[pl.ds(base + u, 1), :] * h + b_ref[pl.ds(base + u, 1), :]
            rows.append(h)
        b_ref[pl.ds(base, SUBLANES), :] = jnp.concatenate(rows, 0)
        return h

    h_last = lax.fori_loop(0, tb // SUBLANES, step, h1_ref[0])
    h1_ref[0] = h_last
    y_ref[0] = (b_ref[...] * _gelu_tanh(yg_ref[0])).astype(y_ref.dtype)
    last = xr_ref[0, tb - SUBLANES:, :]
    xp_ref[...] = last
    conv1_ref[0] = last[SUBLANES - halo:, :]


def lru(u, col0, conv0, h0, conv_w, conv_b, wa, ba, wx, bx, lam, reset_first):
    B, T, _ = u.shape
    W = conv_w.shape[1]
    assert col0 % W == 0
    cb0 = col0 // W
    cw = conv_w.shape[0]
    blocks = wa.shape[0]
    tb = _tile(T, 256)
    assert tb >= cw - 1 and cw - 1 <= SUBLANES
    st = lambda i, t: (i, 0, 0)
    vec = pl.BlockSpec((1, W), lambda i, t: (0, 0))
    wsp = pl.BlockSpec(wa.shape, lambda i, t: (0, 0, 0))
    kern = functools.partial(_lru_kernel, tb=tb, blocks=blocks, cw=cw, reset_first=reset_first)
    return pl.pallas_call(
        kern,
        grid=(B, T // tb),
        in_specs=[pl.BlockSpec((1, tb, W), lambda i, t: (i, t, cb0)),
                  pl.BlockSpec((1, tb, W), lambda i, t: (i, t, cb0 + 1)),
                  pl.BlockSpec((1, cw - 1, W), st),
                  pl.BlockSpec((1, 1, W), st),
                  pl.BlockSpec((cw, W), lambda i, t: (0, 0)),
                  vec, wsp, vec, wsp, vec, vec],
        out_specs=[pl.BlockSpec((1, tb, W), lambda i, t: (i, t, 0)),
                   pl.BlockSpec((1, cw - 1, W), st),
                   pl.BlockSpec((1, 1, W), st)],
        out_shape=[jax.ShapeDtypeStruct((B, T, W), BF16),
                   jax.ShapeDtypeStruct((B, cw - 1, W), F32),
                   jax.ShapeDtypeStruct((B, 1, W), F32)],
        scratch_shapes=[pltpu.VMEM((SUBLANES, W), F32),
                        pltpu.VMEM((tb, W), F32),
                        pltpu.VMEM((tb, W), F32)],
        compiler_params=_params("parallel", "arbitrary"),
        name="lru",
    )(u, u, conv0, h0, conv_w, conv_b, wa, ba, wx, bx, lam)


def _lora_math(x, w1_ref, w2_ref, bias_ref, mid, post):
    z = jnp.dot(x, w1_ref[...], preferred_element_type=F32)
    if mid == "tanh":
        z = jnp.tanh(z)
    elif mid == "sigmoid":
        z = _sigmoid(z)
    y = jnp.dot(z.astype(BF16), w2_ref[...], preferred_element_type=F32)
    if post == "decay":
        y = jnp.exp(-EXP_NEG_HALF * _sigmoid(bias_ref[...] + y))
    elif post == "sigmoid":
        y = _sigmoid(bias_ref[...] + y)
    return y


def _rwkv_scan_kernel(r_ref, k_ref, v_ref, w_ref, a_ref, kk_p, ka_p, rk_p, gg_p, gb_p, s0_ref,
                      y_ref, s1_ref, nkk_s, kka_s, km_s, vt_s, ys_s, *, tt, n):
    t = pl.program_id(1)
    nb = n // SUBLANES

    @pl.when(t == 0)
    def _():
        s1_ref[...] = s0_ref[...]

    def prow(p_ref, f):
        return p_ref[f:f + 1, :]

    nrm = jnp.zeros((tt, LANES), F32)
    for f in range(n):
        kk = k_ref[0, f] * prow(kk_p, f)
        nrm = nrm + kk * kk
    inv = lax.rsqrt(jnp.maximum(nrm, 1e-24))
    for f in range(n):
        kf, af = k_ref[0, f], a_ref[0, f]
        kk = kf * prow(kk_p, f) * inv
        nkk_s[f] = -kk
        kka_s[f] = kk * af
        km_s[f] = kf * (1.0 + (af - 1.0) * prow(ka_p, f))

    vt_s[...] = jnp.swapaxes(v_ref[0], 0, 1)
    zeros = tuple(jnp.zeros((SUBLANES, LANES), F32) for _ in range(nb))

    def reduce_keys(kx, acc):
        nk = nkk_s[kx, pl.ds(0, 1), :]
        return tuple(acc[jb] + s1_ref[0, jb, kx] * nk for jb in range(nb))

    sa0 = lax.fori_loop(0, n, reduce_keys, zeros, unroll=8)

    def time_step(i, sa):
        row = pl.ds(i, 1)
        nxt = pl.ds(jnp.minimum(i + 1, tt - 1), 1)
        vt = [vt_s[i, jb * SUBLANES:(jb + 1) * SUBLANES, :] for jb in range(nb)]

        def update_keys(kx, acc):
            wr = w_ref[0, kx, row, :]
            ar = kka_s[kx, row, :]
            mr = km_s[kx, row, :]
            rr = r_ref[0, kx, row, :]
            nk = nkk_s[kx, nxt, :]
            ys, sn = [], []
            for jb in range(nb):
                s = s1_ref[0, jb, kx] * wr + sa[jb] * ar + vt[jb] * mr
                s1_ref[0, jb, kx] = s
                ys.append(acc[jb] + s * rr)
                sn.append(acc[nb + jb] + s * nk)
            return tuple(ys + sn)

        acc = lax.fori_loop(0, n, update_keys, zeros + zeros, unroll=32)
        for jb in range(nb):
            ys_s[i, jb * SUBLANES:(jb + 1) * SUBLANES, :] = acc[jb]
        return tuple(acc[nb:])

    lax.fori_loop(0, tt, time_step, sa0)
    y_ref[0] = jnp.swapaxes(ys_s[...], 0, 1)

    mu = jnp.zeros((tt, LANES), F32)
    cb = jnp.zeros((tt, LANES), F32)
    for f in range(n):
        mu = mu + y_ref[0, f]
        cb = cb + r_ref[0, f] * km_s[f] * prow(rk_p, f)
    mu = mu * (1.0 / n)
    var = jnp.zeros((tt, LANES), F32)
    for f in range(n):
        d = y_ref[0, f] - mu
        var = var + d * d
    rs = lax.rsqrt(var * (1.0 / n) + RWKV_GN_EPS)
    for f in range(n):
        y_ref[0, f] = (y_ref[0, f] - mu) * rs * prow(gg_p, f) + prow(gb_p, f) + cb * v_ref[0, f]


def rwkv_scan(r, k, v, w, a, kk_p, ka_p, rk_p, gg_p, gb_p, s0):
    G, n, T, _ = r.shape
    tt = _tile(T, 64)
    seq = pl.BlockSpec((1, n, tt, LANES), lambda g, t: (g, 0, t, 0))
    par = pl.BlockSpec((n, LANES), lambda g, t: (0, 0))
    st = pl.BlockSpec((1, n // SUBLANES, n, SUBLANES, LANES), lambda g, t: (g, 0, 0, 0, 0))
    return pl.pallas_call(
        functools.partial(_rwkv_scan_kernel, tt=tt, n=n),
        grid=(G, T // tt),
        in_specs=[seq] * 5 + [par] * 5 + [st],
        out_specs=[seq, st],
        out_shape=[jax.ShapeDtypeStruct((G, n, T, LANES), F32),
                   jax.ShapeDtypeStruct(s0.shape, F32)],
        scratch_shapes=[pltpu.VMEM((n, tt, LANES), F32)] * 3 + [pltpu.VMEM((tt, n, LANES), F32)] * 2,
        compiler_params=_params("parallel", "arbitrary"),
        name="rwkv_scan",
    )(r, k, v, w, a, kk_p, ka_p, rk_p, gg_p, gb_p, s0)


def _store_scan_layout(res, z_ref, heads, c0=0):
    bl = LANES // heads
    for c in range(res.shape[1] // LANES):
        sub = [res[s * LANES:(s + 1) * LANES, c * LANES:(c + 1) * LANES].T for s in range(bl)]
        for ni in range(bl):
            tile = jnp.concatenate([sub[s][ni * heads:(ni + 1) * heads, :] for s in range(bl)], 0)
            z_ref[0, (c0 + c) * bl + ni] = tile.T


MXU_WIDTH = 256


def _mixed_tokens(x_ref, prev_ref, shift_ref, mu_ref):
    x = x_ref[...]
    bl, tq, D = x.shape
    prev = jnp.where(pl.program_id(1) == 0, shift_ref[...], prev_ref[:, SUBLANES - 1:, :])
    xs = pltpu.roll(x, 1, 1)
    head_rows = lax.broadcasted_iota(jnp.int32, (bl, SUBLANES, D), 1)
    first = jnp.where(head_rows == 0, prev, xs[:, :SUBLANES, :])
    x_prev = jnp.concatenate([first, xs[:, SUBLANES:, :]], 1)
    xm = x + (x_prev - x) * mu_ref[0]
    return xm.astype(BF16).reshape(bl * tq, D)


def _mix_proj_kernel(x_ref, prev_ref, shift_ref, mu_ref, *refs, heads, lora_args, scan_out):
    o_ref = refs[-1]
    bl, tq, _ = x_ref.shape
    a = _mixed_tokens(x_ref, prev_ref, shift_ref, mu_ref)
    if lora_args is not None:
        res = _lora_math(a, *refs[:3], *lora_args)
        if scan_out:
            _store_scan_layout(res, o_ref, heads)
        else:
            o_ref[...] = res.reshape(bl, tq, res.shape[1])
    elif scan_out:
        w_ref = refs[0]
        for c in range(w_ref.shape[1] // MXU_WIDTH):
            cols = slice(c * MXU_WIDTH, (c + 1) * MXU_WIDTH)
            res = jnp.dot(a, w_ref[:, cols], preferred_element_type=F32)
            _store_scan_layout(res, o_ref, heads, c * (MXU_WIDTH // LANES))
    else:
        res = jnp.dot(a, refs[0][...], preferred_element_type=F32)
        o_ref[...] = res.reshape(bl, tq, res.shape[1])


def mix_proj(x, shift, mu, j, weights, heads, scan_out, lora_args=None):
    B, T, D = x.shape
    N = weights[-1].shape[1]
    bl = LANES // heads
    tq = min(T, LANES)
    assert T % tq == 0 and tq % SUBLANES == 0 and not (scan_out and tq != LANES)
    steps = tq // SUBLANES
    const = lambda w: pl.BlockSpec(w.shape, lambda g, t: (0,) * w.ndim, pipeline_mode=pl.Buffered(1))
    if scan_out:
        out_spec = pl.BlockSpec((1, N // heads, LANES, LANES), lambda g, t: (g, 0, t, 0))
        out_shape = jax.ShapeDtypeStruct((B // bl, N // heads, T, LANES), F32)
    else:
        out_spec = pl.BlockSpec((bl, tq, N), lambda g, t: (g, t, 0))
        out_shape = jax.ShapeDtypeStruct((B, T, N), F32)
    return pl.pallas_call(
        functools.partial(_mix_proj_kernel, heads=heads, lora_args=lora_args, scan_out=scan_out),
        grid=(B // bl, T // tq),
        in_specs=[pl.BlockSpec((bl, tq, D), lambda g, t: (g, t, 0)),
                  pl.BlockSpec((bl, SUBLANES, D), lambda g, t: (g, jnp.maximum(t * steps - 1, 0), 0)),
                  pl.BlockSpec((bl, 1, D), lambda g, t: (g, 0, 0)),
                  pl.BlockSpec((1, 1, D), lambda g, t: (j, 0, 0))] + [const(w) for w in weights],
        out_specs=out_spec,
        out_shape=out_shape,
        compiler_params=_params("parallel", "arbitrary"),
        name="mix_proj",
    )(x, x, shift, mu.reshape(mu.shape[0], 1, D), *weights)


def _scan_proj_ln_kernel(y_ref, gate_ref, w_ref, x_ref, g_ref, b_ref, o_ref, *, heads, alpha):
    bl = LANES // heads
    D = o_ref.shape[2]
    per = MXU_WIDTH // LANES
    acc = None
    for kc in range(D // MXU_WIDTH):
        parts = []
        for c in range(kc * per, (kc + 1) * per):
            sub = [y_ref[0, c * bl + ni].T for ni in range(bl)]
            cols = slice(c * LANES, (c + 1) * LANES)
            tiles = []
            for s in range(bl):
                tile = jnp.concatenate([sub[ni][s * heads:(s + 1) * heads, :] for ni in range(bl)], 0)
                tiles.append((tile.T * gate_ref[s, :, cols]).astype(BF16))
            parts.append(jnp.concatenate(tiles, 0))
        a = jnp.concatenate(parts, 1)
        part = jnp.dot(a, w_ref[kc * MXU_WIDTH:(kc + 1) * MXU_WIDTH, :], preferred_element_type=F32)
        acc = part if acc is None else acc + part
    y = alpha * x_ref[...].reshape(bl * LANES, D) + acc
    o_ref[...] = _layer_norm(y, g_ref[...], b_ref[...]).reshape(bl, LANES, D)


def scan_proj_ln(y, gate, w, x, g, b, alpha, heads):
    G, n, T, _ = y.shape
    B, _, D = x.shape
    bl = LANES // heads
    tok = pl.BlockSpec((bl, LANES, D), lambda g_, t: (g_, t, 0))
    vec = pl.BlockSpec((1, D), lambda g_, t: (0, 0))
    return pl.pallas_call(
        functools.partial(_scan_proj_ln_kernel, heads=heads, alpha=alpha),
        grid=(G, T // LANES),
        in_specs=[pl.BlockSpec((1, n, LANES, LANES), lambda g_, t: (g_, 0, t, 0)), tok,
                  pl.BlockSpec((D, D), lambda g_, t: (0, 0), pipeline_mode=pl.Buffered(1)),
                  tok, vec, vec],
        out_specs=tok,
        out_shape=jax.ShapeDtypeStruct((B, T, D), F32),
        compiler_params=_params("parallel", "parallel"),
        name="scan_proj_ln",
    )(y, gate, w, x, g.reshape(1, D), b.reshape(1, D))


def _pad_cols(w, n):
    return jnp.pad(w, ((0, 0), (0, n - w.shape[1])))


def _pad_rows(w, n):
    return jnp.pad(w, ((0, n - w.shape[0]), (0, 0)))


def _layer_a(x, st, p, li, reset_first, alpha):
    c0, n0, m0, conv0, h0 = st
    B, T, D = x.shape
    heads = p['a_b_ig'].shape[1]
    dv = p['a_mlstm_norm'].shape[1] // heads
    dk = dv // 2
    qw, vw = heads * dk, heads * dv
    W = p['a_conv_w'].shape[2]
    w_in = p['a_w_in'][li]
    n_qkvo = 2 * qw + 2 * vw
    w_gate = _pad_cols(w_in[:, n_qkvo:n_qkvo + 2 * heads], LANES).astype(BF16)
    w_u = jnp.concatenate([w_in[:, :n_qkvo], w_in[:, n_qkvo + 2 * heads:]], 1).astype(BF16)
    x2 = x.reshape(B * T, D)
    u = matmul(x2, w_u).reshape(B, T, n_qkvo + 2 * W)
    gbias = _pad_cols(jnp.concatenate([p['a_b_ig'][li], p['a_b_fg'][li]])[None, :], LANES)
    hm, c1, n1, m1 = mlstm(u, x, w_gate, gbias, p['a_mlstm_norm'][li][None, :],
                           c0, n0, m0.reshape(B, 1, heads), heads, dk, dv)
    yb, conv1, h1 = lru(u, n_qkvo, conv0, h0.reshape(B, 1, W), p['a_conv_w'][li], p['a_conv_b'][li][None, :],
                        p['a_lru_wa'][li].astype(BF16), p['a_lru_ba'][li][None, :],
                        p['a_lru_wx'][li].astype(BF16), p['a_lru_bx'][li][None, :],
                        p['a_lru_lambda'][li][None, :], reset_first)
    assert vw == W
    y = proj_ln([hm.reshape(B * T, vw), yb.reshape(B * T, W)], None, p['a_w_out'][li].astype(BF16), x2,
                p['ln1_g'][2 * li], p['ln1_b'][2 * li], alpha)
    return y.reshape(B, T, D), (c1, n1, m1.reshape(B, heads), conv1, h1.reshape(B, W))


def _layer_c(x, shift, s0, p, li, layer, alpha):
    B, T, D = x.shape
    H, N = p['c_r_k'].shape[1:]
    M = B * T
    bl = LANES // H
    G = B // bl
    nb = N // SUBLANES
    def cols(w):
        return w.reshape(w.shape[0], H, N).swapaxes(1, 2).reshape(w.shape[0], D)

    w_r = cols(p['c_w_r'][li].astype(BF16))
    w_k = cols(p['c_w_k'][li].astype(BF16))
    w_v = cols(p['c_w_v'][li].astype(BF16))
    rd = -(-p['c_w1'].shape[2] // LANES) * LANES
    ra = -(-p['c_a1'].shape[2] // LANES) * LANES
    w_d = (_pad_cols(p['c_w1'][li], rd).astype(BF16), _pad_rows(cols(p['c_w2'][li]), rd).astype(BF16),
           cols(p['c_w0'][li][None, :]))
    w_a = (_pad_cols(p['c_a1'][li], ra).astype(BF16), _pad_rows(cols(p['c_a2'][li]), ra).astype(BF16),
           cols(p['c_a0'][li][None, :]))
    w_g = (p['c_g1'][li].astype(BF16), cols(p['c_g2'][li].astype(BF16)), jnp.zeros((1, D), F32))
    fused = T % LANES == 0
    mix = functools.partial(mix_proj, x, shift.reshape(B, 1, D), p['c_mu'][li])
    r = mix(0, (w_r,), H, fused)
    k = mix(2, (w_k,), H, fused)
    v = mix(3, (w_v,), H, fused)
    decay = mix(1, w_d, H, fused, ("tanh", "decay"))
    a = mix(4, w_a, H, fused, ("none", "sigmoid"))
    g = mix(5, w_g, H, False, ("sigmoid", "none"))
    if not fused:
        def to_scan(z):
            return z.reshape(G, bl, T, N, H).transpose(0, 3, 2, 1, 4).reshape(G, N, T, LANES)

        r, k, v, decay, a = (to_scan(z) for z in (r, k, v, decay, a))

    def par(z):
        return jnp.tile(z.reshape(H, N).T, (1, bl))

    s0t = s0.reshape(G, bl, H, nb, SUBLANES, N).transpose(0, 3, 5, 4, 1, 2).reshape(G, nb, N, SUBLANES, LANES)
    yt, s1t = rwkv_scan(r, k, v, decay, a,
                        par(p['c_k_k'][li]), par(p['c_k_a'][li]), par(p['c_r_k'][li].reshape(D)),
                        par(p['c_gn_g'][li]), par(p['c_gn_b'][li]), s0t)
    s1 = s1t.reshape(G, nb, N, SUBLANES, bl, H).transpose(0, 4, 5, 1, 3, 2).reshape(B, H, N, N)
    w_o = p['c_w_o'][li].astype(BF16).reshape(H, N, D).swapaxes(0, 1).reshape(D, D)
    if fused:
        out = scan_proj_ln(yt, g, w_o, x, p['ln1_g'][layer], p['ln1_b'][layer], alpha, H)
    else:
        y = yt.reshape(G, N, T, bl, H).transpose(0, 3, 2, 1, 4).reshape(M, D)
        out = proj_ln([y], g.reshape(M, D), w_o, x.reshape(M, D), p['ln1_g'][layer], p['ln1_b'][layer], alpha)
    return out.reshape(B, T, D), (x[:, -1], s1)


def _trunk(x, states, p, reset_first):
    mC, mn, mm, cv, hl, sh, S = states
    depth = p['ln1_g'].shape[0]
    alpha = (2 * depth) ** 0.25
    B, T, D = x.shape
    new_a, new_c = [], []
    for layer in range(depth):
        li = layer // 2
        if layer % 2 == 0:
            x, st = _layer_a(x, (mC[li], mn[li], mm[li], cv[li], hl[li]), p, li, reset_first, alpha)
            new_a.append(st)
        else:
            x, st = _layer_c(x, sh[li], S[li], p, li, layer, alpha)
            new_c.append(st)
        x = mlp_ln(x.reshape(B * T, D), p['mlp_w1'].astype(BF16), p['mlp_w2'].astype(BF16), layer,
                   p['ln2_g'][layer], p['ln2_b'][layer], alpha).reshape(B, T, D)
    sa = [jnp.stack([s[j] for s in new_a]) for j in range(5)]
    sc = [jnp.stack([s[j] for s in new_c]) for j in range(2)]
    return x, sa + sc


def kernel(x_prompt, x_sample, state_mlstm_C, state_mlstm_n, state_mlstm_m, state_lru_conv, state_lru_h,
           state_rwkv_shift, state_rwkv_S, a_w_in, a_b_ig, a_b_fg, a_mlstm_norm, a_conv_w, a_conv_b,
           a_lru_wa, a_lru_ba, a_lru_wx, a_lru_bx, a_lru_lambda, a_w_out, c_mu, c_w_r, c_w_k, c_w_v,
           c_w0, c_w1, c_w2, c_a0, c_a1, c_a2, c_g1, c_g2, c_k_k, c_k_a, c_r_k, c_gn_g, c_gn_b, c_w_o,
           ln1_g, ln1_b, ln2_g, ln2_b, mlp_w1, mlp_w2):
    p = dict(a_w_in=a_w_in, a_b_ig=a_b_ig, a_b_fg=a_b_fg, a_mlstm_norm=a_mlstm_norm, a_conv_w=a_conv_w,
             a_conv_b=a_conv_b, a_lru_wa=a_lru_wa, a_lru_ba=a_lru_ba, a_lru_wx=a_lru_wx, a_lru_bx=a_lru_bx,
             a_lru_lambda=a_lru_lambda, a_w_out=a_w_out, c_mu=c_mu, c_w_r=c_w_r, c_w_k=c_w_k, c_w_v=c_w_v,
             c_w0=c_w0, c_w1=c_w1, c_w2=c_w2, c_a0=c_a0, c_a1=c_a1, c_a2=c_a2, c_g1=c_g1, c_g2=c_g2,
             c_k_k=c_k_k, c_k_a=c_k_a, c_r_k=c_r_k, c_gn_g=c_gn_g, c_gn_b=c_gn_b, c_w_o=c_w_o,
             ln1_g=ln1_g, ln1_b=ln1_b, ln2_g=ln2_g, ln2_b=ln2_b, mlp_w1=mlp_w1, mlp_w2=mlp_w2)
    Bp = x_prompt.shape[0]
    init = tuple(jnp.zeros((s.shape[0], Bp) + s.shape[2:], s.dtype)
                 for s in (state_mlstm_C, state_mlstm_n, state_mlstm_m, state_lru_conv, state_lru_h,
                           state_rwkv_shift, state_rwkv_S))
    y_prompt, ps = _trunk(x_prompt, init, p, True)
    y_sample, ss = _trunk(x_sample, (state_mlstm_C, state_mlstm_n, state_mlstm_m, state_lru_conv,
                                     state_lru_h, state_rwkv_shift, state_rwkv_S), p, False)
    return (y_prompt, y_sample, *ps, *ss)
```

```python
import functools

import jax
import jax.numpy as jnp
from jax import lax
from jax.experimental import pallas as pl
from jax.experimental.pallas import tpu as pltpu

F32 = jnp.float32
BF16 = jnp.bfloat16

LANES = 128
SUBLANES = 8
VMEM_LIMIT_BYTES = 56 * 1024 * 1024

CHUNK = 256
LRU_C = 8.0
LN_EPS = 1e-5
RWKV_GN_EPS = 64e-5
HIGHEST = lax.Precision.HIGHEST
EXP_NEG_HALF = 0.6065306597126334


def _params(*sem):
    return pltpu.CompilerParams(dimension_semantics=sem, vmem_limit_bytes=VMEM_LIMIT_BYTES)


def _tile(n, pref):
    t = min(n, pref)
    while n % t:
        t -= 1
    return t


def _sigmoid(x):
    return 1.0 / (1.0 + jnp.exp(-x))


def _softplus(x):
    return jnp.maximum(x, 0.0) + jnp.log1p(jnp.exp(-jnp.abs(x)))


def _layer_norm(y, g, b):
    mu = jnp.mean(y, -1, keepdims=True)
    d = y - mu
    var = jnp.mean(d * d, -1, keepdims=True)
    return d * lax.rsqrt(var + LN_EPS) * g + b


def _mm_kernel(a_ref, w_ref, o_ref, ab_ref):
    @pl.when(pl.program_id(1) == 0)
    def _():
        ab_ref[...] = a_ref[...].astype(BF16)

    o_ref[...] = jnp.dot(ab_ref[...], w_ref[...], preferred_element_type=F32)


def matmul(a, w, tm=1024, tn=1024):
    M, K = a.shape
    N = w.shape[1]
    tm, tn = _tile(M, tm), _tile(N, tn)
    return pl.pallas_call(
        _mm_kernel,
        grid=(M // tm, N // tn),
        in_specs=[pl.BlockSpec((tm, K), lambda i, j: (i, 0)),
                  pl.BlockSpec((K, tn), lambda i, j: (0, j))],
        out_specs=pl.BlockSpec((tm, tn), lambda i, j: (i, j)),
        out_shape=jax.ShapeDtypeStruct((M, N), F32),
        scratch_shapes=[pltpu.VMEM((tm, K), BF16)],
        compiler_params=_params("parallel", "arbitrary"),
        name="matmul",
    )(a, w)


def _proj_ln_kernel(*refs, n_a, alpha, gated):
    a_refs, rest = refs[:n_a], list(refs[n_a:])
    gate_ref = rest.pop(0) if gated else None
    w_ref, x_ref, g_ref, b_ref, o_ref = rest
    a = jnp.concatenate([r[...] for r in a_refs], 1) if n_a > 1 else a_refs[0][...]
    if gated:
        a = a * gate_ref[...]
    acc = jnp.dot(a.astype(BF16), w_ref[...], preferred_element_type=F32)
    o_ref[...] = _layer_norm(alpha * x_ref[...] + acc, g_ref[...], b_ref[...])


PROJ_VMEM_BUDGET = 46 * 1024 * 1024


def proj_ln(a_list, gate, w, x, g, b, alpha):
    n_a = len(a_list)
    M, K = a_list[0].shape
    D = w.shape[1]
    assert w.shape[0] == n_a * K and not (gate is not None and n_a > 1)
    gated = gate is not None

    def vmem_bytes(tm):
        blocks = sum(tm * K * a.dtype.itemsize for a in a_list) + gated * tm * K * 4 + 2 * tm * D * 4
        return 2 * blocks + w.size * w.dtype.itemsize

    tm = next(t for t in (512, 256, 128, 64, 32, 16, 8) if M % t == 0 and vmem_bytes(t) <= PROJ_VMEM_BUDGET)
    a_spec = pl.BlockSpec((tm, K), lambda i: (i, 0))
    row_spec = pl.BlockSpec((tm, D), lambda i: (i, 0))
    vec_spec = pl.BlockSpec((1, D), lambda i: (0, 0))
    in_specs = [a_spec] * n_a + [a_spec] * gated
    in_specs += [pl.BlockSpec(w.shape, lambda i: (0, 0), pipeline_mode=pl.Buffered(1)),
                 row_spec, vec_spec, vec_spec]
    args = list(a_list) + [gate] * gated + [w, x, g.reshape(1, D), b.reshape(1, D)]
    return pl.pallas_call(
        functools.partial(_proj_ln_kernel, n_a=n_a, alpha=alpha, gated=gated),
        grid=(M // tm,),
        in_specs=in_specs,
        out_specs=row_spec,
        out_shape=jax.ShapeDtypeStruct((M, D), F32),
        compiler_params=_params("parallel"),
        name="proj_ln",
    )(*args)


def _mlp_ln_kernel(x_ref, w1_ref, w2_ref, g_ref, b_ref, o_ref, xb_ref, *, nf, alpha):
    f = pl.program_id(1)

    @pl.when(f == 0)
    def _():
        o_ref[...] = jnp.zeros_like(o_ref)
        xb_ref[...] = x_ref[...].astype(BF16)

    h = jnp.maximum(jnp.dot(xb_ref[...], w1_ref[0], preferred_element_type=F32), 0.0)
    o_ref[...] += jnp.dot((h * h).astype(BF16), w2_ref[0], preferred_element_type=F32)

    @pl.when(f == nf - 1)
    def _():
        y = alpha * x_ref[...] + o_ref[...]
        o_ref[...] = _layer_norm(y, g_ref[...], b_ref[...])


def mlp_ln(x, w1, w2, layer, g, b, alpha, tm=1024, tf=512):
    M, D = x.shape
    FF = w1.shape[2]
    tm, tf = _tile(M, tm), _tile(FF, tf)
    nf = FF // tf
    row_spec = pl.BlockSpec((tm, D), lambda i, f: (i, 0))
    vec_spec = pl.BlockSpec((1, D), lambda i, f: (0, 0))
    return pl.pallas_call(
        functools.partial(_mlp_ln_kernel, nf=nf, alpha=alpha),
        grid=(M // tm, nf),
        in_specs=[row_spec,
                  pl.BlockSpec((1, D, tf), lambda i, f: (layer, 0, f)),
                  pl.BlockSpec((1, tf, D), lambda i, f: (layer, f, 0)),
                  vec_spec, vec_spec],
        out_specs=row_spec,
        out_shape=jax.ShapeDtypeStruct((M, D), F32),
        scratch_shapes=[pltpu.VMEM((tm, D), BF16)],
        compiler_params=_params("parallel", "arbitrary"),
        name="mlp_ln",
    )(x, w1, w2, g.reshape(1, D), b.reshape(1, D))


def _mlstm_kernel(q_ref, k_ref, v_ref, o_ref, x_ref, wg_ref, gbias_ref, nw_ref, c0_ref, n0_ref, m0_ref,
                  h_ref, c1_ref, n1_ref, m1_ref, *, heads, dk, dv, L):
    t = pl.program_id(1)

    @pl.when(t == 0)
    def _():
        c1_ref[...] = c0_ref[...]
        n1_ref[...] = n0_ref[...]
        m1_ref[...] = m0_ref[...]

    row = lax.broadcasted_iota(jnp.int32, (L, L), 0)
    col = lax.broadcasted_iota(jnp.int32, (L, L), 1)
    causal = row >= col
    tril = causal.astype(F32)
    sel_r = lax.broadcasted_iota(jnp.int32, (2 * heads, LANES), 0)
    sel_c = lax.broadcasted_iota(jnp.int32, (2 * heads, LANES), 1)
    pick = (sel_r == sel_c).astype(F32)
    lane = lax.broadcasted_iota(jnp.int32, (L, LANES), 1)
    scale = dk ** -0.5

    gt = jnp.dot(x_ref[0].astype(BF16), wg_ref[...], preferred_element_type=F32) + gbias_ref[...]
    logf = jnp.minimum(gt, 0.0) - jnp.log1p(jnp.exp(-jnp.abs(gt)))
    gl = jnp.where(lane < heads, gt, logf)
    cum = jnp.dot(tril, gl, preferred_element_type=F32, precision=HIGHEST)
    nt_dims = (((1,), (1,)), ((), ()))
    gl_t = lax.dot_general(pick, gl, nt_dims, preferred_element_type=F32, precision=HIGHEST)
    cum_t = lax.dot_general(pick, cum, nt_dims, preferred_element_type=F32, precision=HIGHEST)
    for h in range(heads):
        ig_col = gl[:, h:h + 1]
        b_col = cum[:, heads + h:heads + h + 1]
        ig_row = gl_t[h:h + 1, :]
        b_row = cum_t[heads + h:heads + h + 1, :]
        b_last = b_row[:, L - 1:L]
        m_prev = m1_ref[0, :, h:h + 1]
        qh = (q_ref[0, :, h * dk:(h + 1) * dk] * scale).astype(BF16)
        kf = k_ref[0, :, h * dk:(h + 1) * dk]
        kh = kf.astype(BF16)
        vh = v_ref[0, :, h * dv:(h + 1) * dv].astype(BF16)
        c_prev = c1_ref[0, h]
        n_prev = n1_ref[0, h:h + 1, :]

        dmat = jnp.where(causal, b_col - b_row + ig_row, -jnp.inf)
        inter = b_col + m_prev
        m_t = jnp.maximum(inter, jnp.max(dmat, -1, keepdims=True))
        w_intra = jnp.exp(dmat - m_t)
        w_inter = jnp.exp(inter - m_t)
        s = lax.dot_general(qh, kh, nt_dims, preferred_element_type=F32)
        qk = s * w_intra
        num = jnp.dot(qk.astype(BF16), vh, preferred_element_type=F32)
        num = num + w_inter * jnp.dot(qh, c_prev.astype(BF16), preferred_element_type=F32)
        qn = jnp.sum(qh.astype(F32) * n_prev, -1, keepdims=True)
        den = jnp.sum(qk, -1, keepdims=True) + w_inter * qn
        hh = num / jnp.maximum(jnp.abs(den), jnp.exp(-m_t))

        mu = jnp.mean(hh, -1, keepdims=True)
        d = hh - mu
        var = jnp.mean(d * d, -1, keepdims=True)
        hn = d * lax.rsqrt(var + LN_EPS)
        og = o_ref[0, :, h * dv:(h + 1) * dv]
        hg = hn * nw_ref[:, h * dv:(h + 1) * dv] * _sigmoid(og)
        h_ref[0, :, h * dv:(h + 1) * dv] = hg.astype(h_ref.dtype)

        lw_col = b_last - b_col + ig_col
        lw_row = b_last - b_row + ig_row
        m_new = jnp.maximum(b_last + m_prev, jnp.max(lw_row, -1, keepdims=True))
        ws_col = jnp.exp(lw_col - m_new)
        wc = jnp.exp(b_last + m_prev - m_new)
        kw = kf * ws_col
        tn_dims = (((0,), (0,)), ((), ()))
        c1_ref[0, h] = wc * c_prev + lax.dot_general(kw.astype(BF16), vh, tn_dims,
                                                     preferred_element_type=F32)
        n1_ref[0, h:h + 1, :] = wc * n_prev + jnp.sum(kw, 0, keepdims=True)
        m1_ref[0, :, h:h + 1] = m_new


def mlstm(u, x, w_gate, gbias, norm_w, c0, n0, m0, heads, dk, dv):
    B, T, D = x.shape
    L = min(T, CHUNK)
    assert T % L == 0
    tb = L
    qw, vw = heads * dk, heads * dv
    assert qw % LANES == 0 and vw == 2 * qw
    st = lambda i, t: (i, 0, 0)
    kern = functools.partial(_mlstm_kernel, heads=heads, dk=dk, dv=dv, L=L)
    return pl.pallas_call(
        kern,
        grid=(B, T // tb),
        in_specs=[pl.BlockSpec((1, tb, qw), lambda i, t: (i, t, 0)),
                  pl.BlockSpec((1, tb, qw), lambda i, t: (i, t, 1)),
                  pl.BlockSpec((1, tb, vw), lambda i, t: (i, t, 1)),
                  pl.BlockSpec((1, tb, vw), lambda i, t: (i, t, 2)),
                  pl.BlockSpec((1, tb, D), lambda i, t: (i, t, 0)),
                  pl.BlockSpec((D, LANES), lambda i, t: (0, 0)),
                  pl.BlockSpec((1, LANES), lambda i, t: (0, 0)),
                  pl.BlockSpec((1, vw), lambda i, t: (0, 0)),
                  pl.BlockSpec((1, heads, dk, dv), lambda i, t: (i, 0, 0, 0)),
                  pl.BlockSpec((1, heads, dk), st),
                  pl.BlockSpec((1, 1, heads), st)],
        out_specs=[pl.BlockSpec((1, tb, vw), lambda i, t: (i, t, 0)),
                   pl.BlockSpec((1, heads, dk, dv), lambda i, t: (i, 0, 0, 0)),
                   pl.BlockSpec((1, heads, dk), st),
                   pl.BlockSpec((1, 1, heads), st)],
        out_shape=[jax.ShapeDtypeStruct((B, T, vw), BF16),
                   jax.ShapeDtypeStruct((B, heads, dk, dv), F32),
                   jax.ShapeDtypeStruct((B, heads, dk), F32),
                   jax.ShapeDtypeStruct((B, 1, heads), F32)],
        compiler_params=_params("parallel", "arbitrary"),
        name="mlstm",
    )(u, u, u, u, x, w_gate, gbias, norm_w, c0, n0, m0)


def _gelu_tanh(x):
    return 0.5 * x * (1.0 + jnp.tanh(0.7978845608028654 * (x + 0.044715 * x * x * x)))


def _lru_kernel(xr_ref, yg_ref, conv0_ref, h0_ref, cw_ref, cb_ref, wa_ref, ba_ref, wx_ref, bx_ref, lam_ref,
                y_ref, conv1_ref, h1_ref, xp_ref, a_ref, b_ref, *, tb, blocks, cw, reset_first):
    t = pl.program_id(1)
    halo = cw - 1

    @pl.when(t == 0)
    def _():
        xp_ref[...] = jnp.zeros_like(xp_ref)
        xp_ref[SUBLANES - halo:, :] = conv0_ref[0]
        h1_ref[0] = h0_ref[0]

    x = xr_ref[0]
    prev = xp_ref[...]
    head_rows = lax.broadcasted_iota(jnp.int32, prev.shape, 0)
    xc = cb_ref[...] + x * cw_ref[cw - 1:cw, :]
    for d in range(1, cw):
        xs = pltpu.roll(x, d, 0)
        first = jnp.where(head_rows < d, pltpu.roll(prev, d, 0), xs[:SUBLANES])
        xs = jnp.concatenate([first, xs[SUBLANES:]], 0)
        xc = xc + xs * cw_ref[cw - 1 - d:cw - d, :]

    sp = _softplus(-lam_ref[...])
    bw = xc.shape[1] // blocks
    for g in range(blocks):
        sl = slice(g * bw, (g + 1) * bw)
        xg = xc[:, sl]
        xgb = xg.astype(BF16)
        gr = _sigmoid(jnp.dot(xgb, wa_ref[g], preferred_element_type=F32) + ba_ref[:, sl])
        gi = _sigmoid(jnp.dot(xgb, wx_ref[g], preferred_element_type=F32) + bx_ref[:, sl])
        log_a = -LRU_C * gr * sp[:, sl]
        th = jnp.tanh(log_a)
        z = -2.0 * th / (1.0 - th)
        mult = jnp.where(z > 0.0, z * lax.rsqrt(z), 0.0)
        if reset_first:
            first = (lax.broadcasted_iota(jnp.int32, mult.shape, 0) == 0) & (t == 0)
            mult = jnp.where(first, 1.0, mult)
        a_ref[:, sl] = jnp.exp(log_a)
        b_ref[:, sl] = mult * gi * xg

    def step(i, h):
        h = a_ref[pl.ds(i, 1), :] * h + b_ref[pl.ds(i, 1), :]
        b_ref[pl.ds(i, 1), :] = h
        return h

    h_last = lax.fori_loop(0, tb, step, h1_ref[0], unroll=8)
    h1_ref[0] = h_last
    y_ref[0] = (b_ref[...] * _gelu_tanh(yg_ref[0])).astype(y_ref.dtype)
    last = xr_ref[0, tb - SUBLANES:, :]
    xp_ref[...] = last
    conv1_ref[0] = last[SUBLANES - halo:, :]


def lru(u, col0, conv0, h0, conv_w, conv_b, wa, ba, wx, bx, lam, reset_first):
    B, T, _ = u.shape
    W = conv_w.shape[1]
    assert col0 % W == 0
    cb0 = col0 // W
    cw = conv_w.shape[0]
    blocks = wa.shape[0]
    tb = _tile(T, 256)
    assert tb >= cw - 1 and cw - 1 <= SUBLANES
    st = lambda i, t: (i, 0, 0)
    vec = pl.BlockSpec((1, W), lambda i, t: (0, 0))
    wsp = pl.BlockSpec(wa.shape, lambda i, t: (0, 0, 0))
    kern = functools.partial(_lru_kernel, tb=tb, blocks=blocks, cw=cw, reset_first=reset_first)
    return pl.pallas_call(
        kern,
        grid=(B, T // tb),
        in_specs=[pl.BlockSpec((1, tb, W), lambda i, t: (i, t, cb0)),
                  pl.BlockSpec((1, tb, W), lambda i, t: (i, t, cb0 + 1)),
                  pl.BlockSpec((1, cw - 1, W), st),
                  pl.BlockSpec((1, 1, W), st),
                  pl.BlockSpec((cw, W), lambda i, t: (0, 0)),
                  vec, wsp, vec, wsp, vec, vec],
        out_specs=[pl.BlockSpec((1, tb, W), lambda i, t: (i, t, 0)),
                   pl.BlockSpec((1, cw - 1, W), st),
                   pl.BlockSpec((1, 1, W), st)],
        out_shape=[jax.ShapeDtypeStruct((B, T, W), BF16),
                   jax.ShapeDtypeStruct((B, cw - 1, W), F32),
                   jax.ShapeDtypeStruct((B, 1, W), F32)],
        scratch_shapes=[pltpu.VMEM((SUBLANES, W), F32),
                        pltpu.VMEM((tb, W), F32),
                        pltpu.VMEM((tb, W), F32)],
        compiler_params=_params("parallel", "arbitrary"),
        name="lru",
    )(u, u, conv0, h0, conv_w, conv_b, wa, ba, wx, bx, lam)


def _lora_math(x, w1_ref, w2_ref, bias_ref, mid, post):
    z = jnp.dot(x, w1_ref[...], preferred_element_type=F32)
    if mid == "tanh":
        z = jnp.tanh(z)
    elif mid == "sigmoid":
        z = _sigmoid(z)
    y = jnp.dot(z.astype(BF16), w2_ref[...], preferred_element_type=F32)
    if post == "decay":
        y = jnp.exp(-EXP_NEG_HALF * _sigmoid(bias_ref[...] + y))
    elif post == "sigmoid":
        y = _sigmoid(bias_ref[...] + y)
    return y


def _rwkv_scan_kernel(r_ref, k_ref, v_ref, w_ref, a_ref, kk_p, ka_p, rk_p, gg_p, gb_p, s0_ref,
                      y_ref, s1_ref, nkk_s, kka_s, km_s, vt_s, ys_s, *, tt, n):
    t = pl.program_id(1)
    nb = n // SUBLANES

    @pl.when(t == 0)
    def _():
        s1_ref[...] = s0_ref[...]

    def prow(p_ref, f):
        return p_ref[f:f + 1, :]

    nrm = jnp.zeros((tt, LANES), F32)
    for f in range(n):
        kk = k_ref[0, f] * prow(kk_p, f)
        nrm = nrm + kk * kk
    inv = lax.rsqrt(jnp.maximum(nrm, 1e-24))
    for f in range(n):
        kf, af = k_ref[0, f], a_ref[0, f]
        kk = kf * prow(kk_p, f) * inv
        nkk_s[f] = -kk
        kka_s[f] = kk * af
        km_s[f] = kf * (1.0 + (af - 1.0) * prow(ka_p, f))

    vt_s[...] = jnp.swapaxes(v_ref[0], 0, 1)
    zeros = tuple(jnp.zeros((SUBLANES, LANES), F32) for _ in range(nb))

    def reduce_keys(kx, acc):
        nk = nkk_s[kx, pl.ds(0, 1), :]
        return tuple(acc[jb] + s1_ref[0, jb, kx] * nk for jb in range(nb))

    sa0 = lax.fori_loop(0, n, reduce_keys, zeros, unroll=8)

    def time_step(i, sa):
        row = pl.ds(i, 1)
        nxt = pl.ds(jnp.minimum(i + 1, tt - 1), 1)
        vt = [vt_s[i, jb * SUBLANES:(jb + 1) * SUBLANES, :] for jb in range(nb)]

        def update_keys(kx, acc):
            wr = w_ref[0, kx, row, :]
            ar = kka_s[kx, row, :]
            mr = km_s[kx, row, :]
            rr = r_ref[0, kx, row, :]
            nk = nkk_s[kx, nxt, :]
            ys, sn = [], []
            for jb in range(nb):
                s = s1_ref[0, jb, kx] * wr + sa[jb] * ar + vt[jb] * mr
                s1_ref[0, jb, kx] = s
                ys.append(acc[jb] + s * rr)
                sn.append(acc[nb + jb] + s * nk)
            return tuple(ys + sn)

        acc = lax.fori_loop(0, n, update_keys, zeros + zeros, unroll=32)
        for jb in range(nb):
            ys_s[i, jb * SUBLANES:(jb + 1) * SUBLANES, :] = acc[jb]
        return tuple(acc[nb:])

    lax.fori_loop(0, tt, time_step, sa0)
    y_ref[0] = jnp.swapaxes(ys_s[...], 0, 1)

    mu = jnp.zeros((tt, LANES), F32)
    cb = jnp.zeros((tt, LANES), F32)
    for f in range(n):
        mu = mu + y_ref[0, f]
        cb = cb + r_ref[0, f] * km_s[f] * prow(rk_p, f)
    mu = mu * (1.0 / n)
    var = jnp.zeros((tt, LANES), F32)
    for f in range(n):
        d = y_ref[0, f] - mu
        var = var + d * d
    rs = lax.rsqrt(var * (1.0 / n) + RWKV_GN_EPS)
    for f in range(n):
        y_ref[0, f] = (y_ref[0, f] - mu) * rs * prow(gg_p, f) + prow(gb_p, f) + cb * v_ref[0, f]


def rwkv_scan(r, k, v, w, a, kk_p, ka_p, rk_p, gg_p, gb_p, s0):
    G, n, T, _ = r.shape
    tt = _tile(T, 64)
    seq = pl.BlockSpec((1, n, tt, LANES), lambda g, t: (g, 0, t, 0))
    par = pl.BlockSpec((n, LANES), lambda g, t: (0, 0))
    st = pl.BlockSpec((1, n // SUBLANES, n, SUBLANES, LANES), lambda g, t: (g, 0, 0, 0, 0))
    return pl.pallas_call(
        functools.partial(_rwkv_scan_kernel, tt=tt, n=n),
        grid=(G, T // tt),
        in_specs=[seq] * 5 + [par] * 5 + [st],
        out_specs=[seq, st],
        out_shape=[jax.ShapeDtypeStruct((G, n, T, LANES), F32),
                   jax.ShapeDtypeStruct(s0.shape, F32)],
        scratch_shapes=[pltpu.VMEM((n, tt, LANES), F32)] * 3 + [pltpu.VMEM((tt, n, LANES), F32)] * 2,
        compiler_params=_params("parallel", "arbitrary"),
        name="rwkv_scan",
    )(r, k, v, w, a, kk_p, ka_p, rk_p, gg_p, gb_p, s0)


def _store_scan_layout(res, z_ref, heads, c0=0):
    bl = LANES // heads
    for c in range(res.shape[1] // LANES):
        sub = [res[s * LANES:(s + 1) * LANES, c * LANES:(c + 1) * LANES].T for s in range(bl)]
        for ni in range(bl):
            tile = jnp.concatenate([sub[s][ni * heads:(ni + 1) * heads, :] for s in range(bl)], 0)
            z_ref[0, (c0 + c) * bl + ni] = tile.T


MXU_WIDTH = 256


def _mix_proj_kernel(x_ref, prev_ref, shift_ref, mu_ref, *refs, heads, projs):
    x = x_ref[...]
    bl, tq, D = x.shape
    prev = jnp.where(pl.program_id(1) == 0, shift_ref[...], prev_ref[:, SUBLANES - 1:, :])
    xs = pltpu.roll(x, 1, 1)
    head_rows = lax.broadcasted_iota(jnp.int32, (bl, SUBLANES, D), 1)
    first = jnp.where(head_rows == 0, prev, xs[:, :SUBLANES, :])
    xx = jnp.concatenate([first, xs[:, SUBLANES:, :]], 1) - x
    o_refs = refs[len(refs) - len(projs):]
    at = 0
    for (j, n_w, lora_args, scan_out), o_ref in zip(projs, o_refs):
        w_refs = refs[at:at + n_w]
        at += n_w
        a = (x + xx * mu_ref[j]).astype(BF16).reshape(bl * tq, D)
        if lora_args is not None:
            res = _lora_math(a, *w_refs, *lora_args)
            if scan_out:
                _store_scan_layout(res, o_ref, heads)
            else:
                o_ref[...] = res.reshape(bl, tq, res.shape[1])
        elif scan_out:
            for c in range(w_refs[0].shape[1] // MXU_WIDTH):
                cols = slice(c * MXU_WIDTH, (c + 1) * MXU_WIDTH)
                res = jnp.dot(a, w_refs[0][:, cols], preferred_element_type=F32)
                _store_scan_layout(res, o_ref, heads, c * (MXU_WIDTH // LANES))
        else:
            res = jnp.dot(a, w_refs[0][...], preferred_element_type=F32)
            o_ref[...] = res.reshape(bl, tq, res.shape[1])


def mix_proj(x, shift, mu, heads, projs):
    B, T, D = x.shape
    bl = LANES // heads
    tq = min(T, LANES)
    assert T % tq == 0 and tq % SUBLANES == 0
    steps = tq // SUBLANES
    const = lambda w: pl.BlockSpec(w.shape, lambda g, t: (0,) * w.ndim, pipeline_mode=pl.Buffered(1))
    out_specs, out_shapes, weights, static = [], [], [], []
    for j, ws, scan_out, lora_args in projs:
        N = ws[-1].shape[1]
        assert not (scan_out and tq != LANES)
        if scan_out:
            out_specs.append(pl.BlockSpec((1, N // heads, LANES, LANES), lambda g, t: (g, 0, t, 0)))
            out_shapes.append(jax.ShapeDtypeStruct((B // bl, N // heads, T, LANES), F32))
        else:
            out_specs.append(pl.BlockSpec((bl, tq, N), lambda g, t: (g, t, 0)))
            out_shapes.append(jax.ShapeDtypeStruct((B, T, N), F32))
        weights += list(ws)
        static.append((j, len(ws), lora_args, scan_out))
    n_mix = mu.shape[0]
    return pl.pallas_call(
        functools.partial(_mix_proj_kernel, heads=heads, projs=tuple(static)),
        grid=(B // bl, T // tq),
        in_specs=[pl.BlockSpec((bl, tq, D), lambda g, t: (g, t, 0)),
                  pl.BlockSpec((bl, SUBLANES, D), lambda g, t: (g, jnp.maximum(t * steps - 1, 0), 0)),
                  pl.BlockSpec((bl, 1, D), lambda g, t: (g, 0, 0)),
                  pl.BlockSpec((n_mix, 1, D), lambda g, t: (0, 0, 0))] + [const(w) for w in weights],
        out_specs=out_specs,
        out_shape=out_shapes,
        compiler_params=_params("parallel", "arbitrary"),
        name="mix_proj",
    )(x, x, shift, mu.reshape(n_mix, 1, D), *weights)


def _scan_proj_ln_kernel(y_ref, gate_ref, w_ref, x_ref, g_ref, b_ref, o_ref, *, heads, alpha):
    bl = LANES // heads
    D = o_ref.shape[2]
    per = MXU_WIDTH // LANES
    acc = None
    for kc in range(D // MXU_WIDTH):
        parts = []
        for c in range(kc * per, (kc + 1) * per):
            sub = [y_ref[0, c * bl + ni].T for ni in range(bl)]
            cols = slice(c * LANES, (c + 1) * LANES)
            tiles = []
            for s in range(bl):
                tile = jnp.concatenate([sub[ni][s * heads:(s + 1) * heads, :] for ni in range(bl)], 0)
                tiles.append((tile.T * gate_ref[s, :, cols]).astype(BF16))
            parts.append(jnp.concatenate(tiles, 0))
        a = jnp.concatenate(parts, 1)
        part = jnp.dot(a, w_ref[kc * MXU_WIDTH:(kc + 1) * MXU_WIDTH, :], preferred_element_type=F32)
        acc = part if acc is None else acc + part
    y = alpha * x_ref[...].reshape(bl * LANES, D) + acc
    o_ref[...] = _layer_norm(y, g_ref[...], b_ref[...]).reshape(bl, LANES, D)


def scan_proj_ln(y, gate, w, x, g, b, alpha, heads):
    G, n, T, _ = y.shape
    B, _, D = x.shape
    bl = LANES // heads
    tok = pl.BlockSpec((bl, LANES, D), lambda g_, t: (g_, t, 0))
    vec = pl.BlockSpec((1, D), lambda g_, t: (0, 0))
    return pl.pallas_call(
        functools.partial(_scan_proj_ln_kernel, heads=heads, alpha=alpha),
        grid=(G, T // LANES),
        in_specs=[pl.BlockSpec((1, n, LANES, LANES), lambda g_, t: (g_, 0, t, 0)), tok,
                  pl.BlockSpec((D, D), lambda g_, t: (0, 0), pipeline_mode=pl.Buffered(1)),
                  tok, vec, vec],
        out_specs=tok,
        out_shape=jax.ShapeDtypeStruct((B, T, D), F32),
        compiler_params=_params("parallel", "parallel"),
        name="scan_proj_ln",
    )(y, gate, w, x, g.reshape(1, D), b.reshape(1, D))


def _pad_cols(w, n):
    return jnp.pad(w, ((0, 0), (0, n - w.shape[1])))


def _pad_rows(w, n):
    return jnp.pad(w, ((0, n - w.shape[0]), (0, 0)))


def _layer_a(x, st, p, li, reset_first, alpha):
    c0, n0, m0, conv0, h0 = st
    B, T, D = x.shape
    heads = p['a_b_ig'].shape[1]
    dv = p['a_mlstm_norm'].shape[1] // heads
    dk = dv // 2
    qw, vw = heads * dk, heads * dv
    W = p['a_conv_w'].shape[2]
    w_in = p['a_w_in'][li]
    n_qkvo = 2 * qw + 2 * vw
    w_gate = _pad_cols(w_in[:, n_qkvo:n_qkvo + 2 * heads], LANES).astype(BF16)
    w_u = jnp.concatenate([w_in[:, :n_qkvo], w_in[:, n_qkvo + 2 * heads:]], 1).astype(BF16)
    x2 = x.reshape(B * T, D)
    u = matmul(x2, w_u).reshape(B, T, n_qkvo + 2 * W)
    gbias = _pad_cols(jnp.concatenate([p['a_b_ig'][li], p['a_b_fg'][li]])[None, :], LANES)
    hm, c1, n1, m1 = mlstm(u, x, w_gate, gbias, p['a_mlstm_norm'][li][None, :],
                           c0, n0, m0.reshape(B, 1, heads), heads, dk, dv)
    yb, conv1, h1 = lru(u, n_qkvo, conv0, h0.reshape(B, 1, W), p['a_conv_w'][li], p['a_conv_b'][li][None, :],
                        p['a_lru_wa'][li].astype(BF16), p['a_lru_ba'][li][None, :],
                        p['a_lru_wx'][li].astype(BF16), p['a_lru_bx'][li][None, :],
                        p['a_lru_lambda'][li][None, :], reset_first)
    assert vw == W
    y = proj_ln([hm.reshape(B * T, vw), yb.reshape(B * T, W)], None, p['a_w_out'][li].astype(BF16), x2,
                p['ln1_g'][2 * li], p['ln1_b'][2 * li], alpha)
    return y.reshape(B, T, D), (c1, n1, m1.reshape(B, heads), conv1, h1.reshape(B, W))


def _layer_c(x, shift, s0, p, li, layer, alpha):
    B, T, D = x.shape
    H, N = p['c_r_k'].shape[1:]
    M = B * T
    bl = LANES // H
    G = B // bl
    nb = N // SUBLANES
    def cols(w):
        return w.reshape(w.shape[0], H, N).swapaxes(1, 2).reshape(w.shape[0], D)

    w_r = cols(p['c_w_r'][li].astype(BF16))
    w_k = cols(p['c_w_k'][li].astype(BF16))
    w_v = cols(p['c_w_v'][li].astype(BF16))
    rd = -(-p['c_w1'].shape[2] // LANES) * LANES
    ra = -(-p['c_a1'].shape[2] // LANES) * LANES
    w_d = (_pad_cols(p['c_w1'][li], rd).astype(BF16), _pad_rows(cols(p['c_w2'][li]), rd).astype(BF16),
           cols(p['c_w0'][li][None, :]))
    w_a = (_pad_cols(p['c_a1'][li], ra).astype(BF16), _pad_rows(cols(p['c_a2'][li]), ra).astype(BF16),
           cols(p['c_a0'][li][None, :]))
    w_g = (p['c_g1'][li].astype(BF16), cols(p['c_g2'][li].astype(BF16)), jnp.zeros((1, D), F32))
    fused = T % LANES == 0
    mix = functools.partial(mix_proj, x, shift.reshape(B, 1, D), p['c_mu'][li], H)
    r, = mix([(0, (w_r,), fused, None)])
    k, = mix([(2, (w_k,), fused, None)])
    v, = mix([(3, (w_v,), fused, None)])
    decay, a, g = mix([(1, w_d, fused, ("tanh", "decay")),
                       (4, w_a, fused, ("none", "sigmoid")),
                       (5, w_g, False, ("sigmoid", "none"))])
    if not fused:
        def to_scan(z):
            return z.reshape(G, bl, T, N, H).transpose(0, 3, 2, 1, 4).reshape(G, N, T, LANES)

        r, k, v, decay, a = (to_scan(z) for z in (r, k, v, decay, a))

    def par(z):
        return jnp.tile(z.reshape(H, N).T, (1, bl))

    s0t = s0.reshape(G, bl, H, nb, SUBLANES, N).transpose(0, 3, 5, 4, 1, 2).reshape(G, nb, N, SUBLANES, LANES)
    yt, s1t = rwkv_scan(r, k, v, decay, a,
                        par(p['c_k_k'][li]), par(p['c_k_a'][li]), par(p['c_r_k'][li].reshape(D)),
                        par(p['c_gn_g'][li]), par(p['c_gn_b'][li]), s0t)
    s1 = s1t.reshape(G, nb, N, SUBLANES, bl, H).transpose(0, 4, 5, 1, 3, 2).reshape(B, H, N, N)
    w_o = p['c_w_o'][li].astype(BF16).reshape(H, N, D).swapaxes(0, 1).reshape(D, D)
    if fused:
        out = scan_proj_ln(yt, g, w_o, x, p['ln1_g'][layer], p['ln1_b'][layer], alpha, H)
    else:
        y = yt.reshape(G, N, T, bl, H).transpose(0, 3, 2, 1, 4).reshape(M, D)
        out = proj_ln([y], g.reshape(M, D), w_o, x.reshape(M, D), p['ln1_g'][layer], p['ln1_b'][layer], alpha)
    return out.reshape(B, T, D), (x[:, -1], s1)


def _trunk(x, states, p, reset_first):
    mC, mn, mm, cv, hl, sh, S = states
    depth = p['ln1_g'].shape[0]
    alpha = (2 * depth) ** 0.25
    B, T, D = x.shape
    new_a, new_c = [], []
    for layer in range(depth):
        li = layer // 2
        if layer % 2 == 0:
            x, st = _layer_a(x, (mC[li], mn[li], mm[li], cv[li], hl[li]), p, li, reset_first, alpha)
            new_a.append(st)
        else:
            x, st = _layer_c(x, sh[li], S[li], p, li, layer, alpha)
            new_c.append(st)
        x = mlp_ln(x.reshape(B * T, D), p['mlp_w1'].astype(BF16), p['mlp_w2'].astype(BF16), layer,
                   p['ln2_g'][layer], p['ln2_b'][layer], alpha).reshape(B, T, D)
    sa = [jnp.stack([s[j] for s in new_a]) for j in range(5)]
    sc = [jnp.stack([s[j] for s in new_c]) for j in range(2)]
    return x, sa + sc


def kernel(x_prompt, x_sample, state_mlstm_C, state_mlstm_n, state_mlstm_m, state_lru_conv, state_lru_h,
           state_rwkv_shift, state_rwkv_S, a_w_in, a_b_ig, a_b_fg, a_mlstm_norm, a_conv_w, a_conv_b,
           a_lru_wa, a_lru_ba, a_lru_wx, a_lru_bx, a_lru_lambda, a_w_out, c_mu, c_w_r, c_w_k, c_w_v,
           c_w0, c_w1, c_w2, c_a0, c_a1, c_a2, c_g1, c_g2, c_k_k, c_k_a, c_r_k, c_gn_g, c_gn_b, c_w_o,
           ln1_g, ln1_b, ln2_g, ln2_b, mlp_w1, mlp_w2):
    p = dict(a_w_in=a_w_in, a_b_ig=a_b_ig, a_b_fg=a_b_fg, a_mlstm_norm=a_mlstm_norm, a_conv_w=a_conv_w,
             a_conv_b=a_conv_b, a_lru_wa=a_lru_wa, a_lru_ba=a_lru_ba, a_lru_wx=a_lru_wx, a_lru_bx=a_lru_bx,
             a_lru_lambda=a_lru_lambda, a_w_out=a_w_out, c_mu=c_mu, c_w_r=c_w_r, c_w_k=c_w_k, c_w_v=c_w_v,
             c_w0=c_w0, c_w1=c_w1, c_w2=c_w2, c_a0=c_a0, c_a1=c_a1, c_a2=c_a2, c_g1=c_g1, c_g2=c_g2,
             c_k_k=c_k_k, c_k_a=c_k_a, c_r_k=c_r_k, c_gn_g=c_gn_g, c_gn_b=c_gn_b, c_w_o=c_w_o,
             ln1_g=ln1_g, ln1_b=ln1_b, ln2_g=ln2_g, ln2_b=ln2_b, mlp_w1=mlp_w1, mlp_w2=mlp_w2)
    Bp = x_prompt.shape[0]
    init = tuple(jnp.zeros((s.shape[0], Bp) + s.shape[2:], s.dtype)
                 for s in (state_mlstm_C, state_mlstm_n, state_mlstm_m, state_lru_conv, state_lru_h,
                           state_rwkv_shift, state_rwkv_S))
    y_prompt, ps = _trunk(x_prompt, init, p, True)
    y_sample, ss = _trunk(x_sample, (state_mlstm_C, state_mlstm_n, state_mlstm_m, state_lru_conv,
                                     state_lru_h, state_rwkv_shift, state_rwkv_S), p, False)
    return (y_prompt, y_sample, *ps, *ss)
```

```python
import functools

import jax
import jax.numpy as jnp
from jax import lax
from jax.experimental import pallas as pl
from jax.experimental.pallas import tpu as pltpu

F32 = jnp.float32
BF16 = jnp.bfloat16

LANES = 128
SUBLANES = 8
VMEM_LIMIT_BYTES = 56 * 1024 * 1024

CHUNK = 256
LRU_C = 8.0
LN_EPS = 1e-5
RWKV_GN_EPS = 64e-5
HIGHEST = lax.Precision.HIGHEST
EXP_NEG_HALF = 0.6065306597126334


def _params(*sem):
    return pltpu.CompilerParams(dimension_semantics=sem, vmem_limit_bytes=VMEM_LIMIT_BYTES)


def _tile(n, pref):
    t = min(n, pref)
    while n % t:
        t -= 1
    return t


def _sigmoid(x):
    return 1.0 / (1.0 + jnp.exp(-x))


def _softplus(x):
    return jnp.maximum(x, 0.0) + jnp.log1p(jnp.exp(-jnp.abs(x)))


def _layer_norm(y, g, b):
    mu = jnp.mean(y, -1, keepdims=True)
    d = y - mu
    var = jnp.mean(d * d, -1, keepdims=True)
    return d * lax.rsqrt(var + LN_EPS) * g + b


def _mm_kernel(a_ref, w_ref, o_ref, ab_ref):
    @pl.when(pl.program_id(1) == 0)
    def _():
        ab_ref[...] = a_ref[...].astype(BF16)

    o_ref[...] = jnp.dot(ab_ref[...], w_ref[...], preferred_element_type=F32)


def matmul(a, w, tm=1024, tn=1024):
    M, K = a.shape
    N = w.shape[1]
    tm, tn = _tile(M, tm), _tile(N, tn)
    return pl.pallas_call(
        _mm_kernel,
        grid=(M // tm, N // tn),
        in_specs=[pl.BlockSpec((tm, K), lambda i, j: (i, 0)),
                  pl.BlockSpec((K, tn), lambda i, j: (0, j))],
        out_specs=pl.BlockSpec((tm, tn), lambda i, j: (i, j)),
        out_shape=jax.ShapeDtypeStruct((M, N), F32),
        scratch_shapes=[pltpu.VMEM((tm, K), BF16)],
        compiler_params=_params("parallel", "arbitrary"),
        name="matmul",
    )(a, w)


def _proj_ln_kernel(*refs, n_a, alpha, gated):
    a_refs, rest = refs[:n_a], list(refs[n_a:])
    gate_ref = rest.pop(0) if gated else None
    w_ref, x_ref, g_ref, b_ref, o_ref = rest
    a = jnp.concatenate([r[...] for r in a_refs], 1) if n_a > 1 else a_refs[0][...]
    if gated:
        a = a * gate_ref[...]
    acc = jnp.dot(a.astype(BF16), w_ref[...], preferred_element_type=F32)
    o_ref[...] = _layer_norm(alpha * x_ref[...] + acc, g_ref[...], b_ref[...])


PROJ_VMEM_BUDGET = 46 * 1024 * 1024


def proj_ln(a_list, gate, w, x, g, b, alpha):
    n_a = len(a_list)
    M, K = a_list[0].shape
    D = w.shape[1]
    assert w.shape[0] == n_a * K and not (gate is not None and n_a > 1)
    gated = gate is not None

    def vmem_bytes(tm):
        blocks = sum(tm * K * a.dtype.itemsize for a in a_list) + gated * tm * K * 4 + 2 * tm * D * 4
        return 2 * blocks + w.size * w.dtype.itemsize

    tm = next(t for t in (512, 256, 128, 64, 32, 16, 8) if M % t == 0 and vmem_bytes(t) <= PROJ_VMEM_BUDGET)
    a_spec = pl.BlockSpec((tm, K), lambda i: (i, 0))
    row_spec = pl.BlockSpec((tm, D), lambda i: (i, 0))
    vec_spec = pl.BlockSpec((1, D), lambda i: (0, 0))
    in_specs = [a_spec] * n_a + [a_spec] * gated
    in_specs += [pl.BlockSpec(w.shape, lambda i: (0, 0), pipeline_mode=pl.Buffered(1)),
                 row_spec, vec_spec, vec_spec]
    args = list(a_list) + [gate] * gated + [w, x, g.reshape(1, D), b.reshape(1, D)]
    return pl.pallas_call(
        functools.partial(_proj_ln_kernel, n_a=n_a, alpha=alpha, gated=gated),
        grid=(M // tm,),
        in_specs=in_specs,
        out_specs=row_spec,
        out_shape=jax.ShapeDtypeStruct((M, D), F32),
        compiler_params=_params("parallel"),
        name="proj_ln",
    )(*args)


def _mlp_ln_kernel(x_ref, w1_ref, w2_ref, g_ref, b_ref, o_ref, xb_ref, *, nf, alpha):
    f = pl.program_id(1)

    @pl.when(f == 0)
    def _():
        o_ref[...] = jnp.zeros_like(o_ref)
        xb_ref[...] = x_ref[...].astype(BF16)

    h = jnp.maximum(jnp.dot(xb_ref[...], w1_ref[0], preferred_element_type=F32), 0.0)
    o_ref[...] += jnp.dot((h * h).astype(BF16), w2_ref[0], preferred_element_type=F32)

    @pl.when(f == nf - 1)
    def _():
        y = alpha * x_ref[...] + o_ref[...]
        o_ref[...] = _layer_norm(y, g_ref[...], b_ref[...])


def mlp_ln(x, w1, w2, layer, g, b, alpha, tm=1024, tf=512):
    M, D = x.shape
    FF = w1.shape[2]
    tm, tf = _tile(M, tm), _tile(FF, tf)
    nf = FF // tf
    row_spec = pl.BlockSpec((tm, D), lambda i, f: (i, 0))
    vec_spec = pl.BlockSpec((1, D), lambda i, f: (0, 0))
    return pl.pallas_call(
        functools.partial(_mlp_ln_kernel, nf=nf, alpha=alpha),
        grid=(M // tm, nf),
        in_specs=[row_spec,
                  pl.BlockSpec((1, D, tf), lambda i, f: (layer, 0, f)),
                  pl.BlockSpec((1, tf, D), lambda i, f: (layer, f, 0)),
                  vec_spec, vec_spec],
        out_specs=row_spec,
        out_shape=jax.ShapeDtypeStruct((M, D), F32),
        scratch_shapes=[pltpu.VMEM((tm, D), BF16)],
        compiler_params=_params("parallel", "arbitrary"),
        name="mlp_ln",
    )(x, w1, w2, g.reshape(1, D), b.reshape(1, D))


def _mlstm_kernel(q_ref, k_ref, v_ref, o_ref, x_ref, wg_ref, gbias_ref, nw_ref, c0_ref, n0_ref, m0_ref,
                  h_ref, c1_ref, n1_ref, m1_ref, *, heads, dk, dv, L):
    t = pl.program_id(1)

    @pl.when(t == 0)
    def _():
        c1_ref[...] = c0_ref[...]
        n1_ref[...] = n0_ref[...]
        m1_ref[...] = m0_ref[...]

    row = lax.broadcasted_iota(jnp.int32, (L, L), 0)
    col = lax.broadcasted_iota(jnp.int32, (L, L), 1)
    causal = row >= col
    tril = causal.astype(F32)
    sel_r = lax.broadcasted_iota(jnp.int32, (2 * heads, LANES), 0)
    sel_c = lax.broadcasted_iota(jnp.int32, (2 * heads, LANES), 1)
    pick = (sel_r == sel_c).astype(F32)
    lane = lax.broadcasted_iota(jnp.int32, (L, LANES), 1)
    scale = dk ** -0.5

    gt = jnp.dot(x_ref[0].astype(BF16), wg_ref[...], preferred_element_type=F32) + gbias_ref[...]
    logf = jnp.minimum(gt, 0.0) - jnp.log1p(jnp.exp(-jnp.abs(gt)))
    gl = jnp.where(lane < heads, gt, logf)
    cum = jnp.dot(tril, gl, preferred_element_type=F32, precision=HIGHEST)
    nt_dims = (((1,), (1,)), ((), ()))
    gl_t = lax.dot_general(pick, gl, nt_dims, preferred_element_type=F32, precision=HIGHEST)
    cum_t = lax.dot_general(pick, cum, nt_dims, preferred_element_type=F32, precision=HIGHEST)
    for h in range(heads):
        ig_col = gl[:, h:h + 1]
        b_col = cum[:, heads + h:heads + h + 1]
        ig_row = gl_t[h:h + 1, :]
        b_row = cum_t[heads + h:heads + h + 1, :]
        b_last = b_row[:, L - 1:L]
        m_prev = m1_ref[0, :, h:h + 1]
        qh = (q_ref[0, :, h * dk:(h + 1) * dk] * scale).astype(BF16)
        kf = k_ref[0, :, h * dk:(h + 1) * dk]
        kh = kf.astype(BF16)
        vh = v_ref[0, :, h * dv:(h + 1) * dv].astype(BF16)
        c_prev = c1_ref[0, h]
        n_prev = n1_ref[0, h:h + 1, :]

        dmat = jnp.where(causal, b_col - b_row + ig_row, -jnp.inf)
        inter = b_col + m_prev
        m_t = jnp.maximum(inter, jnp.max(dmat, -1, keepdims=True))
        w_intra = jnp.exp(dmat - m_t)
        w_inter = jnp.exp(inter - m_t)
        s = lax.dot_general(qh, kh, nt_dims, preferred_element_type=F32)
        qk = s * w_intra
        num = jnp.dot(qk.astype(BF16), vh, preferred_element_type=F32)
        num = num + w_inter * jnp.dot(qh, c_prev.astype(BF16), preferred_element_type=F32)
        qn = jnp.sum(qh.astype(F32) * n_prev, -1, keepdims=True)
        den = jnp.sum(qk, -1, keepdims=True) + w_inter * qn
        hh = num / jnp.maximum(jnp.abs(den), jnp.exp(-m_t))

        mu = jnp.mean(hh, -1, keepdims=True)
        d = hh - mu
        var = jnp.mean(d * d, -1, keepdims=True)
        hn = d * lax.rsqrt(var + LN_EPS)
        og = o_ref[0, :, h * dv:(h + 1) * dv]
        hg = hn * nw_ref[:, h * dv:(h + 1) * dv] * _sigmoid(og)
        h_ref[0, :, h * dv:(h + 1) * dv] = hg.astype(h_ref.dtype)

        lw_col = b_last - b_col + ig_col
        lw_row = b_last - b_row + ig_row
        m_new = jnp.maximum(b_last + m_prev, jnp.max(lw_row, -1, keepdims=True))
        ws_col = jnp.exp(lw_col - m_new)
        wc = jnp.exp(b_last + m_prev - m_new)
        kw = kf * ws_col
        tn_dims = (((0,), (0,)), ((), ()))
        c1_ref[0, h] = wc * c_prev + lax.dot_general(kw.astype(BF16), vh, tn_dims,
                                                     preferred_element_type=F32)
        n1_ref[0, h:h + 1, :] = wc * n_prev + jnp.sum(kw, 0, keepdims=True)
        m1_ref[0, :, h:h + 1] = m_new


def mlstm(u, x, w_gate, gbias, norm_w, c0, n0, m0, heads, dk, dv):
    B, T, D = x.shape
    L = min(T, CHUNK)
    assert T % L == 0
    tb = L
    qw, vw = heads * dk, heads * dv
    assert qw % LANES == 0 and vw == 2 * qw
    st = lambda i, t: (i, 0, 0)
    kern = functools.partial(_mlstm_kernel, heads=heads, dk=dk, dv=dv, L=L)
    return pl.pallas_call(
        kern,
        grid=(B, T // tb),
        in_specs=[pl.BlockSpec((1, tb, qw), lambda i, t: (i, t, 0)),
                  pl.BlockSpec((1, tb, qw), lambda i, t: (i, t, 1)),
                  pl.BlockSpec((1, tb, vw), lambda i, t: (i, t, 1)),
                  pl.BlockSpec((1, tb, vw), lambda i, t: (i, t, 2)),
                  pl.BlockSpec((1, tb, D), lambda i, t: (i, t, 0)),
                  pl.BlockSpec((D, LANES), lambda i, t: (0, 0)),
                  pl.BlockSpec((1, LANES), lambda i, t: (0, 0)),
                  pl.BlockSpec((1, vw), lambda i, t: (0, 0)),
                  pl.BlockSpec((1, heads, dk, dv), lambda i, t: (i, 0, 0, 0)),
                  pl.BlockSpec((1, heads, dk), st),
                  pl.BlockSpec((1, 1, heads), st)],
        out_specs=[pl.BlockSpec((1, tb, vw), lambda i, t: (i, t, 0)),
                   pl.BlockSpec((1, heads, dk, dv), lambda i, t: (i, 0, 0, 0)),
                   pl.BlockSpec((1, heads, dk), st),
                   pl.BlockSpec((1, 1, heads), st)],
        out_shape=[jax.ShapeDtypeStruct((B, T, vw), BF16),
                   jax.ShapeDtypeStruct((B, heads, dk, dv), F32),
                   jax.ShapeDtypeStruct((B, heads, dk), F32),
                   jax.ShapeDtypeStruct((B, 1, heads), F32)],
        compiler_params=_params("parallel", "arbitrary"),
        name="mlstm",
    )(u, u, u, u, x, w_gate, gbias, norm_w, c0, n0, m0)


def _gelu_tanh(x):
    return 0.5 * x * (1.0 + jnp.tanh(0.7978845608028654 * (x + 0.044715 * x * x * x)))


def _lru_kernel(xr_ref, yg_ref, conv0_ref, h0_ref, cw_ref, cb_ref, wa_ref, ba_ref, wx_ref, bx_ref, lam_ref,
                y_ref, conv1_ref, h1_ref, xp_ref, a_ref, b_ref, *, tb, blocks, cw, reset_first):
    t = pl.program_id(1)
    halo = cw - 1

    @pl.when(t == 0)
    def _():
        xp_ref[...] = jnp.zeros_like(xp_ref)
        xp_ref[SUBLANES - halo:, :] = conv0_ref[0]
        h1_ref[0] = h0_ref[0]

    x = xr_ref[0]
    prev = xp_ref[...]
    head_rows = lax.broadcasted_iota(jnp.int32, prev.shape, 0)
    xc = cb_ref[...] + x * cw_ref[cw - 1:cw, :]
    for d in range(1, cw):
        xs = pltpu.roll(x, d, 0)
        first = jnp.where(head_rows < d, pltpu.roll(prev, d, 0), xs[:SUBLANES])
        xs = jnp.concatenate([first, xs[SUBLANES:]], 0)
        xc = xc + xs * cw_ref[cw - 1 - d:cw - d, :]

    sp = _softplus(-lam_ref[...])
    bw = xc.shape[1] // blocks
    for g in range(blocks):
        sl = slice(g * bw, (g + 1) * bw)
        xg = xc[:, sl]
        xgb = xg.astype(BF16)
        gr = _sigmoid(jnp.dot(xgb, wa_ref[g], preferred_element_type=F32) + ba_ref[:, sl])
        gi = _sigmoid(jnp.dot(xgb, wx_ref[g], preferred_element_type=F32) + bx_ref[:, sl])
        log_a = -LRU_C * gr * sp[:, sl]
        th = jnp.tanh(log_a)
        z = -2.0 * th / (1.0 - th)
        mult = jnp.where(z > 0.0, z * lax.rsqrt(z), 0.0)
        if reset_first:
            first = (lax.broadcasted_iota(jnp.int32, mult.shape, 0) == 0) & (t == 0)
            mult = jnp.where(first, 1.0, mult)
        a_ref[:, sl] = jnp.exp(log_a)
        b_ref[:, sl] = mult * gi * xg

    def step(i, h):
        base = pl.multiple_of(i * SUBLANES, SUBLANES)
        rows = []
        for u in range(SUBLANES):
            h = a_ref[pl.ds(base + u, 1), :] * h + b_ref[pl.ds(base + u, 1), :]
            rows.append(h)
        b_ref[pl.ds(base, SUBLANES), :] = jnp.concatenate(rows, 0)
        return h

    h_last = lax.fori_loop(0, tb // SUBLANES, step, h1_ref[0])
    h1_ref[0] = h_last
    y_ref[0] = (b_ref[...] * _gelu_tanh(yg_ref[0])).astype(y_ref.dtype)
    last = xr_ref[0, tb - SUBLANES:, :]
    xp_ref[...] = last
    conv1_ref[0] = last[SUBLANES - halo:, :]


def lru(u, col0, conv0, h0, conv_w, conv_b, wa, ba, wx, bx, lam, reset_first):
    B, T, _ = u.shape
    W = conv_w.shape[1]
    assert col0 % W == 0
    cb0 = col0 // W
    cw = conv_w.shape[0]
    blocks = wa.shape[0]
    tb = _tile(T, 256)
    assert tb >= cw - 1 and cw - 1 <= SUBLANES
    st = lambda i, t: (i, 0, 0)
    vec = pl.BlockSpec((1, W), lambda i, t: (0, 0))
    wsp = pl.BlockSpec(wa.shape, lambda i, t: (0, 0, 0))
    kern = functools.partial(_lru_kernel, tb=tb, blocks=blocks, cw=cw, reset_first=reset_first)
    return pl.pallas_call(
        kern,
        grid=(B, T // tb),
        in_specs=[pl.BlockSpec((1, tb, W), lambda i, t: (i, t, cb0)),
                  pl.BlockSpec((1, tb, W), lambda i, t: (i, t, cb0 + 1)),
                  pl.BlockSpec((1, cw - 1, W), st),
                  pl.BlockSpec((1, 1, W), st),
                  pl.BlockSpec((cw, W), lambda i, t: (0, 0)),
                  vec, wsp, vec, wsp, vec, vec],
        out_specs=[pl.BlockSpec((1, tb, W), lambda i, t: (i, t, 0)),
                   pl.BlockSpec((1, cw - 1, W), st),
                   pl.BlockSpec((1, 1, W), st)],
        out_shape=[jax.ShapeDtypeStruct((B, T, W), BF16),
                   jax.ShapeDtypeStruct((B, cw - 1, W), F32),
                   jax.ShapeDtypeStruct((B, 1, W), F32)],
        scratch_shapes=[pltpu.VMEM((SUBLANES, W), F32),
                        pltpu.VMEM((tb, W), F32),
                        pltpu.VMEM((tb, W), F32)],
        compiler_params=_params("parallel", "arbitrary"),
        name="lru",
    )(u, u, conv0, h0, conv_w, conv_b, wa, ba, wx, bx, lam)


def _lora_math(x, w1_ref, w2_ref, bias_ref, mid, post):
    z = jnp.dot(x, w1_ref[...], preferred_element_type=F32)
    if mid == "tanh":
        z = jnp.tanh(z)
    elif mid == "sigmoid":
        z = _sigmoid(z)
    y = jnp.dot(z.astype(BF16), w2_ref[...], preferred_element_type=F32)
    if post == "decay":
        y = jnp.exp(-EXP_NEG_HALF * _sigmoid(bias_ref[...] + y))
    elif post == "sigmoid":
        y = _sigmoid(bias_ref[...] + y)
    return y


def _rwkv_scan_kernel(r_ref, k_ref, v_ref, w_ref, a_ref, kk_p, ka_p, rk_p, gg_p, gb_p, s0_ref,
                      y_ref, s1_ref, nkk_s, kka_s, km_s, vt_s, ys_s, *, tt, n):
    t = pl.program_id(1)
    nb = n // SUBLANES

    @pl.when(t == 0)
    def _():
        s1_ref[...] = s0_ref[...]

    def prow(p_ref, f):
        return p_ref[f:f + 1, :]

    nrm = jnp.zeros((tt, LANES), F32)
    for f in range(n):
        kk = k_ref[0, f] * prow(kk_p, f)
        nrm = nrm + kk * kk
    inv = lax.rsqrt(jnp.maximum(nrm, 1e-24))
    for f in range(n):
        kf, af = k_ref[0, f], a_ref[0, f]
        kk = kf * prow(kk_p, f) * inv
        nkk_s[f] = -kk
        kka_s[f] = kk * af
        km_s[f] = kf * (1.0 + (af - 1.0) * prow(ka_p, f))

    vt_s[...] = jnp.swapaxes(v_ref[0], 0, 1)
    zeros = tuple(jnp.zeros((SUBLANES, LANES), F32) for _ in range(nb))

    def reduce_keys(kx, acc):
        nk = nkk_s[kx, pl.ds(0, 1), :]
        return tuple(acc[jb] + s1_ref[0, jb, kx] * nk for jb in range(nb))

    sa0 = lax.fori_loop(0, n, reduce_keys, zeros, unroll=8)

    def time_step(i, sa):
        row = pl.ds(i, 1)
        nxt = pl.ds(jnp.minimum(i + 1, tt - 1), 1)
        vt = [vt_s[i, jb * SUBLANES:(jb + 1) * SUBLANES, :] for jb in range(nb)]

        def update_keys(kx, acc):
            wr = w_ref[0, kx, row, :]
            ar = kka_s[kx, row, :]
            mr = km_s[kx, row, :]
            rr = r_ref[0, kx, row, :]
            nk = nkk_s[kx, nxt, :]
            ys, sn = [], []
            for jb in range(nb):
                s = s1_ref[0, jb, kx] * wr + sa[jb] * ar + vt[jb] * mr
                s1_ref[0, jb, kx] = s
                ys.append(acc[jb] + s * rr)
                sn.append(acc[nb + jb] + s * nk)
            return tuple(ys + sn)

        acc = lax.fori_loop(0, n, update_keys, zeros + zeros, unroll=32)
        for jb in range(nb):
            ys_s[i, jb * SUBLANES:(jb + 1) * SUBLANES, :] = acc[jb]
        return tuple(acc[nb:])

    lax.fori_loop(0, tt, time_step, sa0)
    y_ref[0] = jnp.swapaxes(ys_s[...], 0, 1)

    mu = jnp.zeros((tt, LANES), F32)
    cb = jnp.zeros((tt, LANES), F32)
    for f in range(n):
        mu = mu + y_ref[0, f]
        cb = cb + r_ref[0, f] * km_s[f] * prow(rk_p, f)
    mu = mu * (1.0 / n)
    var = jnp.zeros((tt, LANES), F32)
    for f in range(n):
        d = y_ref[0, f] - mu
        var = var + d * d
    rs = lax.rsqrt(var * (1.0 / n) + RWKV_GN_EPS)
    for f in range(n):
        y_ref[0, f] = (y_ref[0, f] - mu) * rs * prow(gg_p, f) + prow(gb_p, f) + cb * v_ref[0, f]


def rwkv_scan(r, k, v, w, a, kk_p, ka_p, rk_p, gg_p, gb_p, s0):
    G, n, T, _ = r.shape
    tt = _tile(T, 64)
    seq = pl.BlockSpec((1, n, tt, LANES), lambda g, t: (g, 0, t, 0))
    par = pl.BlockSpec((n, LANES), lambda g, t: (0, 0))
    st = pl.BlockSpec((1, n // SUBLANES, n, SUBLANES, LANES), lambda g, t: (g, 0, 0, 0, 0))
    return pl.pallas_call(
        functools.partial(_rwkv_scan_kernel, tt=tt, n=n),
        grid=(G, T // tt),
        in_specs=[seq] * 5 + [par] * 5 + [st],
        out_specs=[seq, st],
        out_shape=[jax.ShapeDtypeStruct((G, n, T, LANES), F32),
                   jax.ShapeDtypeStruct(s0.shape, F32)],
        scratch_shapes=[pltpu.VMEM((n, tt, LANES), F32)] * 3 + [pltpu.VMEM((tt, n, LANES), F32)] * 2,
        compiler_params=_params("parallel", "arbitrary"),
        name="rwkv_scan",
    )(r, k, v, w, a, kk_p, ka_p, rk_p, gg_p, gb_p, s0)


def _store_scan_layout(res, z_ref, heads, c0=0):
    bl = LANES // heads
    for c in range(res.shape[1] // LANES):
        sub = [res[s * LANES:(s + 1) * LANES, c * LANES:(c + 1) * LANES].T for s in range(bl)]
        for ni in range(bl):
            tile = jnp.concatenate([sub[s][ni * heads:(ni + 1) * heads, :] for s in range(bl)], 0)
            z_ref[0, (c0 + c) * bl + ni] = tile.T


MXU_WIDTH = 256


def _mix_proj_kernel(x_ref, prev_ref, shift_ref, mu_ref, *refs, heads, projs):
    x = x_ref[...]
    bl, tq, D = x.shape
    prev = jnp.where(pl.program_id(1) == 0, shift_ref[...], prev_ref[:, SUBLANES - 1:, :])
    xs = pltpu.roll(x, 1, 1)
    head_rows = lax.broadcasted_iota(jnp.int32, (bl, SUBLANES, D), 1)
    first = jnp.where(head_rows == 0, prev, xs[:, :SUBLANES, :])
    xx = jnp.concatenate([first, xs[:, SUBLANES:, :]], 1) - x
    o_refs = refs[len(refs) - len(projs):]
    at = 0
    for (j, n_w, lora_args, scan_out), o_ref in zip(projs, o_refs):
        w_refs = refs[at:at + n_w]
        at += n_w
        a = (x + xx * mu_ref[j]).astype(BF16).reshape(bl * tq, D)
        if lora_args is not None:
            res = _lora_math(a, *w_refs, *lora_args)
            if scan_out:
                _store_scan_layout(res, o_ref, heads)
            else:
                o_ref[...] = res.reshape(bl, tq, res.shape[1])
        elif scan_out:
            for c in range(w_refs[0].shape[1] // MXU_WIDTH):
                cols = slice(c * MXU_WIDTH, (c + 1) * MXU_WIDTH)
                res = jnp.dot(a, w_refs[0][:, cols], preferred_element_type=F32)
                _store_scan_layout(res, o_ref, heads, c * (MXU_WIDTH // LANES))
        else:
            res = jnp.dot(a, w_refs[0][...], preferred_element_type=F32)
            o_ref[...] = res.reshape(bl, tq, res.shape[1])


def mix_proj(x, shift, mu, heads, projs):
    B, T, D = x.shape
    bl = LANES // heads
    tq = min(T, LANES)
    assert T % tq == 0 and tq % SUBLANES == 0
    steps = tq // SUBLANES
    const = lambda w: pl.BlockSpec(w.shape, lambda g, t: (0,) * w.ndim, pipeline_mode=pl.Buffered(1))
    out_specs, out_shapes, weights, static = [], [], [], []
    for j, ws, scan_out, lora_args in projs:
        N = ws[-1].shape[1]
        assert not (scan_out and tq != LANES)
        if scan_out:
            out_specs.append(pl.BlockSpec((1, N // heads, LANES, LANES), lambda g, t: (g, 0, t, 0)))
            out_shapes.append(jax.ShapeDtypeStruct((B // bl, N // heads, T, LANES), F32))
        else:
            out_specs.append(pl.BlockSpec((bl, tq, N), lambda g, t: (g, t, 0)))
            out_shapes.append(jax.ShapeDtypeStruct((B, T, N), F32))
        weights += list(ws)
        static.append((j, len(ws), lora_args, scan_out))
    n_mix = mu.shape[0]
    return pl.pallas_call(
        functools.partial(_mix_proj_kernel, heads=heads, projs=tuple(static)),
        grid=(B // bl, T // tq),
        in_specs=[pl.BlockSpec((bl, tq, D), lambda g, t: (g, t, 0)),
                  pl.BlockSpec((bl, SUBLANES, D), lambda g, t: (g, jnp.maximum(t * steps - 1, 0), 0)),
                  pl.BlockSpec((bl, 1, D), lambda g, t: (g, 0, 0)),
                  pl.BlockSpec((n_mix, 1, D), lambda g, t: (0, 0, 0))] + [const(w) for w in weights],
        out_specs=out_specs,
        out_shape=out_shapes,
        compiler_params=_params("parallel", "arbitrary"),
        name="mix_proj",
    )(x, x, shift, mu.reshape(n_mix, 1, D), *weights)


def _scan_proj_ln_kernel(y_ref, gate_ref, w_ref, x_ref, g_ref, b_ref, o_ref, *, heads, alpha):
    bl = LANES // heads
    D = o_ref.shape[2]
    per = MXU_WIDTH // LANES
    acc = None
    for kc in range(D // MXU_WIDTH):
        parts = []
        for c in range(kc * per, (kc + 1) * per):
            sub = [y_ref[0, c * bl + ni].T for ni in range(bl)]
            cols = slice(c * LANES, (c + 1) * LANES)
            tiles = []
            for s in range(bl):
                tile = jnp.concatenate([sub[ni][s * heads:(s + 1) * heads, :] for ni in range(bl)], 0)
                tiles.append((tile.T * gate_ref[s, :, cols]).astype(BF16))
            parts.append(jnp.concatenate(tiles, 0))
        a = jnp.concatenate(parts, 1)
        part = jnp.dot(a, w_ref[kc * MXU_WIDTH:(kc + 1) * MXU_WIDTH, :], preferred_element_type=F32)
        acc = part if acc is None else acc + part
    y = alpha * x_ref[...].reshape(bl * LANES, D) + acc
    o_ref[...] = _layer_norm(y, g_ref[...], b_ref[...]).reshape(bl, LANES, D)


def scan_proj_ln(y, gate, w, x, g, b, alpha, heads):
    G, n, T, _ = y.shape
    B, _, D = x.shape
    bl = LANES // heads
    tok = pl.BlockSpec((bl, LANES, D), lambda g_, t: (g_, t, 0))
    vec = pl.BlockSpec((1, D), lambda g_, t: (0, 0))
    return pl.pallas_call(
        functools.partial(_scan_proj_ln_kernel, heads=heads, alpha=alpha),
        grid=(G, T // LANES),
        in_specs=[pl.BlockSpec((1, n, LANES, LANES), lambda g_, t: (g_, 0, t, 0)), tok,
                  pl.BlockSpec((D, D), lambda g_, t: (0, 0), pipeline_mode=pl.Buffered(1)),
                  tok, vec, vec],
        out_specs=tok,
        out_shape=jax.ShapeDtypeStruct((B, T, D), F32),
        compiler_params=_params("parallel", "parallel"),
        name="scan_proj_ln",
    )(y, gate, w, x, g.reshape(1, D), b.reshape(1, D))


def _pad_cols(w, n):
    return jnp.pad(w, ((0, 0), (0, n - w.shape[1])))


def _pad_rows(w, n):
    return jnp.pad(w, ((0, n - w.shape[0]), (0, 0)))


def _layer_a(x, st, p, li, reset_first, alpha):
    c0, n0, m0, conv0, h0 = st
    B, T, D = x.shape
    heads = p['a_b_ig'].shape[1]
    dv = p['a_mlstm_norm'].shape[1] // heads
    dk = dv // 2
    qw, vw = heads * dk, heads * dv
    W = p['a_conv_w'].shape[2]
    w_in = p['a_w_in'][li]
    n_qkvo = 2 * qw + 2 * vw
    w_gate = _pad_cols(w_in[:, n_qkvo:n_qkvo + 2 * heads], LANES).astype(BF16)
    w_u = jnp.concatenate([w_in[:, :n_qkvo], w_in[:, n_qkvo + 2 * heads:]], 1).astype(BF16)
    x2 = x.reshape(B * T, D)
    u = matmul(x2, w_u).reshape(B, T, n_qkvo + 2 * W)
    gbias = _pad_cols(jnp.concatenate([p['a_b_ig'][li], p['a_b_fg'][li]])[None, :], LANES)
    hm, c1, n1, m1 = mlstm(u, x, w_gate, gbias, p['a_mlstm_norm'][li][None, :],
                           c0, n0, m0.reshape(B, 1, heads), heads, dk, dv)
    yb, conv1, h1 = lru(u, n_qkvo, conv0, h0.reshape(B, 1, W), p['a_conv_w'][li], p['a_conv_b'][li][None, :],
                        p['a_lru_wa'][li].astype(BF16), p['a_lru_ba'][li][None, :],
                        p['a_lru_wx'][li].astype(BF16), p['a_lru_bx'][li][None, :],
                        p['a_lru_lambda'][li][None, :], reset_first)
    assert vw == W
    y = proj_ln([hm.reshape(B * T, vw), yb.reshape(B * T, W)], None, p['a_w_out'][li].astype(BF16), x2,
                p['ln1_g'][2 * li], p['ln1_b'][2 * li], alpha)
    return y.reshape(B, T, D), (c1, n1, m1.reshape(B, heads), conv1, h1.reshape(B, W))


def _layer_c(x, shift, s0, p, li, layer, alpha):
    B, T, D = x.shape
    H, N = p['c_r_k'].shape[1:]
    M = B * T
    bl = LANES // H
    G = B // bl
    nb = N // SUBLANES
    def cols(w):
        return w.reshape(w.shape[0], H, N).swapaxes(1, 2).reshape(w.shape[0], D)

    w_r = cols(p['c_w_r'][li].astype(BF16))
    w_k = cols(p['c_w_k'][li].astype(BF16))
    w_v = cols(p['c_w_v'][li].astype(BF16))
    rd = -(-p['c_w1'].shape[2] // LANES) * LANES
    ra = -(-p['c_a1'].shape[2] // LANES) * LANES
    w_d = (_pad_cols(p['c_w1'][li], rd).astype(BF16), _pad_rows(cols(p['c_w2'][li]), rd).astype(BF16),
           cols(p['c_w0'][li][None, :]))
    w_a = (_pad_cols(p['c_a1'][li], ra).astype(BF16), _pad_rows(cols(p['c_a2'][li]), ra).astype(BF16),
           cols(p['c_a0'][li][None, :]))
    w_g = (p['c_g1'][li].astype(BF16), cols(p['c_g2'][li].astype(BF16)), jnp.zeros((1, D), F32))
    fused = T % LANES == 0
    mix = functools.partial(mix_proj, x, shift.reshape(B, 1, D), p['c_mu'][li], H)
    r, k = mix([(0, (w_r,), fused, None), (2, (w_k,), fused, None)])
    v, = mix([(3, (w_v,), fused, None)])
    decay, a, g = mix([(1, w_d, fused, ("tanh", "decay")),
                       (4, w_a, fused, ("none", "sigmoid")),
                       (5, w_g, False, ("sigmoid", "none"))])
    if not fused:
        def to_scan(z):
            return z.reshape(G, bl, T, N, H).transpose(0, 3, 2, 1, 4).reshape(G, N, T, LANES)

        r, k, v, decay, a = (to_scan(z) for z in (r, k, v, decay, a))

    def par(z):
        return jnp.tile(z.reshape(H, N).T, (1, bl))

    s0t = s0.reshape(G, bl, H, nb, SUBLANES, N).transpose(0, 3, 5, 4, 1, 2).reshape(G, nb, N, SUBLANES, LANES)
    yt, s1t = rwkv_scan(r, k, v, decay, a,
                        par(p['c_k_k'][li]), par(p['c_k_a'][li]), par(p['c_r_k'][li].reshape(D)),
                        par(p['c_gn_g'][li]), par(p['c_gn_b'][li]), s0t)
    s1 = s1t.reshape(G, nb, N, SUBLANES, bl, H).transpose(0, 4, 5, 1, 3, 2).reshape(B, H, N, N)
    w_o = p['c_w_o'][li].astype(BF16).reshape(H, N, D).swapaxes(0, 1).reshape(D, D)
    if fused:
        out = scan_proj_ln(yt, g, w_o, x, p['ln1_g'][layer], p['ln1_b'][layer], alpha, H)
    else:
        y = yt.reshape(G, N, T, bl, H).transpose(0, 3, 2, 1, 4).reshape(M, D)
        out = proj_ln([y], g.reshape(M, D), w_o, x.reshape(M, D), p['ln1_g'][layer], p['ln1_b'][layer], alpha)
    return out.reshape(B, T, D), (x[:, -1], s1)


def _trunk(x, states, p, reset_first):
    mC, mn, mm, cv, hl, sh, S = states
    depth = p['ln1_g'].shape[0]
    alpha = (2 * depth) ** 0.25
    B, T, D = x.shape
    new_a, new_c = [], []
    for layer in range(depth):
        li = layer // 2
        if layer % 2 == 0:
            x, st = _layer_a(x, (mC[li], mn[li], mm[li], cv[li], hl[li]), p, li, reset_first, alpha)
            new_a.append(st)
        else:
            x, st = _layer_c(x, sh[li], S[li], p, li, layer, alpha)
            new_c.append(st)
        x = mlp_ln(x.reshape(B * T, D), p['mlp_w1'].astype(BF16), p['mlp_w2'].astype(BF16), layer,
                   p['ln2_g'][layer], p['ln2_b'][layer], alpha).reshape(B, T, D)
    sa = [jnp.stack([s[j] for s in new_a]) for j in range(5)]
    sc = [jnp.stack([s[j] for s in new_c]) for j in range(2)]
    return x, sa + sc


def kernel(x_prompt, x_sample, state_mlstm_C, state_mlstm_n, state_mlstm_m, state_lru_conv, state_lru_h,
           state_rwkv_shift, state_rwkv_S, a_w_in, a_b_ig, a_b_fg, a_mlstm_norm, a_conv_w, a_conv_b,
           a_lru_wa, a_lru_ba, a_lru_wx, a_lru_bx, a_lru_lambda, a_w_out, c_mu, c_w_r, c_w_k, c_w_v,
           c_w0, c_w1, c_w2, c_a0, c_a1, c_a2, c_g1, c_g2, c_k_k, c_k_a, c_r_k, c_gn_g, c_gn_b, c_w_o,
           ln1_g, ln1_b, ln2_g, ln2_b, mlp_w1, mlp_w2):
    p = dict(a_w_in=a_w_in, a_b_ig=a_b_ig, a_b_fg=a_b_fg, a_mlstm_norm=a_mlstm_norm, a_conv_w=a_conv_w,
             a_conv_b=a_conv_b, a_lru_wa=a_lru_wa, a_lru_ba=a_lru_ba, a_lru_wx=a_lru_wx, a_lru_bx=a_lru_bx,
             a_lru_lambda=a_lru_lambda, a_w_out=a_w_out, c_mu=c_mu, c_w_r=c_w_r, c_w_k=c_w_k, c_w_v=c_w_v,
             c_w0=c_w0, c_w1=c_w1, c_w2=c_w2, c_a0=c_a0, c_a1=c_a1, c_a2=c_a2, c_g1=c_g1, c_g2=c_g2,
             c_k_k=c_k_k, c_k_a=c_k_a, c_r_k=c_r_k, c_gn_g=c_gn_g, c_gn_b=c_gn_b, c_w_o=c_w_o,
             ln1_g=ln1_g, ln1_b=ln1_b, ln2_g=ln2_g, ln2_b=ln2_b, mlp_w1=mlp_w1, mlp_w2=mlp_w2)
    Bp = x_prompt.shape[0]
    init = tuple(jnp.zeros((s.shape[0], Bp) + s.shape[2:], s.dtype)
                 for s in (state_mlstm_C, state_mlstm_n, state_mlstm_m, state_lru_conv, state_lru_h,
                           state_rwkv_shift, state_rwkv_S))
    y_prompt, ps = _trunk(x_prompt, init, p, True)
    y_sample, ss = _trunk(x_sample, (state_mlstm_C, state_mlstm_n, state_mlstm_m, state_lru_conv,
                                     state_lru_h, state_rwkv_shift, state_rwkv_S), p, False)
    return (y_prompt, y_sample, *ps, *ss)
```
